```python
import jax, jax.numpy as jnp
from jax import lax
import numpy as np

D_MODEL = 2048
BATCH = 4
SEQ = 2048
DEPTH = 1

GRID_W = 64
MIX_WIDTH = D_MODEL
FOURIER_WIDTH = MIX_WIDTH // 4
FOURIER_GROUP_DIM = 128
FOURIER_GROUPS = FOURIER_WIDTH // FOURIER_GROUP_DIM
ATTN_WIDTH = MIX_WIDTH - FOURIER_WIDTH
HEAD_DIM = 128
N_Q_HEADS = ATTN_WIDTH // HEAD_DIM
GQA_GROUP = 3
N_KV_HEADS = N_Q_HEADS // GQA_GROUP
KV_WIDTH = N_KV_HEADS * HEAD_DIM
IN_WIDTH = FOURIER_WIDTH + ATTN_WIDTH + 2 * KV_WIDTH
ROPE_THETA = 10000.0
AXIS_ROPE_DIM = HEAD_DIM // 2
Q_BLOCK = 128
N_EXPERTS = 32
TOP_K = 4
D_FF_EXPERT = D_MODEL
SWIGLU_LIMIT = 7.0
SWIGLU_ALPHA = 1.702
EXPERT_BLOCK = 128
N_MOD = 6
EPS = 1e-6

kernel_name = 'hybrid_fnet_axialgqa_moe_block'


def rms_norm(x, g):
    xf = x.astype(jnp.float32)
    y = xf * lax.rsqrt(jnp.mean(xf * xf, axis=-1, keepdims=True) + EPS)
    return (y * g.astype(jnp.float32)).astype(x.dtype)


def modulate(h, shift, scale):
    return h * (1.0 + scale[:, None, :]) + shift[:, None, :]


def axial_rope_tables(seq_len):
    rows = seq_len // GRID_W
    row_idx = jnp.repeat(jnp.arange(rows, dtype=jnp.float32), GRID_W)
    col_idx = jnp.tile(jnp.arange(GRID_W, dtype=jnp.float32), rows)
    inv_freq = 1.0 / (ROPE_THETA ** (jnp.arange(0, AXIS_ROPE_DIM, 2, dtype=jnp.float32) / AXIS_ROPE_DIM))
    ang = jnp.concatenate([row_idx[:, None] * inv_freq, col_idx[:, None] * inv_freq], axis=-1)
    return jnp.cos(ang), jnp.sin(ang)


def apply_rope(x, cos, sin):
    xf = x.astype(jnp.float32).reshape(x.shape[:-1] + (HEAD_DIM // 2, 2))
    x0, x1 = xf[..., 0], xf[..., 1]
    c = cos[None, :, None, :]
    s = sin[None, :, None, :]
    out = jnp.stack([x0 * c - x1 * s, x0 * s + x1 * c], axis=-1).reshape(x.shape)
    return out.astype(x.dtype)


def fourier_mix(f, w_fourier):
    b, s, _ = f.shape
    fu = f.reshape(b, s, FOURIER_GROUPS, FOURIER_GROUP_DIM).astype(jnp.float32)
    fm = jnp.fft.fft2(fu, axes=(1, 3), norm='ortho').real.astype(f.dtype)
    fo = jnp.einsum('bsgc,gcd->bsgd', fm, w_fourier)
    return fo.reshape(b, s, FOURIER_WIDTH)


def block_attention(q, k, v):
    b, s = q.shape[0], q.shape[1]
    nb = s // Q_BLOCK
    qb = q.reshape(b, nb, Q_BLOCK, N_KV_HEADS, GQA_GROUP, HEAD_DIM).transpose(1, 0, 2, 3, 4, 5)
    scale = HEAD_DIM ** -0.5

    def one_block(q_blk):
        sc = jnp.einsum('bqkgd,bskd->bkgqs', q_blk, k).astype(jnp.float32) * scale
        p = jax.nn.softmax(sc, axis=-1).astype(v.dtype)
        return jnp.einsum('bkgqs,bskd->bqkgd', p, v)

    o = lax.map(one_block, qb)
    return o.transpose(1, 0, 2, 3, 4, 5).reshape(b, s, ATTN_WIDTH)


def clamped_swiglu(g, u):
    g = jnp.minimum(g, SWIGLU_LIMIT)
    u = jnp.clip(u, -SWIGLU_LIMIT, SWIGLU_LIMIT)
    return (u + 1.0) * (g * jax.nn.sigmoid(SWIGLU_ALPHA * g))


def moe_ffn(h, w_router, b_router, w_gate, b_gate, w_up, b_up, w_down, b_down):
    b, s, d = h.shape
    n_tok = b * s
    hf = h.reshape(n_tok, d)
    logits = (hf @ w_router).astype(jnp.float32) + b_router.astype(jnp.float32)
    top_val, top_idx = lax.top_k(logits, TOP_K)
    gates = jax.nn.softmax(top_val, axis=-1).astype(h.dtype)
    n_assign = n_tok * TOP_K
    flat_e = top_idx.reshape(-1).astype(jnp.int32)
    flat_tok = jnp.arange(n_assign, dtype=jnp.int32) // TOP_K
    flat_gate = gates.reshape(-1)
    order = jnp.argsort(flat_e)
    sorted_e = flat_e[order]
    counts = jnp.zeros((N_EXPERTS,), jnp.int32).at[flat_e].add(1)
    padded = (counts + EXPERT_BLOCK - 1) // EXPERT_BLOCK * EXPERT_BLOCK
    pad_end = jnp.cumsum(padded)
    pad_start = pad_end - padded
    start = jnp.cumsum(counts) - counts
    rank = jnp.arange(n_assign, dtype=jnp.int32) - start[sorted_e]
    dest = pad_start[sorted_e] + rank
    n_pad = n_assign + N_EXPERTS * EXPERT_BLOCK
    n_blk = n_pad // EXPERT_BLOCK
    row_tok = jnp.full((n_pad,), n_tok, jnp.int32).at[dest].set(flat_tok[order])
    row_gate = jnp.zeros((n_pad,), h.dtype).at[dest].set(flat_gate[order])
    blk_start = jnp.arange(n_blk, dtype=jnp.int32) * EXPERT_BLOCK
    blk_e = jnp.minimum(jnp.searchsorted(pad_end, blk_start, side='right'), N_EXPERTS - 1)
    x_pad = jnp.concatenate([hf, jnp.zeros((1, d), hf.dtype)], axis=0)

    def expert_block(args):
        tok, e = args
        xb = x_pad[tok]
        g = xb @ w_gate[e] + b_gate[e]
        u = xb @ w_up[e] + b_up[e]
        return clamped_swiglu(g, u) @ w_down[e] + b_down[e]

    out = lax.map(expert_block, (row_tok.reshape(n_blk, EXPERT_BLOCK), blk_e))
    out = out.reshape(n_pad, d) * row_gate[:, None]
    y = jnp.zeros((n_tok + 1, d), h.dtype).at[row_tok].add(out)[:n_tok]
    return y.reshape(b, s, d)


def setup_inputs(seed: int = 0) -> dict:
    key = jax.random.key(seed)
    ks = jax.random.split(key, 24)
    f32 = jnp.float32
    def nrm(k, shape, scale):
        return jax.random.normal(k, shape, f32) * scale
    def gain(k, shape):
        return 1.0 + 0.02 * jax.random.normal(k, shape, f32)
    L = DEPTH
    return {
        'x': nrm(ks[0], (BATCH, SEQ, D_MODEL), 1.0),
        'c': nrm(ks[1], (BATCH, D_MODEL), 1.0),
        'w_ada': nrm(ks[2], (L, D_MODEL, N_MOD * D_MODEL), 0.5 * D_MODEL ** -0.5),
        'b_ada': nrm(ks[3], (L, N_MOD * D_MODEL), 0.02),
        'g_pre_mix': gain(ks[4], (L, D_MODEL)),
        'w_in': nrm(ks[5], (L, D_MODEL, IN_WIDTH), D_MODEL ** -0.5),
        'w_fourier': nrm(ks[6], (L, FOURIER_GROUPS, FOURIER_GROUP_DIM, FOURIER_GROUP_DIM), FOURIER_GROUP_DIM ** -0.5),
        'q_norm_g': gain(ks[7], (L, HEAD_DIM)),
        'k_norm_g': gain(ks[8], (L, HEAD_DIM)),
        'g_fourier_out': gain(ks[9], (L, FOURIER_WIDTH)),
        'g_attn_out': gain(ks[10], (L, ATTN_WIDTH)),
        'w_out': nrm(ks[11], (L, MIX_WIDTH, D_MODEL), MIX_WIDTH ** -0.5),
        'g_post_mix': gain(ks[12], (L, D_MODEL)),
        'g_pre_ffn': gain(ks[13], (L, D_MODEL)),
        'w_router': nrm(ks[14], (L, D_MODEL, N_EXPERTS), D_MODEL ** -0.5),
        'b_router': nrm(ks[15], (L, N_EXPERTS), 0.01),
        'w_gate': nrm(ks[16], (L, N_EXPERTS, D_MODEL, D_FF_EXPERT), D_MODEL ** -0.5),
        'b_gate': nrm(ks[17], (L, N_EXPERTS, D_FF_EXPERT), 0.02),
        'w_up': nrm(ks[18], (L, N_EXPERTS, D_MODEL, D_FF_EXPERT), D_MODEL ** -0.5),
        'b_up': nrm(ks[19], (L, N_EXPERTS, D_FF_EXPERT), 0.02),
        'w_down': nrm(ks[20], (L, N_EXPERTS, D_FF_EXPERT, D_MODEL), D_FF_EXPERT ** -0.5),
        'b_down': nrm(ks[21], (L, N_EXPERTS, D_MODEL), 0.02),
        'g_post_ffn': gain(ks[22], (L, D_MODEL)),
    }


def reference(x, c, w_ada, b_ada, g_pre_mix, w_in, w_fourier, q_norm_g, k_norm_g, g_fourier_out, g_attn_out, w_out, g_post_mix, g_pre_ffn, w_router, b_router, w_gate, b_gate, w_up, b_up, w_down, b_down, g_post_ffn):
    b, s, _ = x.shape
    cos, sin = axial_rope_tables(s)
    for l in range(DEPTH):
        mod = (jax.nn.silu(c) @ w_ada[l] + b_ada[l]).reshape(b, N_MOD, D_MODEL)
        shift_m, scale_m, gate_m = mod[:, 0], mod[:, 1], mod[:, 2]
        shift_f, scale_f, gate_f = mod[:, 3], mod[:, 4], mod[:, 5]

        h = modulate(rms_norm(x, g_pre_mix[l]), shift_m, scale_m)
        proj = h @ w_in[l]
        f = proj[..., :FOURIER_WIDTH]
        q = proj[..., FOURIER_WIDTH:FOURIER_WIDTH + ATTN_WIDTH].reshape(b, s, N_Q_HEADS, HEAD_DIM)
        k = proj[..., FOURIER_WIDTH + ATTN_WIDTH:FOURIER_WIDTH + ATTN_WIDTH + KV_WIDTH].reshape(b, s, N_KV_HEADS, HEAD_DIM)
        v = proj[..., FOURIER_WIDTH + ATTN_WIDTH + KV_WIDTH:].reshape(b, s, N_KV_HEADS, HEAD_DIM)

        fo = fourier_mix(f, w_fourier[l])

        q = apply_rope(rms_norm(q, q_norm_g[l]), cos, sin)
        k = apply_rope(rms_norm(k, k_norm_g[l]), cos, sin)
        ao = block_attention(q, k, v)

        merged = jnp.concatenate([rms_norm(fo, g_fourier_out[l]), rms_norm(ao, g_attn_out[l])], axis=-1)
        mix = merged @ w_out[l]
        x = x + gate_m[:, None, :] * rms_norm(mix, g_post_mix[l])

        h = modulate(rms_norm(x, g_pre_ffn[l]), shift_f, scale_f)
        y = moe_ffn(h, w_router[l], b_router[l], w_gate[l], b_gate[l], w_up[l], b_up[l], w_down[l], b_down[l])
        x = x + gate_f[:, None, :] * rms_norm(y, g_post_ffn[l])
    return x
```

```python
import functools

import numpy as np
import jax
import jax.numpy as jnp
from jax import lax
from jax.experimental import pallas as pl
from jax.experimental.pallas import tpu as pltpu

F32 = jnp.float32
BF16 = jnp.bfloat16

D_MODEL = 2048
GRID_W = 64
FOURIER_WIDTH = 512
FOURIER_GROUP_DIM = 128
FOURIER_GROUPS = 4
ATTN_WIDTH = 1536
HEAD_DIM = 128
N_Q_HEADS = 12
GQA_GROUP = 3
N_KV_HEADS = 4
KV_WIDTH = 512
IN_WIDTH = 3072
ROPE_THETA = 10000.0
N_EXPERTS = 32
TOP_K = 4
D_FF = 2048
SWIGLU_LIMIT = 7.0
SWIGLU_ALPHA = 1.702
N_MOD = 6
EPS = 1e-6

VMEM_LIMIT_BYTES = 56 * 1024 * 1024
LANES = 128

ADA_TN = 1024
INPROJ_TM = 256
FOURIER_TR = 256
ATTN_TQ = 128
OUTPROJ_TM = 256
ROUTER_PAD = LANES
MOE_TM = 256
MOE_R = 5
MOE_RS = MOE_TM * MOE_R
MOE_TF = 256
GATHER_TG = 256
COMBINE_TT = 128


def _rms(x, g):
    return x * lax.rsqrt(jnp.mean(x * x, axis=-1, keepdims=True) + EPS) * g


def _ada_kernel(c_ref, w_ref, b_ref, o_ref):
    c = c_ref[...]
    s = (c * jax.nn.sigmoid(c)).astype(BF16)
    o_ref[...] = jnp.dot(s, w_ref[...].astype(BF16), preferred_element_type=F32) + b_ref[...]


def _ada(c_pad, w_ada, b_ada):
    m, d = c_pad.shape
    n = w_ada.shape[1]
    return pl.pallas_call(
        _ada_kernel,
        grid=(n // ADA_TN,),
        in_specs=[
            pl.BlockSpec((m, d), lambda j: (0, 0)),
            pl.BlockSpec((d, ADA_TN), lambda j: (0, j)),
            pl.BlockSpec((1, ADA_TN), lambda j: (0, j)),
        ],
        out_specs=pl.BlockSpec((m, ADA_TN), lambda j: (0, j)),
        out_shape=jax.ShapeDtypeStruct((m, n), F32),
        compiler_params=pltpu.CompilerParams(
            dimension_semantics=("arbitrary",), vmem_limit_bytes=VMEM_LIMIT_BYTES),
        name="ada",
    )(c_pad, w_ada, b_ada)


def _rope(p, cos, sin_signed, even_lane):
    partner = jnp.where(even_lane, pltpu.roll(p, LANES - 1, 1), pltpu.roll(p, 1, 1))
    return p * cos + partner * sin_signed


def _inproj_kernel(x_ref, g_ref, shift_ref, scale_ref, w_ref, qg_ref, kg_ref, cos_ref, sin_ref,
                   f_ref, q_ref, k_ref, v_ref):
    x = x_ref[...]
    h = _rms(x, g_ref[...]) * (1.0 + scale_ref[...]) + shift_ref[...]
    hb = h.astype(BF16)
    cos = cos_ref[...]
    sin = sin_ref[...]
    even_lane = (lax.broadcasted_iota(jnp.int32, cos.shape, 1) % 2) == 0
    chunk = 512
    heads_per_chunk = chunk // HEAD_DIM
    for ci in range(IN_WIDTH // chunk):
        p = jnp.dot(hb, w_ref[:, ci * chunk:(ci + 1) * chunk], preferred_element_type=F32)
        if ci == 0:
            f_ref[...] = p.astype(BF16)
        elif ci == 5:
            v_ref[...] = p.astype(BF16)
        else:
            gain = kg_ref[...] if ci == 4 else qg_ref[...]
            for hh in range(heads_per_chunk):
                ph = p[:, hh * HEAD_DIM:(hh + 1) * HEAD_DIM]
                ph = _rope(_rms(ph, gain), cos, sin, even_lane).astype(BF16)
                if ci == 4:
                    k_ref[:, hh * HEAD_DIM:(hh + 1) * HEAD_DIM] = ph
                else:
                    c0 = (ci - 1) * chunk + hh * HEAD_DIM
                    q_ref[:, c0:c0 + HEAD_DIM] = ph


def _inproj(x, g, shift, scale, w_in_b, qg, kg, cos, sin):
    b, s, d = x.shape
    tm = INPROJ_TM
    row = lambda bi, i: (bi, i, 0)
    per_batch = lambda bi, i: (bi, 0, 0)
    const2 = lambda bi, i: (0, 0)
    return pl.pallas_call(
        _inproj_kernel,
        grid=(b, s // tm),
        in_specs=[
            pl.BlockSpec((None, tm, d), row),
            pl.BlockSpec((1, d), const2),
            pl.BlockSpec((None, 1, d), per_batch),
            pl.BlockSpec((None, 1, d), per_batch),
            pl.BlockSpec((d, IN_WIDTH), const2),
            pl.BlockSpec((1, HEAD_DIM), const2),
            pl.BlockSpec((1, HEAD_DIM), const2),
            pl.BlockSpec((tm, HEAD_DIM), lambda bi, i: (i, 0)),
            pl.BlockSpec((tm, HEAD_DIM), lambda bi, i: (i, 0)),
        ],
        out_specs=[
            pl.BlockSpec((None, tm, FOURIER_WIDTH), row),
            pl.BlockSpec((None, tm, ATTN_WIDTH), row),
            pl.BlockSpec((None, tm, KV_WIDTH), row),
            pl.BlockSpec((None, tm, KV_WIDTH), row),
        ],
        out_shape=[
            jax.ShapeDtypeStruct((b, s, FOURIER_WIDTH), BF16),
            jax.ShapeDtypeStruct((b, s, ATTN_WIDTH), BF16),
            jax.ShapeDtypeStruct((b, s, KV_WIDTH), BF16),
            jax.ShapeDtypeStruct((b, s, KV_WIDTH), BF16),
        ],
        compiler_params=pltpu.CompilerParams(
            dimension_semantics=("arbitrary", "arbitrary"), vmem_limit_bytes=VMEM_LIMIT_BYTES),
        name="inproj",
    )(x, g, shift, scale, w_in_b, qg, kg, cos, sin)


def _fourier_kernel(cs_ref, ss_ref, f_ref, cc_ref, w_ref, g_ref, o_ref):
    f = f_ref[...]
    a = jnp.dot(cs_ref[...], f, preferred_element_type=F32)
    bm = jnp.dot(ss_ref[...], f, preferred_element_type=F32)
    cc = cc_ref[...]
    outs = []
    for gi in range(FOURIER_GROUPS):
        sl = slice(gi * FOURIER_GROUP_DIM, (gi + 1) * FOURIER_GROUP_DIM)
        ab = jnp.concatenate([a[:, sl], bm[:, sl]], axis=1).astype(BF16)
        fm = jnp.dot(ab, cc, preferred_element_type=F32)
        outs.append(jnp.dot(fm.astype(BF16), w_ref[gi].astype(BF16), preferred_element_type=F32))
    fo = jnp.concatenate(outs, axis=1)
    o_ref[...] = _rms(fo, g_ref[...]).astype(BF16)


def _fourier(f, cs, ss, cc, w_fourier, g):
    b, s, fw = f.shape
    tr = FOURIER_TR
    return pl.pallas_call(
        _fourier_kernel,
        grid=(b, s // tr),
        in_specs=[
            pl.BlockSpec((tr, s), lambda bi, i: (i, 0)),
            pl.BlockSpec((tr, s), lambda bi, i: (i, 0)),
            pl.BlockSpec((None, s, fw), lambda bi, i: (bi, 0, 0)),
            pl.BlockSpec((2 * FOURIER_GROUP_DIM, FOURIER_GROUP_DIM), lambda bi, i: (0, 0)),
            pl.BlockSpec((FOURIER_GROUPS, FOURIER_GROUP_DIM, FOURIER_GROUP_DIM), lambda bi, i: (0, 0, 0)),
            pl.BlockSpec((1, fw), lambda bi, i: (0, 0)),
        ],
        out_specs=pl.BlockSpec((None, tr, fw), lambda bi, i: (bi, i, 0)),
        out_shape=jax.ShapeDtypeStruct((b, s, fw), BF16),
        compiler_params=pltpu.CompilerParams(
            dimension_semantics=("arbitrary", "arbitrary"), vmem_limit_bytes=VMEM_LIMIT_BYTES),
        name="fourier",
    )(cs, ss, f, cc, w_fourier, g)


def _attn_kernel(q_ref, k_ref, v_ref, g_ref, o_ref, ao_ref):
    tq = q_ref.shape[0]
    for kv in range(N_KV_HEADS):
        kh = k_ref[:, kv * HEAD_DIM:(kv + 1) * HEAD_DIM]
        vh = v_ref[:, kv * HEAD_DIM:(kv + 1) * HEAD_DIM]
        c0 = kv * GQA_GROUP * HEAD_DIM
        qh = jnp.concatenate(
            [q_ref[:, c0 + gi * HEAD_DIM:c0 + (gi + 1) * HEAD_DIM] for gi in range(GQA_GROUP)], axis=0)
        sc = lax.dot_general(qh, kh, (((1,), (1,)), ((), ())), preferred_element_type=F32)
        m = jnp.max(sc, axis=-1, keepdims=True)
        p = jnp.exp(sc - m)
        l = jnp.sum(p, axis=-1, keepdims=True)
        o = jnp.dot(p.astype(BF16), vh, preferred_element_type=F32) / l
        for gi in range(GQA_GROUP):
            ao_ref[:, c0 + gi * HEAD_DIM:c0 + (gi + 1) * HEAD_DIM] = o[gi * tq:(gi + 1) * tq]
    o_ref[...] = _rms(ao_ref[...], g_ref[...]).astype(BF16)


def _attn(q, k, v, g):
    b, s, _ = q.shape
    tq = ATTN_TQ
    return pl.pallas_call(
        _attn_kernel,
        grid=(b, s // tq),
        in_specs=[
            pl.BlockSpec((None, tq, ATTN_WIDTH), lambda bi, i: (bi, i, 0)),
            pl.BlockSpec((None, s, KV_WIDTH), lambda bi, i: (bi, 0, 0)),
            pl.BlockSpec((None, s, KV_WIDTH), lambda bi, i: (bi, 0, 0)),
            pl.BlockSpec((1, ATTN_WIDTH), lambda bi, i: (0, 0)),
        ],
        out_specs=pl.BlockSpec((None, tq, ATTN_WIDTH), lambda bi, i: (bi, i, 0)),
        out_shape=jax.ShapeDtypeStruct((b, s, ATTN_WIDTH), BF16),
        scratch_shapes=[pltpu.VMEM((tq, ATTN_WIDTH), F32)],
        compiler_params=pltpu.CompilerParams(
            dimension_semantics=("arbitrary", "arbitrary"), vmem_limit_bytes=VMEM_LIMIT_BYTES),
        name="attn",
    )(q, k, v, g)


def _split_bf16(a):
    hi = a.astype(BF16)
    lo = (a - hi.astype(F32)).astype(BF16)
    return hi, lo


def _outproj_kernel(fo_ref, ao_ref, w_ref, x_ref, gpost_ref, gate_ref, gpre_ref, shift_ref, scale_ref,
                    wr_hi_ref, wr_lo_ref, br_ref, x1_ref, h2_ref, logit_ref):
    mix = jnp.dot(fo_ref[...], w_ref[:FOURIER_WIDTH, :], preferred_element_type=F32)
    mix = mix + jnp.dot(ao_ref[...], w_ref[FOURIER_WIDTH:, :], preferred_element_type=F32)
    x1 = x_ref[...] + gate_ref[...] * _rms(mix, gpost_ref[...])
    x1_ref[...] = x1
    h2 = _rms(x1, gpre_ref[...]) * (1.0 + scale_ref[...]) + shift_ref[...]
    h2_ref[...] = h2
    h_hi, h_lo = _split_bf16(h2)
    lg = jnp.dot(h_hi, wr_hi_ref[...], preferred_element_type=F32)
    lg = lg + jnp.dot(h_hi, wr_lo_ref[...], preferred_element_type=F32)
    lg = lg + jnp.dot(h_lo, wr_hi_ref[...], preferred_element_type=F32)
    logit_ref[...] = lg + br_ref[...]


def _outproj(fo, ao, w_out_b, x, gpost, gate, gpre, shift, scale, wr_hi, wr_lo, br):
    b, s, d = x.shape
    tm = OUTPROJ_TM
    row = lambda bi, i: (bi, i, 0)
    per_batch = lambda bi, i: (bi, 0, 0)
    const2 = lambda bi, i: (0, 0)
    return pl.pallas_call(
        _outproj_kernel,
        grid=(b, s // tm),
        in_specs=[
            pl.BlockSpec((None, tm, FOURIER_WIDTH), row),
            pl.BlockSpec((None, tm, ATTN_WIDTH), row),
            pl.BlockSpec((d, d), const2),
            pl.BlockSpec((None, tm, d), row),
            pl.BlockSpec((1, d), const2),
            pl.BlockSpec((None, 1, d), per_batch),
            pl.BlockSpec((1, d), const2),
            pl.BlockSpec((None, 1, d), per_batch),
            pl.BlockSpec((None, 1, d), per_batch),
            pl.BlockSpec((d, ROUTER_PAD), const2),
            pl.BlockSpec((d, ROUTER_PAD), const2),
            pl.BlockSpec((1, ROUTER_PAD), const2),
        ],
        out_specs=[
            pl.BlockSpec((None, tm, d), row),
            pl.BlockSpec((None, tm, d), row),
            pl.BlockSpec((None, tm, ROUTER_PAD), row),
        ],
        out_shape=[
            jax.ShapeDtypeStruct((b, s, d), F32),
            jax.ShapeDtypeStruct((b, s, d), F32),
            jax.ShapeDtypeStruct((b, s, ROUTER_PAD), F32),
        ],
        compiler_params=pltpu.CompilerParams(
            dimension_semantics=("arbitrary", "arbitrary"), vmem_limit_bytes=VMEM_LIMIT_BYTES),
        name="outproj",
    )(fo, ao, w_out_b, x, gpost, gate, gpre, shift, scale, wr_hi, wr_lo, br)


def _gather_kernel(tok_ref, h_hbm, o_ref, sem):
    base = pl.program_id(0) * GATHER_TG

    def row_copy(r):
        return pltpu.make_async_copy(
            h_hbm.at[pl.ds(tok_ref[base + r], 1)], o_ref.at[pl.ds(r, 1)], sem)

    def start(r, c):
        row_copy(r).start()
        return c

    def wait(r, c):
        row_copy(r).wait()
        return c

    lax.fori_loop(0, GATHER_TG, start, 0)
    lax.fori_loop(0, GATHER_TG, wait, 0)


def _gather(row_tok, h2, n_rows):
    _, d = h2.shape
    return pl.pallas_call(
        _gather_kernel,
        grid_spec=pltpu.PrefetchScalarGridSpec(
            num_scalar_prefetch=1,
            grid=(n_rows // GATHER_TG,),
            in_specs=[pl.BlockSpec(memory_space=pl.ANY)],
            out_specs=pl.BlockSpec((GATHER_TG, d), lambda i, tok: (i, 0)),
            scratch_shapes=[pltpu.SemaphoreType.DMA],
        ),
        out_shape=jax.ShapeDtypeStruct((n_rows, d), h2.dtype),
        compiler_params=pltpu.CompilerParams(
            dimension_semantics=("arbitrary",), vmem_limit_bytes=VMEM_LIMIT_BYTES),
        name="gather",
    )(row_tok, h2)


def _experts_kernel(item_e_ref, item_row0_ref, item_nt_ref, item_nz_ref,
                    xs_hbm, wg_ref, bg_ref, wu_ref, bu_ref, wd_ref, bd_ref, out_hbm,
                    x_buf, acc, wg_b, wu_b, wd_b, sem):
    w = pl.program_id(0)
    j = pl.program_id(1)
    nt = item_nt_ref[w]
    nz = item_nz_ref[w]
    row0 = pl.multiple_of(item_row0_ref[w], MOE_TM)

    def x_copy(t):
        return pltpu.make_async_copy(
            xs_hbm.at[pl.ds(row0 + t * MOE_TM, MOE_TM)], x_buf.at[t], sem)

    def out_copy(t):
        return pltpu.make_async_copy(
            acc.at[t], out_hbm.at[pl.ds(row0 + t * MOE_TM, MOE_TM)], sem)

    @pl.when(jnp.logical_and(j == 0, nt > 0))
    def _load_rows():
        for t in range(MOE_R):
            @pl.when(t < nt)
            def _():
                x_copy(t).start()
        for t in range(MOE_R):
            @pl.when(t < nt)
            def _():
                x_copy(t).wait()

    @pl.when(nt > 0)
    def _compute():
        wg_b[...] = wg_ref[...].astype(BF16)
        wu_b[...] = wu_ref[...].astype(BF16)
        wd_b[...] = wd_ref[...].astype(BF16)
        bg = bg_ref[...]
        bu = bu_ref[...]

        def tile(t, carry):
            x = x_buf[t].astype(BF16)
            g = jnp.dot(x, wg_b[...], preferred_element_type=F32) + bg
            u = jnp.dot(x, wu_b[...], preferred_element_type=F32) + bu
            g = jnp.minimum(g, SWIGLU_LIMIT)
            u = jnp.clip(u, -SWIGLU_LIMIT, SWIGLU_LIMIT)
            hidden = ((u + 1.0) * (g * jax.nn.sigmoid(SWIGLU_ALPHA * g))).astype(BF16)
            contrib = jnp.dot(hidden, wd_b[...], preferred_element_type=F32)

            @pl.when(j == 0)
            def _():
                acc[t] = contrib + bd_ref[...]

            @pl.when(j > 0)
            def _():
                acc[t] = acc[t] + contrib

            return carry

        lax.fori_loop(0, nt, tile, 0)

    @pl.when(jnp.logical_and(j == 0, nz > 0))
    def _zero_tail():
        for t in range(MOE_R):
            @pl.when(t < nz)
            def _():
                acc[t] = jnp.zeros(acc.shape[1:], F32)

    n_out = jnp.where(j == pl.num_programs(1) - 1, nt, 0) + jnp.where(j == 0, nz, 0)

    @pl.when(n_out > 0)
    def _store_rows():
        for t in range(MOE_R):
            @pl.when(t < n_out)
            def _():
                out_copy(t).start()
        for t in range(MOE_R):
            @pl.when(t < n_out)
            def _():
                out_copy(t).wait()


def _experts(item_e, item_row0, item_nt, item_nz, xs, w_gate, b_gate, w_up, b_up, w_down, b_down, n_items):
    n_rows, d = xs.shape
    n_ff = D_FF // MOE_TF
    last = n_ff - 1

    def ff_idx(w, j, nt):
        return jnp.where(nt[w] > 0, j, last)

    col_w = lambda w, j, ie, r0, nt, nz:(ie[w], 0, ff_idx(w, j, nt))
    row_w = lambda w, j, ie, r0, nt, nz:(ie[w], ff_idx(w, j, nt), 0)
    full_b = lambda w, j, ie, r0, nt, nz:(ie[w], 0, 0)
    return pl.pallas_call(
        _experts_kernel,
        grid_spec=pltpu.PrefetchScalarGridSpec(
            num_scalar_prefetch=4,
            grid=(n_items, n_ff),
            in_specs=[
                pl.BlockSpec(memory_space=pl.ANY),
                pl.BlockSpec((None, d, MOE_TF), col_w),
                pl.BlockSpec((None, 1, MOE_TF), col_w),
                pl.BlockSpec((None, d, MOE_TF), col_w),
                pl.BlockSpec((None, 1, MOE_TF), col_w),
                pl.BlockSpec((None, MOE_TF, d), row_w),
                pl.BlockSpec((None, 1, d), full_b),
            ],
            out_specs=pl.BlockSpec(memory_space=pl.ANY),
            scratch_shapes=[
                pltpu.VMEM((MOE_R, MOE_TM, d), F32),
                pltpu.VMEM((MOE_R, MOE_TM, d), F32),
                pltpu.VMEM((d, MOE_TF), BF16),
                pltpu.VMEM((d, MOE_TF), BF16),
                pltpu.VMEM((MOE_TF, d), BF16),
                pltpu.SemaphoreType.DMA,
            ],
        ),
        out_shape=jax.ShapeDtypeStruct((n_rows, d), F32),
        compiler_params=pltpu.CompilerParams(
            dimension_semantics=("arbitrary", "arbitrary"), vmem_limit_bytes=VMEM_LIMIT_BYTES),
        name="experts",
    )(item_e, item_row0, item_nt, item_nz, xs, w_gate, b_gate, w_up, b_up, w_down, b_down)


def _combine_kernel(dest_ref, rows_hbm, gates_ref, x1_ref, gpost_ref, gate_ref, o_ref, buf, sem):
    tt = COMBINE_TT
    base = (pl.program_id(0) * pl.num_programs(1) + pl.program_id(1)) * tt * TOP_K

    def row_copy(i):
        return pltpu.make_async_copy(
            rows_hbm.at[pl.ds(dest_ref[base + i], 1)],
            buf.at[i % TOP_K, pl.ds(i // TOP_K, 1)], sem)

    def start(i, c):
        row_copy(i).start()
        return c

    def wait(i, c):
        row_copy(i).wait()
        return c

    lax.fori_loop(0, tt * TOP_K, start, 0)
    lax.fori_loop(0, tt * TOP_K, wait, 0)
    gates = gates_ref[...]
    y = gates[:, 0:1] * buf[0]
    for kk in range(1, TOP_K):
        y = y + gates[:, kk:kk + 1] * buf[kk]
    o_ref[...] = x1_ref[...] + gate_ref[...] * _rms(y, gpost_ref[...])


def _combine(dest, rows, gates, x1, gpost, gate):
    b, s, d = x1.shape
    tt = COMBINE_TT
    row = lambda bi, i, dst: (bi, i, 0)
    return pl.pallas_call(
        _combine_kernel,
        grid_spec=pltpu.PrefetchScalarGridSpec(
            num_scalar_prefetch=1,
            grid=(b, s // tt),
            in_specs=[
                pl.BlockSpec(memory_space=pl.ANY),
                pl.BlockSpec((None, tt, TOP_K), row),
                pl.BlockSpec((None, tt, d), row),
                pl.BlockSpec((1, d), lambda bi, i, dst: (0, 0)),
                pl.BlockSpec((None, 1, d), lambda bi, i, dst: (bi, 0, 0)),
            ],
            out_specs=pl.BlockSpec((None, tt, d), row),
            scratch_shapes=[pltpu.VMEM((TOP_K, tt, d), F32), pltpu.SemaphoreType.DMA],
        ),
        out_shape=jax.ShapeDtypeStruct((b, s, d), F32),
        compiler_params=pltpu.CompilerParams(
            dimension_semantics=("arbitrary", "arbitrary"), vmem_limit_bytes=VMEM_LIMIT_BYTES),
        name="combine",
    )(dest, rows, gates, x1, gpost, gate)


def _rope_tables(seq_len):
    rows = seq_len // GRID_W
    row_idx = jnp.repeat(jnp.arange(rows, dtype=F32), GRID_W)
    col_idx = jnp.tile(jnp.arange(GRID_W, dtype=F32), rows)
    half = HEAD_DIM // 2
    inv_freq = 1.0 / (ROPE_THETA ** (jnp.arange(0, half, 2, dtype=F32) / half))
    ang = jnp.concatenate([row_idx[:, None] * inv_freq, col_idx[:, None] * inv_freq], axis=-1)
    cos = jnp.repeat(jnp.cos(ang), 2, axis=-1)
    sin = jnp.sin(ang)
    sin_signed = jnp.stack([-sin, sin], axis=-1).reshape(seq_len, HEAD_DIM)
    return cos, sin_signed


def _dft_tables(n):
    idx = np.arange(n, dtype=np.int64)
    ang = 2.0 * np.pi * ((idx[:, None] * idx[None, :]) % n).astype(np.float64) / n
    scale = 1.0 / np.sqrt(n)
    return np.cos(ang) * scale, np.sin(ang) * scale


def _route(logits, n_tok):
    top_val, top_idx = lax.top_k(logits, TOP_K)
    gates = jax.nn.softmax(top_val, axis=-1)
    flat_e = top_idx.reshape(-1).astype(jnp.int32)
    onehot = (flat_e[:, None] == jnp.arange(N_EXPERTS, dtype=jnp.int32)[None, :]).astype(jnp.int32)
    csum = jnp.cumsum(onehot, axis=0)
    rank = jnp.sum(csum * onehot, axis=1) - 1
    counts = csum[-1]
    tiles = (counts + MOE_TM - 1) // MOE_TM
    padded = tiles * MOE_TM
    pad_end = jnp.cumsum(padded)
    pad_start = pad_end - padded
    dest = pad_start[flat_e] + rank
    n_rows = n_tok * TOP_K + N_EXPERTS * MOE_TM
    flat_tok = jnp.arange(n_tok * TOP_K, dtype=jnp.int32) // TOP_K
    row_tok = jnp.zeros((n_rows,), jnp.int32).at[dest].set(flat_tok)
    n_items = N_EXPERTS + n_rows // MOE_RS
    items_per_e = (tiles + MOE_R - 1) // MOE_R
    item_end = jnp.cumsum(items_per_e)
    item_start = item_end - items_per_e
    w = jnp.arange(n_items, dtype=jnp.int32)
    total = item_end[-1]
    e_of_w = jnp.minimum(jnp.searchsorted(item_end, w, side='right'), N_EXPERTS - 1).astype(jnp.int32)
    local = w - item_start[e_of_w]
    active = w < total
    last_e = e_of_w[jnp.maximum(total - 1, 0)]
    item_e = jnp.where(active, e_of_w, last_e).astype(jnp.int32)
    item_nt = jnp.where(active, jnp.minimum(MOE_R, tiles[e_of_w] - local * MOE_R), 0).astype(jnp.int32)
    n_tiles = n_rows // MOE_TM
    fill_tile0 = jnp.sum(tiles) + (w - total) * MOE_R
    item_nz = jnp.where(active, 0, jnp.clip(n_tiles - fill_tile0, 0, MOE_R)).astype(jnp.int32)
    fill_row0 = jnp.minimum(fill_tile0, n_tiles - 1) * MOE_TM
    item_row0 = jnp.where(active, pad_start[e_of_w] + local * MOE_RS, fill_row0).astype(jnp.int32)
    return gates, dest.astype(jnp.int32), row_tok, item_e, item_row0, item_nt, item_nz, n_rows, n_items


def kernel(x, c, w_ada, b_ada, g_pre_mix, w_in, w_fourier, q_norm_g, k_norm_g, g_fourier_out, g_attn_out,
           w_out, g_post_mix, g_pre_ffn, w_router, b_router, w_gate, b_gate, w_up, b_up, w_down, b_down,
           g_post_ffn):
    b, s, d = x.shape
    n_tok = b * s
    depth = w_ada.shape[0]
    cos, sin_signed = _rope_tables(s)
    cs_np, ss_np = _dft_tables(s)
    cs = jnp.asarray(cs_np, dtype=BF16)
    ss = jnp.asarray(ss_np, dtype=BF16)
    cc_np, sc_np = _dft_tables(FOURIER_GROUP_DIM)
    cc = jnp.asarray(np.concatenate([cc_np, -sc_np], axis=0), dtype=BF16)
    c_pad = jnp.pad(c, ((0, 8 - b), (0, 0)))
    row2 = lambda a: a.reshape(1, -1)

    for l in range(depth):
        mod = _ada(c_pad, w_ada[l], row2(b_ada[l]))[:b].reshape(b, N_MOD, 1, d)
        shift_m, scale_m, gate_m = mod[:, 0], mod[:, 1], mod[:, 2]
        shift_f, scale_f, gate_f = mod[:, 3], mod[:, 4], mod[:, 5]

        f, q, k, v = _inproj(
            x, row2(g_pre_mix[l]), shift_m, scale_m, w_in[l].astype(BF16),
            row2(q_norm_g[l] * (HEAD_DIM ** -0.5)), row2(k_norm_g[l]), cos, sin_signed)
        fo = _fourier(f, cs, ss, cc, w_fourier[l], row2(g_fourier_out[l]))
        ao = _attn(q, k, v, row2(g_attn_out[l]))

        wr = jnp.pad(w_router[l], ((0, 0), (0, ROUTER_PAD - N_EXPERTS)))
        wr_hi = wr.astype(BF16)
        wr_lo = (wr - wr_hi.astype(F32)).astype(BF16)
        br = jnp.pad(b_router[l], (0, ROUTER_PAD - N_EXPERTS)).reshape(1, ROUTER_PAD)
        x1, h2, logits = _outproj(
            fo, ao, w_out[l].astype(BF16), x, row2(g_post_mix[l]), gate_m, row2(g_pre_ffn[l]),
            shift_f, scale_f, wr_hi, wr_lo, br)

        logits = logits.reshape(n_tok, ROUTER_PAD)[:, :N_EXPERTS]
        gates, dest, row_tok, item_e, item_row0, item_nt, item_nz, n_rows, n_items = _route(logits, n_tok)
        xs = _gather(row_tok, h2.reshape(n_tok, d), n_rows)
        rows = _experts(
            item_e, item_row0, item_nt, item_nz, xs,
            w_gate[l], b_gate[l].reshape(N_EXPERTS, 1, D_FF), w_up[l], b_up[l].reshape(N_EXPERTS, 1, D_FF),
            w_down[l], b_down[l].reshape(N_EXPERTS, 1, d), n_items)
        x = _combine(dest, rows, gates.reshape(b, s, TOP_K), x1, row2(g_post_ffn[l]), gate_f)
    return x
```

```python
import functools

import numpy as np
import jax
import jax.numpy as jnp
from jax import lax
from jax.experimental import pallas as pl
from jax.experimental.pallas import tpu as pltpu

F32 = jnp.float32
BF16 = jnp.bfloat16
U32 = jnp.uint32

D_MODEL = 2048
GRID_W = 64
FOURIER_WIDTH = 512
FOURIER_GROUP_DIM = 128
FOURIER_GROUPS = 4
ATTN_WIDTH = 1536
HEAD_DIM = 128
N_Q_HEADS = 12
GQA_GROUP = 3
N_KV_HEADS = 4
KV_WIDTH = 512
IN_WIDTH = 3072
ROPE_THETA = 10000.0
N_EXPERTS = 32
TOP_K = 4
D_FF = 2048
SWIGLU_LIMIT = 7.0
SWIGLU_ALPHA = 1.702
N_MOD = 6
EPS = 1e-6

VMEM_LIMIT_BYTES = 56 * 1024 * 1024
LANES = 128

ADA_TN = 1024
INPROJ_TM = 256
FOURIER_TR = 256
ATTN_TQ = 128
OUTPROJ_TM = 256
ROUTER_PAD = LANES
MOE_TM = 256
MOE_R = 5
MOE_RS = MOE_TM * MOE_R
MOE_TF = 256
DISPATCH_TT = 512
COMBINE_TT = 128
PACK_HALF = D_MODEL // 2
PACK_SLABS = PACK_HALF // LANES


def _rms(x, g):
    return x * lax.rsqrt(jnp.mean(x * x, axis=-1, keepdims=True) + EPS) * g


def _ada_kernel(c_ref, w_ref, b_ref, o_ref):
    c = c_ref[...]
    s = (c * jax.nn.sigmoid(c)).astype(BF16)
    o_ref[...] = jnp.dot(s, w_ref[...].astype(BF16), preferred_element_type=F32) + b_ref[...]


def _ada(c_pad, w_ada, b_ada):
    m, d = c_pad.shape
    n = w_ada.shape[1]
    return pl.pallas_call(
        _ada_kernel,
        grid=(n // ADA_TN,),
        in_specs=[
            pl.BlockSpec((m, d), lambda j: (0, 0)),
            pl.BlockSpec((d, ADA_TN), lambda j: (0, j)),
            pl.BlockSpec((1, ADA_TN), lambda j: (0, j)),
        ],
        out_specs=pl.BlockSpec((m, ADA_TN), lambda j: (0, j)),
        out_shape=jax.ShapeDtypeStruct((m, n), F32),
        compiler_params=pltpu.CompilerParams(
            dimension_semantics=("arbitrary",), vmem_limit_bytes=VMEM_LIMIT_BYTES),
        name="ada",
    )(c_pad, w_ada, b_ada)


def _rope(p, cos, sin_signed, even_lane):
    partner = jnp.where(even_lane, pltpu.roll(p, LANES - 1, 1), pltpu.roll(p, 1, 1))
    return p * cos + partner * sin_signed


def _inproj_kernel(x_ref, g_ref, shift_ref, scale_ref, w_ref, qg_ref, kg_ref, cos_ref, sin_ref,
                   f_ref, q_ref, k_ref, v_ref):
    x = x_ref[...]
    h = _rms(x, g_ref[...]) * (1.0 + scale_ref[...]) + shift_ref[...]
    hb = h.astype(BF16)
    cos = cos_ref[...]
    sin = sin_ref[...]
    even_lane = (lax.broadcasted_iota(jnp.int32, cos.shape, 1) % 2) == 0
    chunk = 512
    heads_per_chunk = chunk // HEAD_DIM
    for ci in range(IN_WIDTH // chunk):
        p = jnp.dot(hb, w_ref[:, ci * chunk:(ci + 1) * chunk], preferred_element_type=F32)
        if ci == 0:
            f_ref[...] = p.astype(BF16)
        elif ci == 5:
            v_ref[...] = p.astype(BF16)
        else:
            gain = kg_ref[...] if ci == 4 else qg_ref[...]
            for hh in range(heads_per_chunk):
                ph = p[:, hh * HEAD_DIM:(hh + 1) * HEAD_DIM]
                ph = _rope(_rms(ph, gain), cos, sin, even_lane).astype(BF16)
                if ci == 4:
                    k_ref[:, hh * HEAD_DIM:(hh + 1) * HEAD_DIM] = ph
                else:
                    c0 = (ci - 1) * chunk + hh * HEAD_DIM
                    q_ref[:, c0:c0 + HEAD_DIM] = ph


def _inproj(x, g, shift, scale, w_in_b, qg, kg, cos, sin):
    b, s, d = x.shape
    tm = INPROJ_TM
    row = lambda bi, i: (bi, i, 0)
    per_batch = lambda bi, i: (bi, 0, 0)
    const2 = lambda bi, i: (0, 0)
    return pl.pallas_call(
        _inproj_kernel,
        grid=(b, s // tm),
        in_specs=[
            pl.BlockSpec((None, tm, d), row),
            pl.BlockSpec((1, d), const2),
            pl.BlockSpec((None, 1, d), per_batch),
            pl.BlockSpec((None, 1, d), per_batch),
            pl.BlockSpec((d, IN_WIDTH), const2),
            pl.BlockSpec((1, HEAD_DIM), const2),
            pl.BlockSpec((1, HEAD_DIM), const2),
            pl.BlockSpec((tm, HEAD_DIM), lambda bi, i: (i, 0)),
            pl.BlockSpec((tm, HEAD_DIM), lambda bi, i: (i, 0)),
        ],
        out_specs=[
            pl.BlockSpec((None, tm, FOURIER_WIDTH), row),
            pl.BlockSpec((None, tm, ATTN_WIDTH), row),
            pl.BlockSpec((None, tm, KV_WIDTH), row),
            pl.BlockSpec((None, tm, KV_WIDTH), row),
        ],
        out_shape=[
            jax.ShapeDtypeStruct((b, s, FOURIER_WIDTH), BF16),
            jax.ShapeDtypeStruct((b, s, ATTN_WIDTH), BF16),
            jax.ShapeDtypeStruct((b, s, KV_WIDTH), BF16),
            jax.ShapeDtypeStruct((b, s, KV_WIDTH), BF16),
        ],
        compiler_params=pltpu.CompilerParams(
            dimension_semantics=("arbitrary", "arbitrary"), vmem_limit_bytes=VMEM_LIMIT_BYTES),
        name="inproj",
    )(x, g, shift, scale, w_in_b, qg, kg, cos, sin)


def _fourier_kernel(cs_ref, ss_ref, f_ref, cc_ref, w_ref, g_ref, o_ref):
    f = f_ref[...]
    a = jnp.dot(cs_ref[...], f, preferred_element_type=F32)
    bm = jnp.dot(ss_ref[...], f, preferred_element_type=F32)
    cc = cc_ref[...]
    outs = []
    for gi in range(FOURIER_GROUPS):
        sl = slice(gi * FOURIER_GROUP_DIM, (gi + 1) * FOURIER_GROUP_DIM)
        ab = jnp.concatenate([a[:, sl], bm[:, sl]], axis=1).astype(BF16)
        fm = jnp.dot(ab, cc, preferred_element_type=F32)
        outs.append(jnp.dot(fm.astype(BF16), w_ref[gi].astype(BF16), preferred_element_type=F32))
    fo = jnp.concatenate(outs, axis=1)
    o_ref[...] = _rms(fo, g_ref[...]).astype(BF16)


def _fourier(f, cs, ss, cc, w_fourier, g):
    b, s, fw = f.shape
    tr = FOURIER_TR
    return pl.pallas_call(
        _fourier_kernel,
        grid=(b, s // tr),
        in_specs=[
            pl.BlockSpec((tr, s), lambda bi, i: (i, 0)),
            pl.BlockSpec((tr, s), lambda bi, i: (i, 0)),
            pl.BlockSpec((None, s, fw), lambda bi, i: (bi, 0, 0)),
            pl.BlockSpec((2 * FOURIER_GROUP_DIM, FOURIER_GROUP_DIM), lambda bi, i: (0, 0)),
            pl.BlockSpec((FOURIER_GROUPS, FOURIER_GROUP_DIM, FOURIER_GROUP_DIM), lambda bi, i: (0, 0, 0)),
            pl.BlockSpec((1, fw), lambda bi, i: (0, 0)),
        ],
        out_specs=pl.BlockSpec((None, tr, fw), lambda bi, i: (bi, i, 0)),
        out_shape=jax.ShapeDtypeStruct((b, s, fw), BF16),
        compiler_params=pltpu.CompilerParams(
            dimension_semantics=("arbitrary", "arbitrary"), vmem_limit_bytes=VMEM_LIMIT_BYTES),
        name="fourier",
    )(cs, ss, f, cc, w_fourier, g)


def _attn_kernel(q_ref, k_ref, v_ref, g_ref, o_ref, ao_ref):
    tq = q_ref.shape[0]
    for kv in range(N_KV_HEADS):
        kh = k_ref[:, kv * HEAD_DIM:(kv + 1) * HEAD_DIM]
        vh = v_ref[:, kv * HEAD_DIM:(kv + 1) * HEAD_DIM]
        c0 = kv * GQA_GROUP * HEAD_DIM
        qh = jnp.concatenate(
            [q_ref[:, c0 + gi * HEAD_DIM:c0 + (gi + 1) * HEAD_DIM] for gi in range(GQA_GROUP)], axis=0)
        sc = lax.dot_general(qh, kh, (((1,), (1,)), ((), ())), preferred_element_type=F32)
        m = jnp.max(sc, axis=-1, keepdims=True)
        p = jnp.exp(sc - m)
        l = jnp.sum(p, axis=-1, keepdims=True)
        o = jnp.dot(p.astype(BF16), vh, preferred_element_type=F32) / l
        for gi in range(GQA_GROUP):
            ao_ref[:, c0 + gi * HEAD_DIM:c0 + (gi + 1) * HEAD_DIM] = o[gi * tq:(gi + 1) * tq]
    o_ref[...] = _rms(ao_ref[...], g_ref[...]).astype(BF16)


def _attn(q, k, v, g):
    b, s, _ = q.shape
    tq = ATTN_TQ
    return pl.pallas_call(
        _attn_kernel,
        grid=(b, s // tq),
        in_specs=[
            pl.BlockSpec((None, tq, ATTN_WIDTH), lambda bi, i: (bi, i, 0)),
            pl.BlockSpec((None, s, KV_WIDTH), lambda bi, i: (bi, 0, 0)),
            pl.BlockSpec((None, s, KV_WIDTH), lambda bi, i: (bi, 0, 0)),
            pl.BlockSpec((1, ATTN_WIDTH), lambda bi, i: (0, 0)),
        ],
        out_specs=pl.BlockSpec((None, tq, ATTN_WIDTH), lambda bi, i: (bi, i, 0)),
        out_shape=jax.ShapeDtypeStruct((b, s, ATTN_WIDTH), BF16),
        scratch_shapes=[pltpu.VMEM((tq, ATTN_WIDTH), F32)],
        compiler_params=pltpu.CompilerParams(
            dimension_semantics=("arbitrary", "arbitrary"), vmem_limit_bytes=VMEM_LIMIT_BYTES),
        name="attn",
    )(q, k, v, g)


def _split_bf16(a):
    hi = a.astype(BF16)
    lo = (a - hi.astype(F32)).astype(BF16)
    return hi, lo


def _pack_rows(val, dst_ref):
    n = val.shape[0]
    bits = lax.bitcast_convert_type(val.astype(BF16).astype(F32), U32)
    packed = bits[:, :PACK_HALF] | (bits[:, PACK_HALF:] >> 16)
    for a in range(PACK_SLABS):
        dst_ref[pl.ds(a, n, stride=PACK_SLABS), :] = packed[:, a * LANES:(a + 1) * LANES]


def _unpack_slab(src_ref, a, n):
    word = src_ref[pl.ds(a, n, stride=PACK_SLABS), :]
    hi = lax.bitcast_convert_type(word & jnp.uint32(0xFFFF0000), F32)
    lo = lax.bitcast_convert_type(word << 16, F32)
    return hi, lo


def _outproj_kernel(fo_ref, ao_ref, w_ref, x_ref, gpost_ref, gate_ref, gpre_ref, shift_ref, scale_ref,
                    wr_hi_ref, wr_lo_ref, br_ref, x1_ref, h2p_ref, logit_ref):
    mix = jnp.dot(fo_ref[...], w_ref[:FOURIER_WIDTH, :], preferred_element_type=F32)
    mix = mix + jnp.dot(ao_ref[...], w_ref[FOURIER_WIDTH:, :], preferred_element_type=F32)
    x1 = x_ref[...] + gate_ref[...] * _rms(mix, gpost_ref[...])
    x1_ref[...] = x1
    h2 = _rms(x1, gpre_ref[...]) * (1.0 + scale_ref[...]) + shift_ref[...]
    _pack_rows(h2, h2p_ref)
    h_hi, h_lo = _split_bf16(h2)
    lg = jnp.dot(h_hi, wr_hi_ref[...], preferred_element_type=F32)
    lg = lg + jnp.dot(h_hi, wr_lo_ref[...], preferred_element_type=F32)
    lg = lg + jnp.dot(h_lo, wr_hi_ref[...], preferred_element_type=F32)
    logit_ref[...] = lg + br_ref[...]


def _outproj(fo, ao, w_out_b, x, gpost, gate, gpre, shift, scale, wr_hi, wr_lo, br):
    b, s, d = x.shape
    tm = OUTPROJ_TM
    row = lambda bi, i: (bi, i, 0)
    per_batch = lambda bi, i: (bi, 0, 0)
    const2 = lambda bi, i: (0, 0)
    return pl.pallas_call(
        _outproj_kernel,
        grid=(b, s // tm),
        in_specs=[
            pl.BlockSpec((None, tm, FOURIER_WIDTH), row),
            pl.BlockSpec((None, tm, ATTN_WIDTH), row),
            pl.BlockSpec((d, d), const2),
            pl.BlockSpec((None, tm, d), row),
            pl.BlockSpec((1, d), const2),
            pl.BlockSpec((None, 1, d), per_batch),
            pl.BlockSpec((1, d), const2),
            pl.BlockSpec((None, 1, d), per_batch),
            pl.BlockSpec((None, 1, d), per_batch),
            pl.BlockSpec((d, ROUTER_PAD), const2),
            pl.BlockSpec((d, ROUTER_PAD), const2),
            pl.BlockSpec((1, ROUTER_PAD), const2),
        ],
        out_specs=[
            pl.BlockSpec((None, tm, d), row),
            pl.BlockSpec((tm * PACK_SLABS, LANES), lambda bi, i: (bi * (s // tm) + i, 0)),
            pl.BlockSpec((None, tm, ROUTER_PAD), row),
        ],
        out_shape=[
            jax.ShapeDtypeStruct((b, s, d), F32),
            jax.ShapeDtypeStruct((b * s * PACK_SLABS, LANES), U32),
            jax.ShapeDtypeStruct((b, s, ROUTER_PAD), F32),
        ],
        compiler_params=pltpu.CompilerParams(
            dimension_semantics=("arbitrary", "arbitrary"), vmem_limit_bytes=VMEM_LIMIT_BYTES),
        name="outproj",
    )(fo, ao, w_out_b, x, gpost, gate, gpre, shift, scale, wr_hi, wr_lo, br)


def _tile_rows(row, n_rows=1):
    return pl.ds(pl.multiple_of(row * PACK_SLABS, PACK_SLABS), n_rows * PACK_SLABS)


def _dispatch_kernel(dest_ref, fill_lo_ref, fill_hi_ref, used_tiles_ref, h_hbm, xs_hbm, zeros, sem, fill_sem):
    step = pl.program_id(0)
    n = DISPATCH_TT * TOP_K
    base = step * n

    def row_copy(i):
        tok = step * DISPATCH_TT + i // TOP_K
        return pltpu.make_async_copy(
            h_hbm.at[_tile_rows(tok)], xs_hbm.at[_tile_rows(dest_ref[base + i])], sem)

    def start(i, c):
        row_copy(i).start()
        return c

    def wait(i, c):
        row_copy(i).wait()
        return c

    lax.fori_loop(0, n, start, 0, unroll=8)

    @pl.when(step == 0)
    def _fill():
        zeros[...] = jnp.zeros(zeros.shape, U32)
        n_tiles = xs_hbm.shape[0] // (MOE_TM * PACK_SLABS)

        def pad_chunks(e, fn):
            lo = fill_lo_ref[e]
            length = fill_hi_ref[e] - lo
            for bit in (128, 64, 32, 16, 8, 4, 2, 1):
                @pl.when((length & bit) != 0)
                def _():
                    row = lo + (length & ~(2 * bit - 1))
                    fn(pltpu.make_async_copy(
                        zeros.at[pl.ds(0, bit * PACK_SLABS)], xs_hbm.at[_tile_rows(row, bit)], fill_sem))

        def tail_copy(t):
            return pltpu.make_async_copy(zeros, xs_hbm.at[_tile_rows(t * MOE_TM, MOE_TM)], fill_sem)

        def start_e(e, c):
            pad_chunks(e, lambda cp: cp.start())
            return c

        def wait_e(e, c):
            pad_chunks(e, lambda cp: cp.wait())
            return c

        def start_t(t, c):
            tail_copy(t).start()
            return c

        def wait_t(t, c):
            tail_copy(t).wait()
            return c

        lax.fori_loop(0, N_EXPERTS, start_e, 0)
        lax.fori_loop(used_tiles_ref[0], n_tiles, start_t, 0)
        lax.fori_loop(0, N_EXPERTS, wait_e, 0)
        lax.fori_loop(used_tiles_ref[0], n_tiles, wait_t, 0)

    lax.fori_loop(0, n, wait, 0, unroll=8)


def _dispatch(dest, fill_lo, fill_hi, used_tiles, h2p, n_rows):
    n_tok = h2p.shape[0] // PACK_SLABS
    return pl.pallas_call(
        _dispatch_kernel,
        grid_spec=pltpu.PrefetchScalarGridSpec(
            num_scalar_prefetch=4,
            grid=(n_tok // DISPATCH_TT,),
            in_specs=[pl.BlockSpec(memory_space=pl.ANY)],
            out_specs=pl.BlockSpec(memory_space=pl.ANY),
            scratch_shapes=[
                pltpu.VMEM((MOE_TM * PACK_SLABS, LANES), U32),
                pltpu.SemaphoreType.DMA,
                pltpu.SemaphoreType.DMA,
            ],
        ),
        out_shape=jax.ShapeDtypeStruct((n_rows * PACK_SLABS, LANES), U32),
        compiler_params=pltpu.CompilerParams(
            dimension_semantics=("arbitrary",), vmem_limit_bytes=VMEM_LIMIT_BYTES),
        name="dispatch",
    )(dest, fill_lo, fill_hi, used_tiles, h2p)


def _experts_kernel(item_e_ref, item_row0_ref, item_nt_ref, item_nz_ref,
                    xs_hbm, wg_ref, bg_ref, wu_ref, bu_ref, wd_ref, bd_ref, out_hbm,
                    stage, x_b, acc, wg_b, wu_b, wd_b, sems):
    w = pl.program_id(0)
    j = pl.program_id(1)
    nt = item_nt_ref[w]
    nz = item_nz_ref[w]
    row0 = item_row0_ref[w]
    last_j = j == pl.num_programs(1) - 1

    def x_copy(t):
        return pltpu.make_async_copy(
            xs_hbm.at[_tile_rows(row0 + t * MOE_TM, MOE_TM)], stage.at[t], sems.at[t])

    def out_copy(t):
        return pltpu.make_async_copy(
            stage.at[t], out_hbm.at[_tile_rows(row0 + t * MOE_TM, MOE_TM)], sems.at[t])

    @pl.when(jnp.logical_and(j == 0, nt > 0))
    def _load_rows():
        for t in range(MOE_R):
            @pl.when(t < nt)
            def _():
                x_copy(t).start()
        for t in range(MOE_R):
            @pl.when(t < nt)
            def _():
                x_copy(t).wait()
                for a in range(PACK_SLABS):
                    hi, lo = _unpack_slab(stage.at[t], a, MOE_TM)
                    x_b[t, :, a * LANES:(a + 1) * LANES] = hi.astype(BF16)
                    x_b[t, :, PACK_HALF + a * LANES:PACK_HALF + (a + 1) * LANES] = lo.astype(BF16)
                acc[t] = jnp.broadcast_to(bd_ref[...], acc.shape[1:])

    @pl.when(nt > 0)
    def _compute():
        wg_b[...] = wg_ref[...].astype(BF16)
        wu_b[...] = wu_ref[...].astype(BF16)
        wd_b[...] = wd_ref[...].astype(BF16)
        bg = bg_ref[...]
        bu = bu_ref[...]

        def tile(t, carry):
            x = x_b[t]
            g = jnp.dot(x, wg_b[...], preferred_element_type=F32) + bg
            u = jnp.dot(x, wu_b[...], preferred_element_type=F32) + bu
            g = jnp.minimum(g, SWIGLU_LIMIT)
            u = jnp.clip(u, -SWIGLU_LIMIT, SWIGLU_LIMIT)
            hidden = ((u + 1.0) * (g * jax.nn.sigmoid(SWIGLU_ALPHA * g))).astype(BF16)
            acc[t] += jnp.dot(hidden, wd_b[...], preferred_element_type=F32)
            return carry

        lax.fori_loop(0, nt, tile, 0)

    @pl.when(jnp.logical_and(last_j, nt > 0))
    def _pack_out():
        for t in range(MOE_R):
            @pl.when(t < nt)
            def _():
                _pack_rows(acc[t], stage.at[t])

    @pl.when(jnp.logical_and(j == 0, nz > 0))
    def _zero_tail():
        for t in range(MOE_R):
            @pl.when(t < nz)
            def _():
                stage[t] = jnp.zeros(stage.shape[1:], U32)

    n_out = jnp.where(last_j, nt, 0) + jnp.where(j == 0, nz, 0)

    @pl.when(n_out > 0)
    def _store_rows():
        for t in range(MOE_R):
            @pl.when(t < n_out)
            def _():
                out_copy(t).start()
        for t in range(MOE_R):
            @pl.when(t < n_out)
            def _():
                out_copy(t).wait()


def _experts(item_e, item_row0, item_nt, item_nz, xs, w_gate, b_gate, w_up, b_up, w_down, b_down, n_items):
    d = D_MODEL
    n_ff = D_FF // MOE_TF
    last = n_ff - 1

    def ff_idx(w, j, nt):
        return jnp.where(nt[w] > 0, j, last)

    col_w = lambda w, j, ie, r0, nt, nz:(ie[w], 0, ff_idx(w, j, nt))
    row_w = lambda w, j, ie, r0, nt, nz:(ie[w], ff_idx(w, j, nt), 0)
    full_b = lambda w, j, ie, r0, nt, nz:(ie[w], 0, 0)
    return pl.pallas_call(
        _experts_kernel,
        grid_spec=pltpu.PrefetchScalarGridSpec(
            num_scalar_prefetch=4,
            grid=(n_items, n_ff),
            in_specs=[
                pl.BlockSpec(memory_space=pl.ANY),
                pl.BlockSpec((None, d, MOE_TF), col_w),
                pl.BlockSpec((None, 1, MOE_TF), col_w),
                pl.BlockSpec((None, d, MOE_TF), col_w),
                pl.BlockSpec((None, 1, MOE_TF), col_w),
                pl.BlockSpec((None, MOE_TF, d), row_w),
                pl.BlockSpec((None, 1, d), full_b),
            ],
            out_specs=pl.BlockSpec(memory_space=pl.ANY),
            scratch_shapes=[
                pltpu.VMEM((MOE_R, MOE_TM * PACK_SLABS, LANES), U32),
                pltpu.VMEM((MOE_R, MOE_TM, d), BF16),
                pltpu.VMEM((MOE_R, MOE_TM, d), F32),
                pltpu.VMEM((d, MOE_TF), BF16),
                pltpu.VMEM((d, MOE_TF), BF16),
                pltpu.VMEM((MOE_TF, d), BF16),
                pltpu.SemaphoreType.DMA((MOE_R,)),
            ],
        ),
        out_shape=jax.ShapeDtypeStruct(xs.shape, U32),
        compiler_params=pltpu.CompilerParams(
            dimension_semantics=("arbitrary", "arbitrary"), vmem_limit_bytes=VMEM_LIMIT_BYTES),
        name="experts",
    )(item_e, item_row0, item_nt, item_nz, xs, w_gate, b_gate, w_up, b_up, w_down, b_down)


def _combine_kernel(dest_ref, rows_hbm, gates_ref, x1_ref, gpost_ref, gate_ref, o_ref, buf, sem):
    tt = COMBINE_TT
    base = (pl.program_id(0) * pl.num_programs(1) + pl.program_id(1)) * tt * TOP_K

    def row_copy(i):
        return pltpu.make_async_copy(
            rows_hbm.at[_tile_rows(dest_ref[base + i])],
            buf.at[i % TOP_K, _tile_rows(i // TOP_K)], sem)

    def start(i, c):
        row_copy(i).start()
        return c

    def wait(i, c):
        row_copy(i).wait()
        return c

    lax.fori_loop(0, tt * TOP_K, start, 0, unroll=8)
    lax.fori_loop(0, tt * TOP_K, wait, 0, unroll=8)
    gates = gates_ref[...]
    y_hi, y_lo = [], []
    for a in range(PACK_SLABS):
        acc_hi = acc_lo = None
        for kk in range(TOP_K):
            hi, lo = _unpack_slab(buf.at[kk], a, tt)
            gk = gates[:, kk:kk + 1]
            acc_hi = gk * hi if acc_hi is None else acc_hi + gk * hi
            acc_lo = gk * lo if acc_lo is None else acc_lo + gk * lo
        y_hi.append(acc_hi)
        y_lo.append(acc_lo)
    y = jnp.concatenate(y_hi + y_lo, axis=1)
    o_ref[...] = x1_ref[...] + gate_ref[...] * _rms(y, gpost_ref[...])


def _combine(dest, rows, gates, x1, gpost, gate):
    b, s, d = x1.shape
    tt = COMBINE_TT
    row = lambda bi, i, dst: (bi, i, 0)
    return pl.pallas_call(
        _combine_kernel,
        grid_spec=pltpu.PrefetchScalarGridSpec(
            num_scalar_prefetch=1,
            grid=(b, s // tt),
            in_specs=[
                pl.BlockSpec(memory_space=pl.ANY),
                pl.BlockSpec((None, tt, TOP_K), row),
                pl.BlockSpec((None, tt, d), row),
                pl.BlockSpec((1, d), lambda bi, i, dst: (0, 0)),
                pl.BlockSpec((None, 1, d), lambda bi, i, dst: (bi, 0, 0)),
            ],
            out_specs=pl.BlockSpec((None, tt, d), row),
            scratch_shapes=[pltpu.VMEM((TOP_K, tt * PACK_SLABS, LANES), U32), pltpu.SemaphoreType.DMA],
        ),
        out_shape=jax.ShapeDtypeStruct((b, s, d), F32),
        compiler_params=pltpu.CompilerParams(
            dimension_semantics=("arbitrary", "arbitrary"), vmem_limit_bytes=VMEM_LIMIT_BYTES),
        name="combine",
    )(dest, rows, gates, x1, gpost, gate)


def _rope_tables(seq_len):
    rows = seq_len // GRID_W
    row_idx = jnp.repeat(jnp.arange(rows, dtype=F32), GRID_W)
    col_idx = jnp.tile(jnp.arange(GRID_W, dtype=F32), rows)
    half = HEAD_DIM // 2
    inv_freq = 1.0 / (ROPE_THETA ** (jnp.arange(0, half, 2, dtype=F32) / half))
    ang = jnp.concatenate([row_idx[:, None] * inv_freq, col_idx[:, None] * inv_freq], axis=-1)
    cos = jnp.repeat(jnp.cos(ang), 2, axis=-1)
    sin = jnp.sin(ang)
    sin_signed = jnp.stack([-sin, sin], axis=-1).reshape(seq_len, HEAD_DIM)
    return cos, sin_signed


def _dft_tables(n):
    idx = np.arange(n, dtype=np.int64)
    ang = 2.0 * np.pi * ((idx[:, None] * idx[None, :]) % n).astype(np.float64) / n
    scale = 1.0 / np.sqrt(n)
    return np.cos(ang) * scale, np.sin(ang) * scale


def _route(logits, n_tok):
    top_val, top_idx = lax.top_k(logits, TOP_K)
    gates = jax.nn.softmax(top_val, axis=-1)
    flat_e = top_idx.reshape(-1).astype(jnp.int32)
    onehot = (flat_e[:, None] == jnp.arange(N_EXPERTS, dtype=jnp.int32)[None, :]).astype(jnp.int32)
    csum = jnp.cumsum(onehot, axis=0)
    rank = jnp.sum(csum * onehot, axis=1) - 1
    counts = csum[-1]
    tiles = (counts + MOE_TM - 1) // MOE_TM
    padded = tiles * MOE_TM
    pad_end = jnp.cumsum(padded)
    pad_start = pad_end - padded
    dest = pad_start[flat_e] + rank
    n_rows = n_tok * TOP_K + N_EXPERTS * MOE_TM
    fill_lo = (pad_start + counts).astype(jnp.int32)
    fill_hi = pad_end.astype(jnp.int32)
    used_tiles = jnp.sum(tiles).astype(jnp.int32).reshape(1)
    n_items = N_EXPERTS + n_rows // MOE_RS
    items_per_e = (tiles + MOE_R - 1) // MOE_R
    item_end = jnp.cumsum(items_per_e)
    item_start = item_end - items_per_e
    w = jnp.arange(n_items, dtype=jnp.int32)
    total = item_end[-1]
    e_of_w = jnp.minimum(jnp.searchsorted(item_end, w, side='right'), N_EXPERTS - 1).astype(jnp.int32)
    local = w - item_start[e_of_w]
    active = w < total
    last_e = e_of_w[jnp.maximum(total - 1, 0)]
    item_e = jnp.where(active, e_of_w, last_e).astype(jnp.int32)
    item_nt = jnp.where(active, jnp.minimum(MOE_R, tiles[e_of_w] - local * MOE_R), 0).astype(jnp.int32)
    n_tiles = n_rows // MOE_TM
    fill_tile0 = jnp.sum(tiles) + (w - total) * MOE_R
    item_nz = jnp.where(active, 0, jnp.clip(n_tiles - fill_tile0, 0, MOE_R)).astype(jnp.int32)
    fill_row0 = jnp.minimum(fill_tile0, n_tiles - 1) * MOE_TM
    item_row0 = jnp.where(active, pad_start[e_of_w] + local * MOE_RS, fill_row0).astype(jnp.int32)
    fill = (fill_lo, fill_hi, used_tiles)
    items = (item_e, item_row0, item_nt, item_nz)
    return gates, dest.astype(jnp.int32), fill, items, n_rows, n_items


def kernel(x, c, w_ada, b_ada, g_pre_mix, w_in, w_fourier, q_norm_g, k_norm_g, g_fourier_out, g_attn_out,
           w_out, g_post_mix, g_pre_ffn, w_router, b_router, w_gate, b_gate, w_up, b_up, w_down, b_down,
           g_post_ffn):
    b, s, d = x.shape
    n_tok = b * s
    depth = w_ada.shape[0]
    cos, sin_signed = _rope_tables(s)
    cs_np, ss_np = _dft_tables(s)
    cs = jnp.asarray(cs_np, dtype=BF16)
    ss = jnp.asarray(ss_np, dtype=BF16)
    cc_np, sc_np = _dft_tables(FOURIER_GROUP_DIM)
    cc = jnp.asarray(np.concatenate([cc_np, -sc_np], axis=0), dtype=BF16)
    c_pad = jnp.pad(c, ((0, 8 - b), (0, 0)))
    row2 = lambda a: a.reshape(1, -1)

    for l in range(depth):
        mod = _ada(c_pad, w_ada[l], row2(b_ada[l]))[:b].reshape(b, N_MOD, 1, d)
        shift_m, scale_m, gate_m = mod[:, 0], mod[:, 1], mod[:, 2]
        shift_f, scale_f, gate_f = mod[:, 3], mod[:, 4], mod[:, 5]

        f, q, k, v = _inproj(
            x, row2(g_pre_mix[l]), shift_m, scale_m, w_in[l].astype(BF16),
            row2(q_norm_g[l] * (HEAD_DIM ** -0.5)), row2(k_norm_g[l]), cos, sin_signed)
        fo = _fourier(f, cs, ss, cc, w_fourier[l], row2(g_fourier_out[l]))
        ao = _attn(q, k, v, row2(g_attn_out[l]))

        wr = jnp.pad(w_router[l], ((0, 0), (0, ROUTER_PAD - N_EXPERTS)))
        wr_hi = wr.astype(BF16)
        wr_lo = (wr - wr_hi.astype(F32)).astype(BF16)
        br = jnp.pad(b_router[l], (0, ROUTER_PAD - N_EXPERTS)).reshape(1, ROUTER_PAD)
        x1, h2p, logits = _outproj(
            fo, ao, w_out[l].astype(BF16), x, row2(g_post_mix[l]), gate_m, row2(g_pre_ffn[l]),
            shift_f, scale_f, wr_hi, wr_lo, br)

        logits = logits.reshape(n_tok, ROUTER_PAD)[:, :N_EXPERTS]
        gates, dest, fill, items, n_rows, n_items = _route(logits, n_tok)
        xs = _dispatch(dest, *fill, h2p, n_rows)
        rows = _experts(
            *items, xs,
            w_gate[l], b_gate[l].reshape(N_EXPERTS, 1, D_FF), w_up[l], b_up[l].reshape(N_EXPERTS, 1, D_FF),
            w_down[l], b_down[l].reshape(N_EXPERTS, 1, d), n_items)
        x = _combine(dest, rows, gates.reshape(b, s, TOP_K), x1, row2(g_post_ffn[l]), gate_f)
    return x
```

```python
import functools

import numpy as np
import jax
import jax.numpy as jnp
from jax import lax
from jax.experimental import pallas as pl
from jax.experimental.pallas import tpu as pltpu

F32 = jnp.float32
BF16 = jnp.bfloat16
U32 = jnp.uint32

D_MODEL = 2048
GRID_W = 64
FOURIER_WIDTH = 512
FOURIER_GROUP_DIM = 128
FOURIER_GROUPS = 4
ATTN_WIDTH = 1536
HEAD_DIM = 128
N_Q_HEADS = 12
GQA_GROUP = 3
N_KV_HEADS = 4
KV_WIDTH = 512
IN_WIDTH = 3072
ROPE_THETA = 10000.0
N_EXPERTS = 32
TOP_K = 4
D_FF = 2048
SWIGLU_LIMIT = 7.0
SWIGLU_ALPHA = 1.702
N_MOD = 6
EPS = 1e-6

VMEM_LIMIT_BYTES = 56 * 1024 * 1024
LANES = 128

ADA_TN = 1024
INPROJ_TM = 256
FOURIER_TR = 256
ATTN_TQ = 128
OUTPROJ_TM = 256
ROUTER_PAD = LANES
MOE_TM = 256
MOE_R = 5
MOE_RS = MOE_TM * MOE_R
MOE_TF = 256
DISPATCH_TT = 512
COMBINE_TT = 128
PACK_HALF = D_MODEL // 2
PACK_SLABS = PACK_HALF // LANES


def _rms(x, g):
    return x * lax.rsqrt(jnp.mean(x * x, axis=-1, keepdims=True) + EPS) * g


def _ada_kernel(c_ref, w_ref, b_ref, o_ref):
    c = c_ref[...]
    s = (c * jax.nn.sigmoid(c)).astype(BF16)
    o_ref[...] = jnp.dot(s, w_ref[...].astype(BF16), preferred_element_type=F32) + b_ref[...]


def _ada(c_pad, w_ada, b_ada):
    m, d = c_pad.shape
    n = w_ada.shape[1]
    return pl.pallas_call(
        _ada_kernel,
        grid=(n // ADA_TN,),
        in_specs=[
            pl.BlockSpec((m, d), lambda j: (0, 0)),
            pl.BlockSpec((d, ADA_TN), lambda j: (0, j)),
            pl.BlockSpec((1, ADA_TN), lambda j: (0, j)),
        ],
        out_specs=pl.BlockSpec((m, ADA_TN), lambda j: (0, j)),
        out_shape=jax.ShapeDtypeStruct((m, n), F32),
        compiler_params=pltpu.CompilerParams(
            dimension_semantics=("arbitrary",), vmem_limit_bytes=VMEM_LIMIT_BYTES),
        name="ada",
    )(c_pad, w_ada, b_ada)


def _rope(p, cos, sin_signed, even_lane):
    partner = jnp.where(even_lane, pltpu.roll(p, LANES - 1, 1), pltpu.roll(p, 1, 1))
    return p * cos + partner * sin_signed


def _inproj_kernel(x_ref, g_ref, shift_ref, scale_ref, w_ref, qg_ref, kg_ref, cos_ref, sin_ref,
                   f_ref, q_ref, k_ref, v_ref):
    x = x_ref[...]
    h = _rms(x, g_ref[...]) * (1.0 + scale_ref[...]) + shift_ref[...]
    hb = h.astype(BF16)
    cos = cos_ref[...]
    sin = sin_ref[...]
    even_lane = (lax.broadcasted_iota(jnp.int32, cos.shape, 1) % 2) == 0
    chunk = 512
    heads_per_chunk = chunk // HEAD_DIM
    for ci in range(IN_WIDTH // chunk):
        p = jnp.dot(hb, w_ref[:, ci * chunk:(ci + 1) * chunk], preferred_element_type=F32)
        if ci == 0:
            f_ref[...] = p.astype(BF16)
        elif ci == 5:
            v_ref[...] = p.astype(BF16)
        else:
            gain = kg_ref[...] if ci == 4 else qg_ref[...]
            for hh in range(heads_per_chunk):
                ph = p[:, hh * HEAD_DIM:(hh + 1) * HEAD_DIM]
                ph = _rope(_rms(ph, gain), cos, sin, even_lane).astype(BF16)
                if ci == 4:
                    k_ref[:, hh * HEAD_DIM:(hh + 1) * HEAD_DIM] = ph
                else:
                    c0 = (ci - 1) * chunk + hh * HEAD_DIM
                    q_ref[:, c0:c0 + HEAD_DIM] = ph


def _inproj(x, g, shift, scale, w_in_b, qg, kg, cos, sin):
    b, s, d = x.shape
    tm = INPROJ_TM
    row = lambda bi, i: (bi, i, 0)
    per_batch = lambda bi, i: (bi, 0, 0)
    const2 = lambda bi, i: (0, 0)
    return pl.pallas_call(
        _inproj_kernel,
        grid=(b, s // tm),
        in_specs=[
            pl.BlockSpec((None, tm, d), row),
            pl.BlockSpec((1, d), const2),
            pl.BlockSpec((None, 1, d), per_batch),
            pl.BlockSpec((None, 1, d), per_batch),
            pl.BlockSpec((d, IN_WIDTH), const2),
            pl.BlockSpec((1, HEAD_DIM), const2),
            pl.BlockSpec((1, HEAD_DIM), const2),
            pl.BlockSpec((tm, HEAD_DIM), lambda bi, i: (i, 0)),
            pl.BlockSpec((tm, HEAD_DIM), lambda bi, i: (i, 0)),
        ],
        out_specs=[
            pl.BlockSpec((None, tm, FOURIER_WIDTH), row),
            pl.BlockSpec((None, tm, ATTN_WIDTH), row),
            pl.BlockSpec((None, tm, KV_WIDTH), row),
            pl.BlockSpec((None, tm, KV_WIDTH), row),
        ],
        out_shape=[
            jax.ShapeDtypeStruct((b, s, FOURIER_WIDTH), BF16),
            jax.ShapeDtypeStruct((b, s, ATTN_WIDTH), BF16),
            jax.ShapeDtypeStruct((b, s, KV_WIDTH), BF16),
            jax.ShapeDtypeStruct((b, s, KV_WIDTH), BF16),
        ],
        compiler_params=pltpu.CompilerParams(
            dimension_semantics=("arbitrary", "arbitrary"), vmem_limit_bytes=VMEM_LIMIT_BYTES),
        name="inproj",
    )(x, g, shift, scale, w_in_b, qg, kg, cos, sin)


def _fourier_kernel(cs_ref, ss_ref, f_ref, cc_ref, w_ref, g_ref, o_ref):
    f = f_ref[...]
    a = jnp.dot(cs_ref[...], f, preferred_element_type=F32)
    bm = jnp.dot(ss_ref[...], f, preferred_element_type=F32)
    cc = cc_ref[...]
    outs = []
    for gi in range(FOURIER_GROUPS):
        sl = slice(gi * FOURIER_GROUP_DIM, (gi + 1) * FOURIER_GROUP_DIM)
        ab = jnp.concatenate([a[:, sl], bm[:, sl]], axis=1).astype(BF16)
        fm = jnp.dot(ab, cc, preferred_element_type=F32)
        outs.append(jnp.dot(fm.astype(BF16), w_ref[gi].astype(BF16), preferred_element_type=F32))
    fo = jnp.concatenate(outs, axis=1)
    o_ref[...] = _rms(fo, g_ref[...]).astype(BF16)


def _fourier(f, cs, ss, cc, w_fourier, g):
    b, s, fw = f.shape
    tr = FOURIER_TR
    return pl.pallas_call(
        _fourier_kernel,
        grid=(b, s // tr),
        in_specs=[
            pl.BlockSpec((tr, s), lambda bi, i: (i, 0)),
            pl.BlockSpec((tr, s), lambda bi, i: (i, 0)),
            pl.BlockSpec((None, s, fw), lambda bi, i: (bi, 0, 0)),
            pl.BlockSpec((2 * FOURIER_GROUP_DIM, FOURIER_GROUP_DIM), lambda bi, i: (0, 0)),
            pl.BlockSpec((FOURIER_GROUPS, FOURIER_GROUP_DIM, FOURIER_GROUP_DIM), lambda bi, i: (0, 0, 0)),
            pl.BlockSpec((1, fw), lambda bi, i: (0, 0)),
        ],
        out_specs=pl.BlockSpec((None, tr, fw), lambda bi, i: (bi, i, 0)),
        out_shape=jax.ShapeDtypeStruct((b, s, fw), BF16),
        compiler_params=pltpu.CompilerParams(
            dimension_semantics=("arbitrary", "arbitrary"), vmem_limit_bytes=VMEM_LIMIT_BYTES),
        name="fourier",
    )(cs, ss, f, cc, w_fourier, g)


def _attn_kernel(q_ref, k_ref, v_ref, g_ref, o_ref, ao_ref):
    tq = q_ref.shape[0]
    for kv in range(N_KV_HEADS):
        kh = k_ref[:, kv * HEAD_DIM:(kv + 1) * HEAD_DIM]
        vh = v_ref[:, kv * HEAD_DIM:(kv + 1) * HEAD_DIM]
        c0 = kv * GQA_GROUP * HEAD_DIM
        qh = jnp.concatenate(
            [q_ref[:, c0 + gi * HEAD_DIM:c0 + (gi + 1) * HEAD_DIM] for gi in range(GQA_GROUP)], axis=0)
        sc = lax.dot_general(qh, kh, (((1,), (1,)), ((), ())), preferred_element_type=F32)
        m = jnp.max(sc, axis=-1, keepdims=True)
        p = jnp.exp(sc - m)
        l = jnp.sum(p, axis=-1, keepdims=True)
        o = jnp.dot(p.astype(BF16), vh, preferred_element_type=F32) / l
        for gi in range(GQA_GROUP):
            ao_ref[:, c0 + gi * HEAD_DIM:c0 + (gi + 1) * HEAD_DIM] = o[gi * tq:(gi + 1) * tq]
    o_ref[...] = _rms(ao_ref[...], g_ref[...]).astype(BF16)


def _attn(q, k, v, g):
    b, s, _ = q.shape
    tq = ATTN_TQ
    return pl.pallas_call(
        _attn_kernel,
        grid=(b, s // tq),
        in_specs=[
            pl.BlockSpec((None, tq, ATTN_WIDTH), lambda bi, i: (bi, i, 0)),
            pl.BlockSpec((None, s, KV_WIDTH), lambda bi, i: (bi, 0, 0)),
            pl.BlockSpec((None, s, KV_WIDTH), lambda bi, i: (bi, 0, 0)),
            pl.BlockSpec((1, ATTN_WIDTH), lambda bi, i: (0, 0)),
        ],
        out_specs=pl.BlockSpec((None, tq, ATTN_WIDTH), lambda bi, i: (bi, i, 0)),
        out_shape=jax.ShapeDtypeStruct((b, s, ATTN_WIDTH), BF16),
        scratch_shapes=[pltpu.VMEM((tq, ATTN_WIDTH), F32)],
        compiler_params=pltpu.CompilerParams(
            dimension_semantics=("arbitrary", "arbitrary"), vmem_limit_bytes=VMEM_LIMIT_BYTES),
        name="attn",
    )(q, k, v, g)


def _split_bf16(a):
    hi = a.astype(BF16)
    lo = (a - hi.astype(F32)).astype(BF16)
    return hi, lo


def _pack_rows(val, dst_ref):
    n = val.shape[0]
    bits = lax.bitcast_convert_type(val.astype(BF16).astype(F32), U32)
    packed = bits[:, :PACK_HALF] | (bits[:, PACK_HALF:] >> 16)
    for a in range(PACK_SLABS):
        dst_ref[pl.ds(a, n, stride=PACK_SLABS), :] = packed[:, a * LANES:(a + 1) * LANES]


def _unpack_slab(src_ref, a, n):
    word = src_ref[pl.ds(a, n, stride=PACK_SLABS), :]
    hi = lax.bitcast_convert_type(word & jnp.uint32(0xFFFF0000), F32)
    lo = lax.bitcast_convert_type(word << 16, F32)
    return hi, lo


def _outproj_kernel(fo_ref, ao_ref, w_ref, x_ref, gpost_ref, gate_ref, gpre_ref, shift_ref, scale_ref,
                    wr_hi_ref, wr_lo_ref, br_ref, x1_ref, h2p_ref, logit_ref):
    mix = jnp.dot(fo_ref[...], w_ref[:FOURIER_WIDTH, :], preferred_element_type=F32)
    mix = mix + jnp.dot(ao_ref[...], w_ref[FOURIER_WIDTH:, :], preferred_element_type=F32)
    x1 = x_ref[...] + gate_ref[...] * _rms(mix, gpost_ref[...])
    x1_ref[...] = x1
    h2 = _rms(x1, gpre_ref[...]) * (1.0 + scale_ref[...]) + shift_ref[...]
    _pack_rows(h2, h2p_ref)
    h_hi, h_lo = _split_bf16(h2)
    lg = jnp.dot(h_hi, wr_hi_ref[...], preferred_element_type=F32)
    lg = lg + jnp.dot(h_hi, wr_lo_ref[...], preferred_element_type=F32)
    lg = lg + jnp.dot(h_lo, wr_hi_ref[...], preferred_element_type=F32)
    logit_ref[...] = lg + br_ref[...]


def _outproj(fo, ao, w_out_b, x, gpost, gate, gpre, shift, scale, wr_hi, wr_lo, br):
    b, s, d = x.shape
    tm = OUTPROJ_TM
    row = lambda bi, i: (bi, i, 0)
    per_batch = lambda bi, i: (bi, 0, 0)
    const2 = lambda bi, i: (0, 0)
    return pl.pallas_call(
        _outproj_kernel,
        grid=(b, s // tm),
        in_specs=[
            pl.BlockSpec((None, tm, FOURIER_WIDTH), row),
            pl.BlockSpec((None, tm, ATTN_WIDTH), row),
            pl.BlockSpec((d, d), const2),
            pl.BlockSpec((None, tm, d), row),
            pl.BlockSpec((1, d), const2),
            pl.BlockSpec((None, 1, d), per_batch),
            pl.BlockSpec((1, d), const2),
            pl.BlockSpec((None, 1, d), per_batch),
            pl.BlockSpec((None, 1, d), per_batch),
            pl.BlockSpec((d, ROUTER_PAD), const2),
            pl.BlockSpec((d, ROUTER_PAD), const2),
            pl.BlockSpec((1, ROUTER_PAD), const2),
        ],
        out_specs=[
            pl.BlockSpec((None, tm, d), row),
            pl.BlockSpec((tm * PACK_SLABS, LANES), lambda bi, i: (bi * (s // tm) + i, 0)),
            pl.BlockSpec((None, tm, ROUTER_PAD), row),
        ],
        out_shape=[
            jax.ShapeDtypeStruct((b, s, d), F32),
            jax.ShapeDtypeStruct((b * s * PACK_SLABS, LANES), U32),
            jax.ShapeDtypeStruct((b, s, ROUTER_PAD), F32),
        ],
        compiler_params=pltpu.CompilerParams(
            dimension_semantics=("arbitrary", "arbitrary"), vmem_limit_bytes=VMEM_LIMIT_BYTES),
        name="outproj",
    )(fo, ao, w_out_b, x, gpost, gate, gpre, shift, scale, wr_hi, wr_lo, br)


def _tile_rows(row, n_rows=1):
    return pl.ds(pl.multiple_of(row * PACK_SLABS, PACK_SLABS), n_rows * PACK_SLABS)


def _dispatch_kernel(dest_ref, fill_lo_ref, fill_hi_ref, used_tiles_ref, h_ref, xs_hbm, zeros, sem, fill_sem):
    step = pl.program_id(0)
    base = step * (DISPATCH_TT * TOP_K)

    def row_copy(i, kk):
        return pltpu.make_async_copy(
            h_ref.at[_tile_rows(i)], xs_hbm.at[_tile_rows(dest_ref[base + i * TOP_K + kk])], sem)

    def start(i, c):
        for kk in range(TOP_K):
            row_copy(i, kk).start()
        return c

    def wait(i, c):
        for kk in range(TOP_K):
            row_copy(i, kk).wait()
        return c

    lax.fori_loop(0, DISPATCH_TT, start, 0, unroll=4)

    @pl.when(step == 0)
    def _fill():
        zeros[...] = jnp.zeros(zeros.shape, U32)
        n_tiles = xs_hbm.shape[0] // (MOE_TM * PACK_SLABS)

        def pad_chunks(e, fn):
            lo = fill_lo_ref[e]
            length = fill_hi_ref[e] - lo
            for bit in (128, 64, 32, 16, 8, 4, 2, 1):
                @pl.when((length & bit) != 0)
                def _():
                    row = lo + (length & ~(2 * bit - 1))
                    fn(pltpu.make_async_copy(
                        zeros.at[pl.ds(0, bit * PACK_SLABS)], xs_hbm.at[_tile_rows(row, bit)], fill_sem))

        def tail_copy(t):
            return pltpu.make_async_copy(zeros, xs_hbm.at[_tile_rows(t * MOE_TM, MOE_TM)], fill_sem)

        def start_e(e, c):
            pad_chunks(e, lambda cp: cp.start())
            return c

        def wait_e(e, c):
            pad_chunks(e, lambda cp: cp.wait())
            return c

        def start_t(t, c):
            tail_copy(t).start()
            return c

        def wait_t(t, c):
            tail_copy(t).wait()
            return c

        lax.fori_loop(0, N_EXPERTS, start_e, 0)
        lax.fori_loop(used_tiles_ref[0], n_tiles, start_t, 0)
        lax.fori_loop(0, N_EXPERTS, wait_e, 0)
        lax.fori_loop(used_tiles_ref[0], n_tiles, wait_t, 0)

    lax.fori_loop(0, DISPATCH_TT, wait, 0, unroll=4)


def _dispatch(dest, fill_lo, fill_hi, used_tiles, h2p, n_rows):
    n_tok = h2p.shape[0] // PACK_SLABS
    return pl.pallas_call(
        _dispatch_kernel,
        grid_spec=pltpu.PrefetchScalarGridSpec(
            num_scalar_prefetch=4,
            grid=(n_tok // DISPATCH_TT,),
            in_specs=[pl.BlockSpec((DISPATCH_TT * PACK_SLABS, LANES), lambda i, *_: (i, 0))],
            out_specs=pl.BlockSpec(memory_space=pl.ANY),
            scratch_shapes=[
                pltpu.VMEM((MOE_TM * PACK_SLABS, LANES), U32),
                pltpu.SemaphoreType.DMA,
                pltpu.SemaphoreType.DMA,
            ],
        ),
        out_shape=jax.ShapeDtypeStruct((n_rows * PACK_SLABS, LANES), U32),
        compiler_params=pltpu.CompilerParams(
            dimension_semantics=("arbitrary",), vmem_limit_bytes=VMEM_LIMIT_BYTES),
        name="dispatch",
    )(dest, fill_lo, fill_hi, used_tiles, h2p)


def _experts_kernel(item_e_ref, item_row0_ref, item_nt_ref, item_nz_ref,
                    xs_hbm, wg_ref, bg_ref, wu_ref, bu_ref, wd_ref, bd_ref, out_hbm,
                    stage, x_b, acc, wg_b, wu_b, wd_b, sems):
    w = pl.program_id(0)
    j = pl.program_id(1)
    nt = item_nt_ref[w]
    nz = item_nz_ref[w]
    row0 = item_row0_ref[w]
    last_j = j == pl.num_programs(1) - 1

    def x_copy(t):
        return pltpu.make_async_copy(
            xs_hbm.at[_tile_rows(row0 + t * MOE_TM, MOE_TM)], stage.at[t], sems.at[t])

    def out_copy(t):
        return pltpu.make_async_copy(
            stage.at[t], out_hbm.at[_tile_rows(row0 + t * MOE_TM, MOE_TM)], sems.at[t])

    @pl.when(jnp.logical_and(j == 0, nt > 0))
    def _load_rows():
        for t in range(MOE_R):
            @pl.when(t < nt)
            def _():
                x_copy(t).start()
        for t in range(MOE_R):
            @pl.when(t < nt)
            def _():
                x_copy(t).wait()
                for a in range(PACK_SLABS):
                    hi, lo = _unpack_slab(stage.at[t], a, MOE_TM)
                    x_b[t, :, a * LANES:(a + 1) * LANES] = hi.astype(BF16)
                    x_b[t, :, PACK_HALF + a * LANES:PACK_HALF + (a + 1) * LANES] = lo.astype(BF16)
                acc[t] = jnp.broadcast_to(bd_ref[...], acc.shape[1:])

    @pl.when(nt > 0)
    def _compute():
        wg_b[...] = wg_ref[...].astype(BF16)
        wu_b[...] = wu_ref[...].astype(BF16)
        wd_b[...] = wd_ref[...].astype(BF16)
        bg = bg_ref[...]
        bu = bu_ref[...]

        def tile(t, carry):
            x = x_b[t]
            g = jnp.dot(x, wg_b[...], preferred_element_type=F32) + bg
            u = jnp.dot(x, wu_b[...], preferred_element_type=F32) + bu
            g = jnp.minimum(g, SWIGLU_LIMIT)
            u = jnp.clip(u, -SWIGLU_LIMIT, SWIGLU_LIMIT)
            hidden = ((u + 1.0) * (g * jax.nn.sigmoid(SWIGLU_ALPHA * g))).astype(BF16)
            acc[t] += jnp.dot(hidden, wd_b[...], preferred_element_type=F32)
            return carry

        lax.fori_loop(0, nt, tile, 0)

    @pl.when(jnp.logical_and(last_j, nt > 0))
    def _pack_out():
        for t in range(MOE_R):
            @pl.when(t < nt)
            def _():
                _pack_rows(acc[t], stage.at[t])

    @pl.when(jnp.logical_and(j == 0, nz > 0))
    def _zero_tail():
        for t in range(MOE_R):
            @pl.when(t < nz)
            def _():
                stage[t] = jnp.zeros(stage.shape[1:], U32)

    n_out = jnp.where(last_j, nt, 0) + jnp.where(j == 0, nz, 0)

    @pl.when(n_out > 0)
    def _store_rows():
        for t in range(MOE_R):
            @pl.when(t < n_out)
            def _():
                out_copy(t).start()
        for t in range(MOE_R):
            @pl.when(t < n_out)
            def _():
                out_copy(t).wait()


def _experts(item_e, item_row0, item_nt, item_nz, xs, w_gate, b_gate, w_up, b_up, w_down, b_down, n_items):
    d = D_MODEL
    n_ff = D_FF // MOE_TF
    last = n_ff - 1

    def ff_idx(w, j, nt):
        return jnp.where(nt[w] > 0, j, last)

    col_w = lambda w, j, ie, r0, nt, nz:(ie[w], 0, ff_idx(w, j, nt))
    row_w = lambda w, j, ie, r0, nt, nz:(ie[w], ff_idx(w, j, nt), 0)
    full_b = lambda w, j, ie, r0, nt, nz:(ie[w], 0, 0)
    return pl.pallas_call(
        _experts_kernel,
        grid_spec=pltpu.PrefetchScalarGridSpec(
            num_scalar_prefetch=4,
            grid=(n_items, n_ff),
            in_specs=[
                pl.BlockSpec(memory_space=pl.ANY),
                pl.BlockSpec((None, d, MOE_TF), col_w),
                pl.BlockSpec((None, 1, MOE_TF), col_w),
                pl.BlockSpec((None, d, MOE_TF), col_w),
                pl.BlockSpec((None, 1, MOE_TF), col_w),
                pl.BlockSpec((None, MOE_TF, d), row_w),
                pl.BlockSpec((None, 1, d), full_b),
            ],
            out_specs=pl.BlockSpec(memory_space=pl.ANY),
            scratch_shapes=[
                pltpu.VMEM((MOE_R, MOE_TM * PACK_SLABS, LANES), U32),
                pltpu.VMEM((MOE_R, MOE_TM, d), BF16),
                pltpu.VMEM((MOE_R, MOE_TM, d), F32),
                pltpu.VMEM((d, MOE_TF), BF16),
                pltpu.VMEM((d, MOE_TF), BF16),
                pltpu.VMEM((MOE_TF, d), BF16),
                pltpu.SemaphoreType.DMA((MOE_R,)),
            ],
        ),
        out_shape=jax.ShapeDtypeStruct(xs.shape, U32),
        compiler_params=pltpu.CompilerParams(
            dimension_semantics=("arbitrary", "arbitrary"), vmem_limit_bytes=VMEM_LIMIT_BYTES),
        name="experts",
    )(item_e, item_row0, item_nt, item_nz, xs, w_gate, b_gate, w_up, b_up, w_down, b_down)


def _combine_kernel(dest_ref, rows_hbm, gates_ref, x1_ref, gpost_ref, gate_ref, o_ref, buf, sem):
    tt = COMBINE_TT
    base = (pl.program_id(0) * pl.num_programs(1) + pl.program_id(1)) * tt * TOP_K

    def row_copy(i, kk):
        return pltpu.make_async_copy(
            rows_hbm.at[_tile_rows(dest_ref[base + i * TOP_K + kk])], buf.at[kk, _tile_rows(i)], sem)

    def start(i, c):
        for kk in range(TOP_K):
            row_copy(i, kk).start()
        return c

    def wait(i, c):
        for kk in range(TOP_K):
            row_copy(i, kk).wait()
        return c

    lax.fori_loop(0, tt, start, 0, unroll=4)
    lax.fori_loop(0, tt, wait, 0, unroll=4)
    gates = gates_ref[...]
    y_hi, y_lo = [], []
    for a in range(PACK_SLABS):
        acc_hi = acc_lo = None
        for kk in range(TOP_K):
            hi, lo = _unpack_slab(buf.at[kk], a, tt)
            gk = gates[:, kk:kk + 1]
            acc_hi = gk * hi if acc_hi is None else acc_hi + gk * hi
            acc_lo = gk * lo if acc_lo is None else acc_lo + gk * lo
        y_hi.append(acc_hi)
        y_lo.append(acc_lo)
    y = jnp.concatenate(y_hi + y_lo, axis=1)
    o_ref[...] = x1_ref[...] + gate_ref[...] * _rms(y, gpost_ref[...])


def _combine(dest, rows, gates, x1, gpost, gate):
    b, s, d = x1.shape
    tt = COMBINE_TT
    row = lambda bi, i, dst: (bi, i, 0)
    return pl.pallas_call(
        _combine_kernel,
        grid_spec=pltpu.PrefetchScalarGridSpec(
            num_scalar_prefetch=1,
            grid=(b, s // tt),
            in_specs=[
                pl.BlockSpec(memory_space=pl.ANY),
                pl.BlockSpec((None, tt, TOP_K), row),
                pl.BlockSpec((None, tt, d), row),
                pl.BlockSpec((1, d), lambda bi, i, dst: (0, 0)),
                pl.BlockSpec((None, 1, d), lambda bi, i, dst: (bi, 0, 0)),
            ],
            out_specs=pl.BlockSpec((None, tt, d), row),
            scratch_shapes=[pltpu.VMEM((TOP_K, tt * PACK_SLABS, LANES), U32), pltpu.SemaphoreType.DMA],
        ),
        out_shape=jax.ShapeDtypeStruct((b, s, d), F32),
        compiler_params=pltpu.CompilerParams(
            dimension_semantics=("arbitrary", "arbitrary"), vmem_limit_bytes=VMEM_LIMIT_BYTES),
        name="combine",
    )(dest, rows, gates, x1, gpost, gate)


def _rope_tables(seq_len):
    rows = seq_len // GRID_W
    row_idx = jnp.repeat(jnp.arange(rows, dtype=F32), GRID_W)
    col_idx = jnp.tile(jnp.arange(GRID_W, dtype=F32), rows)
    half = HEAD_DIM // 2
    inv_freq = 1.0 / (ROPE_THETA ** (jnp.arange(0, half, 2, dtype=F32) / half))
    ang = jnp.concatenate([row_idx[:, None] * inv_freq, col_idx[:, None] * inv_freq], axis=-1)
    cos = jnp.repeat(jnp.cos(ang), 2, axis=-1)
    sin = jnp.sin(ang)
    sin_signed = jnp.stack([-sin, sin], axis=-1).reshape(seq_len, HEAD_DIM)
    return cos, sin_signed


def _dft_tables(n):
    idx = np.arange(n, dtype=np.int64)
    ang = 2.0 * np.pi * ((idx[:, None] * idx[None, :]) % n).astype(np.float64) / n
    scale = 1.0 / np.sqrt(n)
    return np.cos(ang) * scale, np.sin(ang) * scale


def _route(logits, n_tok):
    top_val, top_idx = lax.top_k(logits, TOP_K)
    gates = jax.nn.softmax(top_val, axis=-1)
    flat_e = top_idx.reshape(-1).astype(jnp.int32)
    onehot = (flat_e[:, None] == jnp.arange(N_EXPERTS, dtype=jnp.int32)[None, :]).astype(jnp.int32)
    csum = jnp.cumsum(onehot, axis=0)
    rank = jnp.sum(csum * onehot, axis=1) - 1
    counts = csum[-1]
    tiles = (counts + MOE_TM - 1) // MOE_TM
    padded = tiles * MOE_TM
    pad_end = jnp.cumsum(padded)
    pad_start = pad_end - padded
    dest = pad_start[flat_e] + rank
    n_rows = n_tok * TOP_K + N_EXPERTS * MOE_TM
    fill_lo = (pad_start + counts).astype(jnp.int32)
    fill_hi = pad_end.astype(jnp.int32)
    used_tiles = jnp.sum(tiles).astype(jnp.int32).reshape(1)
    n_items = N_EXPERTS + n_rows // MOE_RS
    items_per_e = (tiles + MOE_R - 1) // MOE_R
    item_end = jnp.cumsum(items_per_e)
    item_start = item_end - items_per_e
    w = jnp.arange(n_items, dtype=jnp.int32)
    total = item_end[-1]
    e_of_w = jnp.minimum(jnp.searchsorted(item_end, w, side='right'), N_EXPERTS - 1).astype(jnp.int32)
    local = w - item_start[e_of_w]
    active = w < total
    last_e = e_of_w[jnp.maximum(total - 1, 0)]
    item_e = jnp.where(active, e_of_w, last_e).astype(jnp.int32)
    item_nt = jnp.where(active, jnp.minimum(MOE_R, tiles[e_of_w] - local * MOE_R), 0).astype(jnp.int32)
    n_tiles = n_rows // MOE_TM
    fill_tile0 = jnp.sum(tiles) + (w - total) * MOE_R
    item_nz = jnp.where(active, 0, jnp.clip(n_tiles - fill_tile0, 0, MOE_R)).astype(jnp.int32)
    fill_row0 = jnp.minimum(fill_tile0, n_tiles - 1) * MOE_TM
    item_row0 = jnp.where(active, pad_start[e_of_w] + local * MOE_RS, fill_row0).astype(jnp.int32)
    fill = (fill_lo, fill_hi, used_tiles)
    items = (item_e, item_row0, item_nt, item_nz)
    return gates, dest.astype(jnp.int32), fill, items, n_rows, n_items


def kernel(x, c, w_ada, b_ada, g_pre_mix, w_in, w_fourier, q_norm_g, k_norm_g, g_fourier_out, g_attn_out,
           w_out, g_post_mix, g_pre_ffn, w_router, b_router, w_gate, b_gate, w_up, b_up, w_down, b_down,
           g_post_ffn):
    b, s, d = x.shape
    n_tok = b * s
    depth = w_ada.shape[0]
    cos, sin_signed = _rope_tables(s)
    cs_np, ss_np = _dft_tables(s)
    cs = jnp.asarray(cs_np, dtype=BF16)
    ss = jnp.asarray(ss_np, dtype=BF16)
    cc_np, sc_np = _dft_tables(FOURIER_GROUP_DIM)
    cc = jnp.asarray(np.concatenate([cc_np, -sc_np], axis=0), dtype=BF16)
    c_pad = jnp.pad(c, ((0, 8 - b), (0, 0)))
    row2 = lambda a: a.reshape(1, -1)

    for l in range(depth):
        mod = _ada(c_pad, w_ada[l], row2(b_ada[l]))[:b].reshape(b, N_MOD, 1, d)
        shift_m, scale_m, gate_m = mod[:, 0], mod[:, 1], mod[:, 2]
        shift_f, scale_f, gate_f = mod[:, 3], mod[:, 4], mod[:, 5]

        f, q, k, v = _inproj(
            x, row2(g_pre_mix[l]), shift_m, scale_m, w_in[l].astype(BF16),
            row2(q_norm_g[l] * (HEAD_DIM ** -0.5)), row2(k_norm_g[l]), cos, sin_signed)
        fo = _fourier(f, cs, ss, cc, w_fourier[l], row2(g_fourier_out[l]))
        ao = _attn(q, k, v, row2(g_attn_out[l]))

        wr = jnp.pad(w_router[l], ((0, 0), (0, ROUTER_PAD - N_EXPERTS)))
        wr_hi = wr.astype(BF16)
        wr_lo = (wr - wr_hi.astype(F32)).astype(BF16)
        br = jnp.pad(b_router[l], (0, ROUTER_PAD - N_EXPERTS)).reshape(1, ROUTER_PAD)
        x1, h2p, logits = _outproj(
            fo, ao, w_out[l].astype(BF16), x, row2(g_post_mix[l]), gate_m, row2(g_pre_ffn[l]),
            shift_f, scale_f, wr_hi, wr_lo, br)

        logits = logits.reshape(n_tok, ROUTER_PAD)[:, :N_EXPERTS]
        gates, dest, fill, items, n_rows, n_items = _route(logits, n_tok)
        xs = _dispatch(dest, *fill, h2p, n_rows)
        rows = _experts(
            *items, xs,
            w_gate[l], b_gate[l].reshape(N_EXPERTS, 1, D_FF), w_up[l], b_up[l].reshape(N_EXPERTS, 1, D_FF),
            w_down[l], b_down[l].reshape(N_EXPERTS, 1, d), n_items)
        x = _combine(dest, rows, gates.reshape(b, s, TOP_K), x1, row2(g_post_ffn[l]), gate_f)
    return x
```

```python
import functools

import numpy as np
import jax
import jax.numpy as jnp
from jax import lax
from jax.experimental import pallas as pl
from jax.experimental.pallas import tpu as pltpu

F32 = jnp.float32
BF16 = jnp.bfloat16
U32 = jnp.uint32

D_MODEL = 2048
GRID_W = 64
FOURIER_WIDTH = 512
FOURIER_GROUP_DIM = 128
FOURIER_GROUPS = 4
ATTN_WIDTH = 1536
HEAD_DIM = 128
N_Q_HEADS = 12
GQA_GROUP = 3
N_KV_HEADS = 4
KV_WIDTH = 512
IN_WIDTH = 3072
ROPE_THETA = 10000.0
N_EXPERTS = 32
TOP_K = 4
D_FF = 2048
SWIGLU_LIMIT = 7.0
SWIGLU_ALPHA = 1.702
N_MOD = 6
EPS = 1e-6

VMEM_LIMIT_BYTES = 56 * 1024 * 1024
LANES = 128

ADA_TN = 1024
INPROJ_TM = 256
FOURIER_TR = 256
ATTN_TQ = 128
OUTPROJ_TM = 256
ROUTER_PAD = LANES
MOE_TM = 256
MOE_R = 5
MOE_RS = MOE_TM * MOE_R
MOE_TF = 256
MOE_N_FF = D_FF // MOE_TF
DISPATCH_TT = 512
COMBINE_TT = 128
PACK_HALF = D_MODEL // 2
PACK_SLABS = PACK_HALF // LANES


def _rms(x, g):
    return x * lax.rsqrt(jnp.mean(x * x, axis=-1, keepdims=True) + EPS) * g


def _ada_kernel(c_ref, w_ref, b_ref, o_ref):
    c = c_ref[...]
    s = (c * jax.nn.sigmoid(c)).astype(BF16)
    o_ref[...] = jnp.dot(s, w_ref[...].astype(BF16), preferred_element_type=F32) + b_ref[...]


def _ada(c_pad, w_ada, b_ada):
    m, d = c_pad.shape
    n = w_ada.shape[1]
    return pl.pallas_call(
        _ada_kernel,
        grid=(n // ADA_TN,),
        in_specs=[
            pl.BlockSpec((m, d), lambda j: (0, 0)),
            pl.BlockSpec((d, ADA_TN), lambda j: (0, j)),
            pl.BlockSpec((1, ADA_TN), lambda j: (0, j)),
        ],
        out_specs=pl.BlockSpec((m, ADA_TN), lambda j: (0, j)),
        out_shape=jax.ShapeDtypeStruct((m, n), F32),
        compiler_params=pltpu.CompilerParams(
            dimension_semantics=("arbitrary",), vmem_limit_bytes=VMEM_LIMIT_BYTES),
        name="ada",
    )(c_pad, w_ada, b_ada)


def _rope(p, cos, sin_signed, even_lane):
    partner = jnp.where(even_lane, pltpu.roll(p, LANES - 1, 1), pltpu.roll(p, 1, 1))
    return p * cos + partner * sin_signed


def _inproj_kernel(x_ref, g_ref, shift_ref, scale_ref, w_ref, qg_ref, kg_ref, cos_ref, sin_ref,
                   f_ref, q_ref, k_ref, v_ref):
    x = x_ref[...]
    h = _rms(x, g_ref[...]) * (1.0 + scale_ref[...]) + shift_ref[...]
    hb = h.astype(BF16)
    cos = cos_ref[...]
    sin = sin_ref[...]
    even_lane = (lax.broadcasted_iota(jnp.int32, cos.shape, 1) % 2) == 0
    chunk = 512
    heads_per_chunk = chunk // HEAD_DIM
    for ci in range(IN_WIDTH // chunk):
        p = jnp.dot(hb, w_ref[:, ci * chunk:(ci + 1) * chunk], preferred_element_type=F32)
        if ci == 0:
            f_ref[...] = p.astype(BF16)
        elif ci == 5:
            v_ref[...] = p.astype(BF16)
        else:
            gain = kg_ref[...] if ci == 4 else qg_ref[...]
            for hh in range(heads_per_chunk):
                ph = p[:, hh * HEAD_DIM:(hh + 1) * HEAD_DIM]
                ph = _rope(_rms(ph, gain), cos, sin, even_lane).astype(BF16)
                if ci == 4:
                    k_ref[:, hh * HEAD_DIM:(hh + 1) * HEAD_DIM] = ph
                else:
                    c0 = (ci - 1) * chunk + hh * HEAD_DIM
                    q_ref[:, c0:c0 + HEAD_DIM] = ph


def _inproj(x, g, shift, scale, w_in_b, qg, kg, cos, sin):
    b, s, d = x.shape
    tm = INPROJ_TM
    row = lambda bi, i: (bi, i, 0)
    per_batch = lambda bi, i: (bi, 0, 0)
    const2 = lambda bi, i: (0, 0)
    return pl.pallas_call(
        _inproj_kernel,
        grid=(b, s // tm),
        in_specs=[
            pl.BlockSpec((None, tm, d), row),
            pl.BlockSpec((1, d), const2),
            pl.BlockSpec((None, 1, d), per_batch),
            pl.BlockSpec((None, 1, d), per_batch),
            pl.BlockSpec((d, IN_WIDTH), const2),
            pl.BlockSpec((1, HEAD_DIM), const2),
            pl.BlockSpec((1, HEAD_DIM), const2),
            pl.BlockSpec((tm, HEAD_DIM), lambda bi, i: (i, 0)),
            pl.BlockSpec((tm, HEAD_DIM), lambda bi, i: (i, 0)),
        ],
        out_specs=[
            pl.BlockSpec((None, tm, FOURIER_WIDTH), row),
            pl.BlockSpec((None, tm, ATTN_WIDTH), row),
            pl.BlockSpec((None, tm, KV_WIDTH), row),
            pl.BlockSpec((None, tm, KV_WIDTH), row),
        ],
        out_shape=[
            jax.ShapeDtypeStruct((b, s, FOURIER_WIDTH), BF16),
            jax.ShapeDtypeStruct((b, s, ATTN_WIDTH), BF16),
            jax.ShapeDtypeStruct((b, s, KV_WIDTH), BF16),
            jax.ShapeDtypeStruct((b, s, KV_WIDTH), BF16),
        ],
        compiler_params=pltpu.CompilerParams(
            dimension_semantics=("arbitrary", "arbitrary"), vmem_limit_bytes=VMEM_LIMIT_BYTES),
        name="inproj",
    )(x, g, shift, scale, w_in_b, qg, kg, cos, sin)


def _fourier_kernel(cs_ref, ss_ref, f_ref, cc_ref, w_ref, g_ref, o_ref):
    f = f_ref[...]
    a = jnp.dot(cs_ref[...], f, preferred_element_type=F32)
    bm = jnp.dot(ss_ref[...], f, preferred_element_type=F32)
    cc = cc_ref[...]
    outs = []
    for gi in range(FOURIER_GROUPS):
        sl = slice(gi * FOURIER_GROUP_DIM, (gi + 1) * FOURIER_GROUP_DIM)
        ab = jnp.concatenate([a[:, sl], bm[:, sl]], axis=1).astype(BF16)
        fm = jnp.dot(ab, cc, preferred_element_type=F32)
        outs.append(jnp.dot(fm.astype(BF16), w_ref[gi].astype(BF16), preferred_element_type=F32))
    fo = jnp.concatenate(outs, axis=1)
    o_ref[...] = _rms(fo, g_ref[...]).astype(BF16)


def _fourier(f, cs, ss, cc, w_fourier, g):
    b, s, fw = f.shape
    tr = FOURIER_TR
    return pl.pallas_call(
        _fourier_kernel,
        grid=(b, s // tr),
        in_specs=[
            pl.BlockSpec((tr, s), lambda bi, i: (i, 0)),
            pl.BlockSpec((tr, s), lambda bi, i: (i, 0)),
            pl.BlockSpec((None, s, fw), lambda bi, i: (bi, 0, 0)),
            pl.BlockSpec((2 * FOURIER_GROUP_DIM, FOURIER_GROUP_DIM), lambda bi, i: (0, 0)),
            pl.BlockSpec((FOURIER_GROUPS, FOURIER_GROUP_DIM, FOURIER_GROUP_DIM), lambda bi, i: (0, 0, 0)),
            pl.BlockSpec((1, fw), lambda bi, i: (0, 0)),
        ],
        out_specs=pl.BlockSpec((None, tr, fw), lambda bi, i: (bi, i, 0)),
        out_shape=jax.ShapeDtypeStruct((b, s, fw), BF16),
        compiler_params=pltpu.CompilerParams(
            dimension_semantics=("arbitrary", "arbitrary"), vmem_limit_bytes=VMEM_LIMIT_BYTES),
        name="fourier",
    )(cs, ss, f, cc, w_fourier, g)


def _attn_kernel(q_ref, k_ref, v_ref, g_ref, o_ref, ao_ref):
    tq = q_ref.shape[0]
    for kv in range(N_KV_HEADS):
        kh = k_ref[:, kv * HEAD_DIM:(kv + 1) * HEAD_DIM]
        vh = v_ref[:, kv * HEAD_DIM:(kv + 1) * HEAD_DIM]
        c0 = kv * GQA_GROUP * HEAD_DIM
        qh = jnp.concatenate(
            [q_ref[:, c0 + gi * HEAD_DIM:c0 + (gi + 1) * HEAD_DIM] for gi in range(GQA_GROUP)], axis=0)
        sc = lax.dot_general(qh, kh, (((1,), (1,)), ((), ())), preferred_element_type=F32)
        m = jnp.max(sc, axis=-1, keepdims=True)
        p = jnp.exp(sc - m)
        l = jnp.sum(p, axis=-1, keepdims=True)
        o = jnp.dot(p.astype(BF16), vh, preferred_element_type=F32) / l
        for gi in range(GQA_GROUP):
            ao_ref[:, c0 + gi * HEAD_DIM:c0 + (gi + 1) * HEAD_DIM] = o[gi * tq:(gi + 1) * tq]
    o_ref[...] = _rms(ao_ref[...], g_ref[...]).astype(BF16)


def _attn(q, k, v, g):
    b, s, _ = q.shape
    tq = ATTN_TQ
    return pl.pallas_call(
        _attn_kernel,
        grid=(b, s // tq),
        in_specs=[
            pl.BlockSpec((None, tq, ATTN_WIDTH), lambda bi, i: (bi, i, 0)),
            pl.BlockSpec((None, s, KV_WIDTH), lambda bi, i: (bi, 0, 0)),
            pl.BlockSpec((None, s, KV_WIDTH), lambda bi, i: (bi, 0, 0)),
            pl.BlockSpec((1, ATTN_WIDTH), lambda bi, i: (0, 0)),
        ],
        out_specs=pl.BlockSpec((None, tq, ATTN_WIDTH), lambda bi, i: (bi, i, 0)),
        out_shape=jax.ShapeDtypeStruct((b, s, ATTN_WIDTH), BF16),
        scratch_shapes=[pltpu.VMEM((tq, ATTN_WIDTH), F32)],
        compiler_params=pltpu.CompilerParams(
            dimension_semantics=("arbitrary", "arbitrary"), vmem_limit_bytes=VMEM_LIMIT_BYTES),
        name="attn",
    )(q, k, v, g)


def _split_bf16(a):
    hi = a.astype(BF16)
    lo = (a - hi.astype(F32)).astype(BF16)
    return hi, lo


def _pack_rows(val, dst_ref):
    n = val.shape[0]
    bits = lax.bitcast_convert_type(val.astype(BF16).astype(F32), U32)
    packed = bits[:, :PACK_HALF] | (bits[:, PACK_HALF:] >> 16)
    for a in range(PACK_SLABS):
        dst_ref[pl.ds(a, n, stride=PACK_SLABS), :] = packed[:, a * LANES:(a + 1) * LANES]


def _unpack_slab(src_ref, a, n):
    word = src_ref[pl.ds(a, n, stride=PACK_SLABS), :]
    hi = lax.bitcast_convert_type(word & jnp.uint32(0xFFFF0000), F32)
    lo = lax.bitcast_convert_type(word << 16, F32)
    return hi, lo


def _outproj_kernel(fo_ref, ao_ref, w_ref, x_ref, gpost_ref, gate_ref, gpre_ref, shift_ref, scale_ref,
                    wr_hi_ref, wr_lo_ref, br_ref, x1_ref, h2p_ref, logit_ref):
    mix = jnp.dot(fo_ref[...], w_ref[:FOURIER_WIDTH, :], preferred_element_type=F32)
    mix = mix + jnp.dot(ao_ref[...], w_ref[FOURIER_WIDTH:, :], preferred_element_type=F32)
    x1 = x_ref[...] + gate_ref[...] * _rms(mix, gpost_ref[...])
    x1_ref[...] = x1
    h2 = _rms(x1, gpre_ref[...]) * (1.0 + scale_ref[...]) + shift_ref[...]
    _pack_rows(h2, h2p_ref)
    h_hi, h_lo = _split_bf16(h2)
    lg = jnp.dot(h_hi, wr_hi_ref[...], preferred_element_type=F32)
    lg = lg + jnp.dot(h_hi, wr_lo_ref[...], preferred_element_type=F32)
    lg = lg + jnp.dot(h_lo, wr_hi_ref[...], preferred_element_type=F32)
    logit_ref[...] = lg + br_ref[...]


def _outproj(fo, ao, w_out_b, x, gpost, gate, gpre, shift, scale, wr_hi, wr_lo, br):
    b, s, d = x.shape
    tm = OUTPROJ_TM
    row = lambda bi, i: (bi, i, 0)
    per_batch = lambda bi, i: (bi, 0, 0)
    const2 = lambda bi, i: (0, 0)
    return pl.pallas_call(
        _outproj_kernel,
        grid=(b, s // tm),
        in_specs=[
            pl.BlockSpec((None, tm, FOURIER_WIDTH), row),
            pl.BlockSpec((None, tm, ATTN_WIDTH), row),
            pl.BlockSpec((d, d), const2),
            pl.BlockSpec((None, tm, d), row),
            pl.BlockSpec((1, d), const2),
            pl.BlockSpec((None, 1, d), per_batch),
            pl.BlockSpec((1, d), const2),
            pl.BlockSpec((None, 1, d), per_batch),
            pl.BlockSpec((None, 1, d), per_batch),
            pl.BlockSpec((d, ROUTER_PAD), const2),
            pl.BlockSpec((d, ROUTER_PAD), const2),
            pl.BlockSpec((1, ROUTER_PAD), const2),
        ],
        out_specs=[
            pl.BlockSpec((None, tm, d), row),
            pl.BlockSpec((tm * PACK_SLABS, LANES), lambda bi, i: (bi * (s // tm) + i, 0)),
            pl.BlockSpec((None, tm, ROUTER_PAD), row),
        ],
        out_shape=[
            jax.ShapeDtypeStruct((b, s, d), F32),
            jax.ShapeDtypeStruct((b * s * PACK_SLABS, LANES), U32),
            jax.ShapeDtypeStruct((b, s, ROUTER_PAD), F32),
        ],
        compiler_params=pltpu.CompilerParams(
            dimension_semantics=("arbitrary", "arbitrary"), vmem_limit_bytes=VMEM_LIMIT_BYTES),
        name="outproj",
    )(fo, ao, w_out_b, x, gpost, gate, gpre, shift, scale, wr_hi, wr_lo, br)


def _tile_rows(row, n_rows=1):
    return pl.ds(pl.multiple_of(row * PACK_SLABS, PACK_SLABS), n_rows * PACK_SLABS)


def _dispatch_kernel(dest_ref, fill_lo_ref, fill_hi_ref, used_tiles_ref, h_ref, xs_hbm, zeros, sem, fill_sem):
    step = pl.program_id(0)
    base = step * (DISPATCH_TT * TOP_K)

    def row_copy(i, kk):
        return pltpu.make_async_copy(
            h_ref.at[_tile_rows(i)], xs_hbm.at[_tile_rows(dest_ref[base + i * TOP_K + kk])], sem)

    def start(i, c):
        for kk in range(TOP_K):
            row_copy(i, kk).start()
        return c

    def wait(i, c):
        for kk in range(TOP_K):
            row_copy(i, kk).wait()
        return c

    lax.fori_loop(0, DISPATCH_TT, start, 0, unroll=4)

    @pl.when(step == 0)
    def _fill():
        zeros[...] = jnp.zeros(zeros.shape, U32)
        n_tiles = xs_hbm.shape[0] // (MOE_TM * PACK_SLABS)

        def pad_chunks(e, fn):
            lo = fill_lo_ref[e]
            length = fill_hi_ref[e] - lo
            for bit in (128, 64, 32, 16, 8, 4, 2, 1):
                @pl.when((length & bit) != 0)
                def _():
                    row = lo + (length & ~(2 * bit - 1))
                    fn(pltpu.make_async_copy(
                        zeros.at[pl.ds(0, bit * PACK_SLABS)], xs_hbm.at[_tile_rows(row, bit)], fill_sem))

        def tail_copy(t):
            return pltpu.make_async_copy(zeros, xs_hbm.at[_tile_rows(t * MOE_TM, MOE_TM)], fill_sem)

        def start_e(e, c):
            pad_chunks(e, lambda cp: cp.start())
            return c

        def wait_e(e, c):
            pad_chunks(e, lambda cp: cp.wait())
            return c

        def start_t(t, c):
            tail_copy(t).start()
            return c

        def wait_t(t, c):
            tail_copy(t).wait()
            return c

        lax.fori_loop(0, N_EXPERTS, start_e, 0)
        lax.fori_loop(used_tiles_ref[0], n_tiles, start_t, 0)
        lax.fori_loop(0, N_EXPERTS, wait_e, 0)
        lax.fori_loop(used_tiles_ref[0], n_tiles, wait_t, 0)

    lax.fori_loop(0, DISPATCH_TT, wait, 0, unroll=4)


def _dispatch(dest, fill_lo, fill_hi, used_tiles, h2p, n_rows):
    n_tok = h2p.shape[0] // PACK_SLABS
    return pl.pallas_call(
        _dispatch_kernel,
        grid_spec=pltpu.PrefetchScalarGridSpec(
            num_scalar_prefetch=4,
            grid=(n_tok // DISPATCH_TT,),
            in_specs=[pl.BlockSpec((DISPATCH_TT * PACK_SLABS, LANES), lambda i, *_: (i, 0))],
            out_specs=pl.BlockSpec(memory_space=pl.ANY),
            scratch_shapes=[
                pltpu.VMEM((MOE_TM * PACK_SLABS, LANES), U32),
                pltpu.SemaphoreType.DMA,
                pltpu.SemaphoreType.DMA,
            ],
        ),
        out_shape=jax.ShapeDtypeStruct((n_rows * PACK_SLABS, LANES), U32),
        compiler_params=pltpu.CompilerParams(
            dimension_semantics=("arbitrary",), vmem_limit_bytes=VMEM_LIMIT_BYTES),
        name="dispatch",
    )(dest, fill_lo, fill_hi, used_tiles, h2p)


def _experts_kernel(item_e_ref, item_row0_ref, item_nt_ref, item_nz_ref,
                    xs_hbm, wg_hbm, wu_hbm, wd_hbm, bg_ref, bu_ref, bd_ref, out_hbm,
                    stage, x_b, acc, wf_g0, wf_u0, wf_d0, wf_g1, wf_u1, wf_d1,
                    wb_g0, wb_u0, wb_d0, wb_g1, wb_u1, wb_d1, x_sems, o_sems, w_sems):
    w = pl.program_id(0)
    nt = item_nt_ref[w]
    nz = item_nz_ref[w]
    e = item_e_ref[w]
    row0 = item_row0_ref[w]
    w_prev = jnp.maximum(w - 1, 0)
    row0_prev, nt_prev = item_row0_ref[w_prev], item_nt_ref[w_prev]
    e_next, row0_next, nt_next = item_e_ref[w + 1], item_row0_ref[w + 1], item_nt_ref[w + 1]
    has_next = nt_next > 0
    slot = w & 1
    wf = ((wf_g0, wf_u0, wf_d0), (wf_g1, wf_u1, wf_d1))
    wb = ((wb_g0, wb_u0, wb_d0), (wb_g1, wb_u1, wb_d1))

    def w_copies(ee, j, ws):
        col = pl.ds(pl.multiple_of(j * MOE_TF, MOE_TF), MOE_TF)
        f_g, f_u, f_d = wf[ws]
        return (pltpu.make_async_copy(wg_hbm.at[ee, :, col], f_g, w_sems.at[ws, 0]),
                pltpu.make_async_copy(wu_hbm.at[ee, :, col], f_u, w_sems.at[ws, 1]),
                pltpu.make_async_copy(wd_hbm.at[ee, col, :], f_d, w_sems.at[ws, 2]))

    def w_start(ee, j, ws):
        for cp in w_copies(ee, j, ws):
            cp.start()

    def w_wait(ee, j, ws):
        for cp in w_copies(ee, j, ws):
            cp.wait()

    def w_cast(ws):
        for f_ref, b_ref in zip(wf[ws], wb[ws]):
            b_ref[...] = f_ref[...].astype(BF16)

    def x_copy(r0, t, ss):
        return pltpu.make_async_copy(
            xs_hbm.at[_tile_rows(r0 + t * MOE_TM, MOE_TM)], stage.at[ss, t], x_sems.at[ss, t])

    def out_copy(r0, t, ss):
        return pltpu.make_async_copy(
            stage.at[ss, t], out_hbm.at[_tile_rows(r0 + t * MOE_TM, MOE_TM)], o_sems.at[ss, t])

    def for_tiles(n, fn):
        for t in range(MOE_R):
            @pl.when(t < n)
            def _():
                fn(t)

    def tile(t, j, ws):
        b_g, b_u, b_d = wb[ws]
        x = x_b[t]
        g = jnp.dot(x, b_g[...], preferred_element_type=F32) + bg_ref[j]
        u = jnp.dot(x, b_u[...], preferred_element_type=F32) + bu_ref[j]
        g = jnp.minimum(g, SWIGLU_LIMIT)
        u = jnp.clip(u, -SWIGLU_LIMIT, SWIGLU_LIMIT)
        hidden = ((u + 1.0) * (g * jax.nn.sigmoid(SWIGLU_ALPHA * g))).astype(BF16)
        acc[t] += jnp.dot(hidden, b_d[...], preferred_element_type=F32)

    def ff_step(j, ws):
        @pl.when(j + 2 < MOE_N_FF)
        def _():
            w_start(e, j + 2, ws)

        @pl.when(jnp.logical_and(j + 2 >= MOE_N_FF, has_next))
        def _():
            w_start(e_next, j + 2 - MOE_N_FF, ws)

        @pl.when(j + 1 < MOE_N_FF)
        def _():
            w_wait(e, j + 1, 1 - ws)

        @pl.when(jnp.logical_and(j + 1 >= MOE_N_FF, has_next))
        def _():
            w_wait(e_next, 0, 1 - ws)

        tile(0, j, ws)
        w_cast(1 - ws)

        def rest(t, c):
            tile(t, j, ws)
            return c

        lax.fori_loop(1, nt, rest, 0)

    @pl.when(nt > 0)
    def _active():
        @pl.when(w == 0)
        def _prologue():
            w_start(e, 0, 0)
            w_start(e, 1, 1)
            for_tiles(nt, lambda t: x_copy(row0, t, 0).start())
            w_wait(e, 0, 0)
            w_cast(0)

        def load_tile(t):
            x_copy(row0, t, slot).wait()
            for a in range(PACK_SLABS):
                hi, lo = _unpack_slab(stage.at[slot, t], a, MOE_TM)
                x_b[t, :, a * LANES:(a + 1) * LANES] = hi.astype(BF16)
                x_b[t, :, PACK_HALF + a * LANES:PACK_HALF + (a + 1) * LANES] = lo.astype(BF16)
            acc[t] = jnp.broadcast_to(bd_ref[...], acc.shape[1:])

        for_tiles(nt, load_tile)

        def ff_pair(jj, c):
            ff_step(2 * jj, 0)

            @pl.when(jj == 0)
            def _():
                @pl.when(w > 0)
                def _():
                    for_tiles(nt_prev, lambda t: out_copy(row0_prev, t, 1 - slot).wait())

                @pl.when(has_next)
                def _():
                    for_tiles(nt_next, lambda t: x_copy(row0_next, t, 1 - slot).start())

            ff_step(2 * jj + 1, 1)
            return c

        lax.fori_loop(0, MOE_N_FF // 2, ff_pair, 0)

        def store_tile(t):
            _pack_rows(acc[t], stage.at[slot, t])
            out_copy(row0, t, slot).start()

        for_tiles(nt, store_tile)

        @pl.when(jnp.logical_not(has_next))
        def _():
            for_tiles(nt, lambda t: out_copy(row0, t, slot).wait())

    @pl.when(nz > 0)
    def _zero_tail():
        def zero_tile(t):
            stage[0, t] = jnp.zeros(stage.shape[2:], U32)
            out_copy(row0, t, 0).start()

        for_tiles(nz, zero_tile)
        for_tiles(nz, lambda t: out_copy(row0, t, 0).wait())


def _experts(item_e, item_row0, item_nt, item_nz, xs, w_gate, b_gate, w_up, b_up, w_down, b_down, n_items):
    d = D_MODEL
    w_shapes = [(d, MOE_TF), (d, MOE_TF), (MOE_TF, d)]
    by_expert = lambda w, ie, r0, nt, nz: (ie[w], 0, 0, 0)
    return pl.pallas_call(
        _experts_kernel,
        grid_spec=pltpu.PrefetchScalarGridSpec(
            num_scalar_prefetch=4,
            grid=(n_items,),
            in_specs=[
                pl.BlockSpec(memory_space=pl.ANY),
                pl.BlockSpec(memory_space=pl.ANY),
                pl.BlockSpec(memory_space=pl.ANY),
                pl.BlockSpec(memory_space=pl.ANY),
                pl.BlockSpec((None, MOE_N_FF, 1, MOE_TF), by_expert),
                pl.BlockSpec((None, MOE_N_FF, 1, MOE_TF), by_expert),
                pl.BlockSpec((None, 1, d), lambda w, ie, r0, nt, nz: (ie[w], 0, 0)),
            ],
            out_specs=pl.BlockSpec(memory_space=pl.ANY),
            scratch_shapes=[
                pltpu.VMEM((2, MOE_R, MOE_TM * PACK_SLABS, LANES), U32),
                pltpu.VMEM((MOE_R, MOE_TM, d), BF16),
                pltpu.VMEM((MOE_R, MOE_TM, d), F32),
                *[pltpu.VMEM(shape, F32) for shape in w_shapes * 2],
                *[pltpu.VMEM(shape, BF16) for shape in w_shapes * 2],
                pltpu.SemaphoreType.DMA((2, MOE_R)),
                pltpu.SemaphoreType.DMA((2, MOE_R)),
                pltpu.SemaphoreType.DMA((2, 3)),
            ],
        ),
        out_shape=jax.ShapeDtypeStruct(xs.shape, U32),
        compiler_params=pltpu.CompilerParams(
            dimension_semantics=("arbitrary",), vmem_limit_bytes=VMEM_LIMIT_BYTES),
        name="experts",
    )(item_e, item_row0, item_nt, item_nz, xs, w_gate, w_up, w_down,
      b_gate.reshape(N_EXPERTS, MOE_N_FF, 1, MOE_TF), b_up.reshape(N_EXPERTS, MOE_N_FF, 1, MOE_TF),
      b_down.reshape(N_EXPERTS, 1, d))


def _combine_kernel(dest_ref, rows_hbm, gates_ref, x1_ref, gpost_ref, gate_ref, o_ref, buf, sem):
    tt = COMBINE_TT
    base = (pl.program_id(0) * pl.num_programs(1) + pl.program_id(1)) * tt * TOP_K

    def row_copy(i, kk):
        return pltpu.make_async_copy(
            rows_hbm.at[_tile_rows(dest_ref[base + i * TOP_K + kk])], buf.at[kk, _tile_rows(i)], sem)

    def start(i, c):
        for kk in range(TOP_K):
            row_copy(i, kk).start()
        return c

    def wait(i, c):
        for kk in range(TOP_K):
            row_copy(i, kk).wait()
        return c

    lax.fori_loop(0, tt, start, 0, unroll=4)
    lax.fori_loop(0, tt, wait, 0, unroll=4)
    gates = gates_ref[...]
    y_hi, y_lo = [], []
    for a in range(PACK_SLABS):
        acc_hi = acc_lo = None
        for kk in range(TOP_K):
            hi, lo = _unpack_slab(buf.at[kk], a, tt)
            gk = gates[:, kk:kk + 1]
            acc_hi = gk * hi if acc_hi is None else acc_hi + gk * hi
            acc_lo = gk * lo if acc_lo is None else acc_lo + gk * lo
        y_hi.append(acc_hi)
        y_lo.append(acc_lo)
    y = jnp.concatenate(y_hi + y_lo, axis=1)
    o_ref[...] = x1_ref[...] + gate_ref[...] * _rms(y, gpost_ref[...])


def _combine(dest, rows, gates, x1, gpost, gate):
    b, s, d = x1.shape
    tt = COMBINE_TT
    row = lambda bi, i, dst: (bi, i, 0)
    return pl.pallas_call(
        _combine_kernel,
        grid_spec=pltpu.PrefetchScalarGridSpec(
            num_scalar_prefetch=1,
            grid=(b, s // tt),
            in_specs=[
                pl.BlockSpec(memory_space=pl.ANY),
                pl.BlockSpec((None, tt, TOP_K), row),
                pl.BlockSpec((None, tt, d), row),
                pl.BlockSpec((1, d), lambda bi, i, dst: (0, 0)),
                pl.BlockSpec((None, 1, d), lambda bi, i, dst: (bi, 0, 0)),
            ],
            out_specs=pl.BlockSpec((None, tt, d), row),
            scratch_shapes=[pltpu.VMEM((TOP_K, tt * PACK_SLABS, LANES), U32), pltpu.SemaphoreType.DMA],
        ),
        out_shape=jax.ShapeDtypeStruct((b, s, d), F32),
        compiler_params=pltpu.CompilerParams(
            dimension_semantics=("arbitrary", "arbitrary"), vmem_limit_bytes=VMEM_LIMIT_BYTES),
        name="combine",
    )(dest, rows, gates, x1, gpost, gate)


def _rope_tables(seq_len):
    rows = seq_len // GRID_W
    row_idx = jnp.repeat(jnp.arange(rows, dtype=F32), GRID_W)
    col_idx = jnp.tile(jnp.arange(GRID_W, dtype=F32), rows)
    half = HEAD_DIM // 2
    inv_freq = 1.0 / (ROPE_THETA ** (jnp.arange(0, half, 2, dtype=F32) / half))
    ang = jnp.concatenate([row_idx[:, None] * inv_freq, col_idx[:, None] * inv_freq], axis=-1)
    cos = jnp.repeat(jnp.cos(ang), 2, axis=-1)
    sin = jnp.sin(ang)
    sin_signed = jnp.stack([-sin, sin], axis=-1).reshape(seq_len, HEAD_DIM)
    return cos, sin_signed


def _dft_tables(n):
    idx = np.arange(n, dtype=np.int64)
    ang = 2.0 * np.pi * ((idx[:, None] * idx[None, :]) % n).astype(np.float64) / n
    scale = 1.0 / np.sqrt(n)
    return np.cos(ang) * scale, np.sin(ang) * scale


def _route(logits, n_tok):
    top_val, top_idx = lax.top_k(logits, TOP_K)
    gates = jax.nn.softmax(top_val, axis=-1)
    flat_e = top_idx.reshape(-1).astype(jnp.int32)
    onehot = (flat_e[:, None] == jnp.arange(N_EXPERTS, dtype=jnp.int32)[None, :]).astype(jnp.int32)
    csum = jnp.cumsum(onehot, axis=0)
    rank = jnp.sum(csum * onehot, axis=1) - 1
    counts = csum[-1]
    tiles = (counts + MOE_TM - 1) // MOE_TM
    padded = tiles * MOE_TM
    pad_end = jnp.cumsum(padded)
    pad_start = pad_end - padded
    dest = pad_start[flat_e] + rank
    n_rows = n_tok * TOP_K + N_EXPERTS * MOE_TM
    fill_lo = (pad_start + counts).astype(jnp.int32)
    fill_hi = pad_end.astype(jnp.int32)
    used_tiles = jnp.sum(tiles).astype(jnp.int32).reshape(1)
    n_items = N_EXPERTS + n_rows // MOE_RS
    items_per_e = (tiles + MOE_R - 1) // MOE_R
    item_end = jnp.cumsum(items_per_e)
    item_start = item_end - items_per_e
    w = jnp.arange(n_items, dtype=jnp.int32)
    total = item_end[-1]
    e_of_w = jnp.minimum(jnp.searchsorted(item_end, w, side='right'), N_EXPERTS - 1).astype(jnp.int32)
    local = w - item_start[e_of_w]
    active = w < total
    last_e = e_of_w[jnp.maximum(total - 1, 0)]
    item_e = jnp.where(active, e_of_w, last_e).astype(jnp.int32)
    item_nt = jnp.where(active, jnp.minimum(MOE_R, tiles[e_of_w] - local * MOE_R), 0).astype(jnp.int32)
    n_tiles = n_rows // MOE_TM
    fill_tile0 = jnp.sum(tiles) + (w - total) * MOE_R
    item_nz = jnp.where(active, 0, jnp.clip(n_tiles - fill_tile0, 0, MOE_R)).astype(jnp.int32)
    fill_row0 = jnp.minimum(fill_tile0, n_tiles - 1) * MOE_TM
    item_row0 = jnp.where(active, pad_start[e_of_w] + local * MOE_RS, fill_row0).astype(jnp.int32)
    fill = (fill_lo, fill_hi, used_tiles)
    items = tuple(jnp.pad(a, (0, 1)) for a in (item_e, item_row0, item_nt, item_nz))
    return gates, dest.astype(jnp.int32), fill, items, n_rows, n_items


def kernel(x, c, w_ada, b_ada, g_pre_mix, w_in, w_fourier, q_norm_g, k_norm_g, g_fourier_out, g_attn_out,
           w_out, g_post_mix, g_pre_ffn, w_router, b_router, w_gate, b_gate, w_up, b_up, w_down, b_down,
           g_post_ffn):
    b, s, d = x.shape
    n_tok = b * s
    depth = w_ada.shape[0]
    cos, sin_signed = _rope_tables(s)
    cs_np, ss_np = _dft_tables(s)
    cs = jnp.asarray(cs_np, dtype=BF16)
    ss = jnp.asarray(ss_np, dtype=BF16)
    cc_np, sc_np = _dft_tables(FOURIER_GROUP_DIM)
    cc = jnp.asarray(np.concatenate([cc_np, -sc_np], axis=0), dtype=BF16)
    c_pad = jnp.pad(c, ((0, 8 - b), (0, 0)))
    row2 = lambda a: a.reshape(1, -1)

    for l in range(depth):
        mod = _ada(c_pad, w_ada[l], row2(b_ada[l]))[:b].reshape(b, N_MOD, 1, d)
        shift_m, scale_m, gate_m = mod[:, 0], mod[:, 1], mod[:, 2]
        shift_f, scale_f, gate_f = mod[:, 3], mod[:, 4], mod[:, 5]

        f, q, k, v = _inproj(
            x, row2(g_pre_mix[l]), shift_m, scale_m, w_in[l].astype(BF16),
            row2(q_norm_g[l] * (HEAD_DIM ** -0.5)), row2(k_norm_g[l]), cos, sin_signed)
        fo = _fourier(f, cs, ss, cc, w_fourier[l], row2(g_fourier_out[l]))
        ao = _attn(q, k, v, row2(g_attn_out[l]))

        wr = jnp.pad(w_router[l], ((0, 0), (0, ROUTER_PAD - N_EXPERTS)))
        wr_hi = wr.astype(BF16)
        wr_lo = (wr - wr_hi.astype(F32)).astype(BF16)
        br = jnp.pad(b_router[l], (0, ROUTER_PAD - N_EXPERTS)).reshape(1, ROUTER_PAD)
        x1, h2p, logits = _outproj(
            fo, ao, w_out[l].astype(BF16), x, row2(g_post_mix[l]), gate_m, row2(g_pre_ffn[l]),
            shift_f, scale_f, wr_hi, wr_lo, br)

        logits = logits.reshape(n_tok, ROUTER_PAD)[:, :N_EXPERTS]
        gates, dest, fill, items, n_rows, n_items = _route(logits, n_tok)
        xs = _dispatch(dest, *fill, h2p, n_rows)
        rows = _experts(
            *items, xs,
            w_gate[l], b_gate[l], w_up[l], b_up[l], w_down[l], b_down[l], n_items)
        x = _combine(dest, rows, gates.reshape(b, s, TOP_K), x1, row2(g_post_ffn[l]), gate_f)
    return x
```

```python
import functools

import numpy as np
import jax
import jax.numpy as jnp
from jax import lax
from jax.experimental import pallas as pl
from jax.experimental.pallas import tpu as pltpu

F32 = jnp.float32
BF16 = jnp.bfloat16
U32 = jnp.uint32

D_MODEL = 2048
GRID_W = 64
FOURIER_WIDTH = 512
FOURIER_GROUP_DIM = 128
FOURIER_GROUPS = 4
ATTN_WIDTH = 1536
HEAD_DIM = 128
N_Q_HEADS = 12
GQA_GROUP = 3
N_KV_HEADS = 4
KV_WIDTH = 512
IN_WIDTH = 3072
ROPE_THETA = 10000.0
N_EXPERTS = 32
TOP_K = 4
D_FF = 2048
SWIGLU_LIMIT = 7.0
SWIGLU_ALPHA = 1.702
N_MOD = 6
EPS = 1e-6

VMEM_LIMIT_BYTES = 56 * 1024 * 1024
LANES = 128

ADA_TN = 1024
INPROJ_TM = 256
FOURIER_TR = 256
ATTN_TQ = 128
OUTPROJ_TM = 256
ROUTER_PAD = LANES
MOE_TM = 256
MOE_R = 5
MOE_RS = MOE_TM * MOE_R
MOE_TF = 256
MOE_N_FF = D_FF // MOE_TF
MOE_GROUPS = (5, 4, 2, 1)
DISPATCH_TT = 512
COMBINE_TT = 128
PACK_HALF = D_MODEL // 2
PACK_SLABS = PACK_HALF // LANES


def _rms(x, g):
    return x * lax.rsqrt(jnp.mean(x * x, axis=-1, keepdims=True) + EPS) * g


def _ada_kernel(c_ref, w_ref, b_ref, o_ref):
    c = c_ref[...]
    s = (c * jax.nn.sigmoid(c)).astype(BF16)
    o_ref[...] = jnp.dot(s, w_ref[...].astype(BF16), preferred_element_type=F32) + b_ref[...]


def _ada(c_pad, w_ada, b_ada):
    m, d = c_pad.shape
    n = w_ada.shape[1]
    return pl.pallas_call(
        _ada_kernel,
        grid=(n // ADA_TN,),
        in_specs=[
            pl.BlockSpec((m, d), lambda j: (0, 0)),
            pl.BlockSpec((d, ADA_TN), lambda j: (0, j)),
            pl.BlockSpec((1, ADA_TN), lambda j: (0, j)),
        ],
        out_specs=pl.BlockSpec((m, ADA_TN), lambda j: (0, j)),
        out_shape=jax.ShapeDtypeStruct((m, n), F32),
        compiler_params=pltpu.CompilerParams(
            dimension_semantics=("arbitrary",), vmem_limit_bytes=VMEM_LIMIT_BYTES),
        name="ada",
    )(c_pad, w_ada, b_ada)


def _rope(p, cos, sin_signed, even_lane):
    partner = jnp.where(even_lane, pltpu.roll(p, LANES - 1, 1), pltpu.roll(p, 1, 1))
    return p * cos + partner * sin_signed


def _inproj_kernel(x_ref, g_ref, shift_ref, scale_ref, w_ref, qg_ref, kg_ref, cos_ref, sin_ref,
                   f_ref, q_ref, k_ref, v_ref):
    x = x_ref[...]
    h = _rms(x, g_ref[...]) * (1.0 + scale_ref[...]) + shift_ref[...]
    hb = h.astype(BF16)
    cos = cos_ref[...]
    sin = sin_ref[...]
    even_lane = (lax.broadcasted_iota(jnp.int32, cos.shape, 1) % 2) == 0
    chunk = 512
    heads_per_chunk = chunk // HEAD_DIM
    for ci in range(IN_WIDTH // chunk):
        p = jnp.dot(hb, w_ref[:, ci * chunk:(ci + 1) * chunk], preferred_element_type=F32)
        if ci == 0:
            f_ref[...] = p.astype(BF16)
        elif ci == 5:
            v_ref[...] = p.astype(BF16)
        else:
            gain = kg_ref[...] if ci == 4 else qg_ref[...]
            for hh in range(heads_per_chunk):
                ph = p[:, hh * HEAD_DIM:(hh + 1) * HEAD_DIM]
                ph = _rope(_rms(ph, gain), cos, sin, even_lane).astype(BF16)
                if ci == 4:
                    k_ref[:, hh * HEAD_DIM:(hh + 1) * HEAD_DIM] = ph
                else:
                    c0 = (ci - 1) * chunk + hh * HEAD_DIM
                    q_ref[:, c0:c0 + HEAD_DIM] = ph


def _inproj(x, g, shift, scale, w_in_b, qg, kg, cos, sin):
    b, s, d = x.shape
    tm = INPROJ_TM
    row = lambda bi, i: (bi, i, 0)
    per_batch = lambda bi, i: (bi, 0, 0)
    const2 = lambda bi, i: (0, 0)
    return pl.pallas_call(
        _inproj_kernel,
        grid=(b, s // tm),
        in_specs=[
            pl.BlockSpec((None, tm, d), row),
            pl.BlockSpec((1, d), const2),
            pl.BlockSpec((None, 1, d), per_batch),
            pl.BlockSpec((None, 1, d), per_batch),
            pl.BlockSpec((d, IN_WIDTH), const2),
            pl.BlockSpec((1, HEAD_DIM), const2),
            pl.BlockSpec((1, HEAD_DIM), const2),
            pl.BlockSpec((tm, HEAD_DIM), lambda bi, i: (i, 0)),
            pl.BlockSpec((tm, HEAD_DIM), lambda bi, i: (i, 0)),
        ],
        out_specs=[
            pl.BlockSpec((None, tm, FOURIER_WIDTH), row),
            pl.BlockSpec((None, tm, ATTN_WIDTH), row),
            pl.BlockSpec((None, tm, KV_WIDTH), row),
            pl.BlockSpec((None, tm, KV_WIDTH), row),
        ],
        out_shape=[
            jax.ShapeDtypeStruct((b, s, FOURIER_WIDTH), BF16),
            jax.ShapeDtypeStruct((b, s, ATTN_WIDTH), BF16),
            jax.ShapeDtypeStruct((b, s, KV_WIDTH), BF16),
            jax.ShapeDtypeStruct((b, s, KV_WIDTH), BF16),
        ],
        compiler_params=pltpu.CompilerParams(
            dimension_semantics=("arbitrary", "arbitrary"), vmem_limit_bytes=VMEM_LIMIT_BYTES),
        name="inproj",
    )(x, g, shift, scale, w_in_b, qg, kg, cos, sin)


def _fourier_kernel(cs_ref, ss_ref, f_ref, cc_ref, w_ref, g_ref, o_ref):
    f = f_ref[...]
    a = jnp.dot(cs_ref[...], f, preferred_element_type=F32)
    bm = jnp.dot(ss_ref[...], f, preferred_element_type=F32)
    cc = cc_ref[...]
    outs = []
    for gi in range(FOURIER_GROUPS):
        sl = slice(gi * FOURIER_GROUP_DIM, (gi + 1) * FOURIER_GROUP_DIM)
        ab = jnp.concatenate([a[:, sl], bm[:, sl]], axis=1).astype(BF16)
        fm = jnp.dot(ab, cc, preferred_element_type=F32)
        outs.append(jnp.dot(fm.astype(BF16), w_ref[gi].astype(BF16), preferred_element_type=F32))
    fo = jnp.concatenate(outs, axis=1)
    o_ref[...] = _rms(fo, g_ref[...]).astype(BF16)


def _fourier(f, cs, ss, cc, w_fourier, g):
    b, s, fw = f.shape
    tr = FOURIER_TR
    return pl.pallas_call(
        _fourier_kernel,
        grid=(b, s // tr),
        in_specs=[
            pl.BlockSpec((tr, s), lambda bi, i: (i, 0)),
            pl.BlockSpec((tr, s), lambda bi, i: (i, 0)),
            pl.BlockSpec((None, s, fw), lambda bi, i: (bi, 0, 0)),
            pl.BlockSpec((2 * FOURIER_GROUP_DIM, FOURIER_GROUP_DIM), lambda bi, i: (0, 0)),
            pl.BlockSpec((FOURIER_GROUPS, FOURIER_GROUP_DIM, FOURIER_GROUP_DIM), lambda bi, i: (0, 0, 0)),
            pl.BlockSpec((1, fw), lambda bi, i: (0, 0)),
        ],
        out_specs=pl.BlockSpec((None, tr, fw), lambda bi, i: (bi, i, 0)),
        out_shape=jax.ShapeDtypeStruct((b, s, fw), BF16),
        compiler_params=pltpu.CompilerParams(
            dimension_semantics=("arbitrary", "arbitrary"), vmem_limit_bytes=VMEM_LIMIT_BYTES),
        name="fourier",
    )(cs, ss, f, cc, w_fourier, g)


def _attn_kernel(q_ref, k_ref, v_ref, g_ref, o_ref, ao_ref):
    tq = q_ref.shape[0]
    for kv in range(N_KV_HEADS):
        kh = k_ref[:, kv * HEAD_DIM:(kv + 1) * HEAD_DIM]
        vh = v_ref[:, kv * HEAD_DIM:(kv + 1) * HEAD_DIM]
        c0 = kv * GQA_GROUP * HEAD_DIM
        qh = jnp.concatenate(
            [q_ref[:, c0 + gi * HEAD_DIM:c0 + (gi + 1) * HEAD_DIM] for gi in range(GQA_GROUP)], axis=0)
        sc = lax.dot_general(qh, kh, (((1,), (1,)), ((), ())), preferred_element_type=F32)
        m = jnp.max(sc, axis=-1, keepdims=True)
        p = jnp.exp(sc - m)
        l = jnp.sum(p, axis=-1, keepdims=True)
        o = jnp.dot(p.astype(BF16), vh, preferred_element_type=F32) / l
        for gi in range(GQA_GROUP):
            ao_ref[:, c0 + gi * HEAD_DIM:c0 + (gi + 1) * HEAD_DIM] = o[gi * tq:(gi + 1) * tq]
    o_ref[...] = _rms(ao_ref[...], g_ref[...]).astype(BF16)


def _attn(q, k, v, g):
    b, s, _ = q.shape
    tq = ATTN_TQ
    return pl.pallas_call(
        _attn_kernel,
        grid=(b, s // tq),
        in_specs=[
            pl.BlockSpec((None, tq, ATTN_WIDTH), lambda bi, i: (bi, i, 0)),
            pl.BlockSpec((None, s, KV_WIDTH), lambda bi, i: (bi, 0, 0)),
            pl.BlockSpec((None, s, KV_WIDTH), lambda bi, i: (bi, 0, 0)),
            pl.BlockSpec((1, ATTN_WIDTH), lambda bi, i: (0, 0)),
        ],
        out_specs=pl.BlockSpec((None, tq, ATTN_WIDTH), lambda bi, i: (bi, i, 0)),
        out_shape=jax.ShapeDtypeStruct((b, s, ATTN_WIDTH), BF16),
        scratch_shapes=[pltpu.VMEM((tq, ATTN_WIDTH), F32)],
        compiler_params=pltpu.CompilerParams(
            dimension_semantics=("arbitrary", "arbitrary"), vmem_limit_bytes=VMEM_LIMIT_BYTES),
        name="attn",
    )(q, k, v, g)


def _split_bf16(a):
    hi = a.astype(BF16)
    lo = (a - hi.astype(F32)).astype(BF16)
    return hi, lo


def _pack_rows(val, dst_ref):
    n = val.shape[0]
    bits = lax.bitcast_convert_type(val.astype(BF16).astype(F32), U32)
    packed = bits[:, :PACK_HALF] | (bits[:, PACK_HALF:] >> 16)
    for a in range(PACK_SLABS):
        dst_ref[pl.ds(a, n, stride=PACK_SLABS), :] = packed[:, a * LANES:(a + 1) * LANES]


def _unpack_slab(src_ref, a, n):
    word = src_ref[pl.ds(a, n, stride=PACK_SLABS), :]
    hi = lax.bitcast_convert_type(word & jnp.uint32(0xFFFF0000), F32)
    lo = lax.bitcast_convert_type(word << 16, F32)
    return hi, lo


def _outproj_kernel(fo_ref, ao_ref, w_ref, x_ref, gpost_ref, gate_ref, gpre_ref, shift_ref, scale_ref,
                    wr_hi_ref, wr_lo_ref, br_ref, x1_ref, h2p_ref, logit_ref):
    mix = jnp.dot(fo_ref[...], w_ref[:FOURIER_WIDTH, :], preferred_element_type=F32)
    mix = mix + jnp.dot(ao_ref[...], w_ref[FOURIER_WIDTH:, :], preferred_element_type=F32)
    x1 = x_ref[...] + gate_ref[...] * _rms(mix, gpost_ref[...])
    x1_ref[...] = x1
    h2 = _rms(x1, gpre_ref[...]) * (1.0 + scale_ref[...]) + shift_ref[...]
    _pack_rows(h2, h2p_ref)
    h_hi, h_lo = _split_bf16(h2)
    lg = jnp.dot(h_hi, wr_hi_ref[...], preferred_element_type=F32)
    lg = lg + jnp.dot(h_hi, wr_lo_ref[...], preferred_element_type=F32)
    lg = lg + jnp.dot(h_lo, wr_hi_ref[...], preferred_element_type=F32)
    logit_ref[...] = lg + br_ref[...]


def _outproj(fo, ao, w_out_b, x, gpost, gate, gpre, shift, scale, wr_hi, wr_lo, br):
    b, s, d = x.shape
    tm = OUTPROJ_TM
    row = lambda bi, i: (bi, i, 0)
    per_batch = lambda bi, i: (bi, 0, 0)
    const2 = lambda bi, i: (0, 0)
    return pl.pallas_call(
        _outproj_kernel,
        grid=(b, s // tm),
        in_specs=[
            pl.BlockSpec((None, tm, FOURIER_WIDTH), row),
            pl.BlockSpec((None, tm, ATTN_WIDTH), row),
            pl.BlockSpec((d, d), const2),
            pl.BlockSpec((None, tm, d), row),
            pl.BlockSpec((1, d), const2),
            pl.BlockSpec((None, 1, d), per_batch),
            pl.BlockSpec((1, d), const2),
            pl.BlockSpec((None, 1, d), per_batch),
            pl.BlockSpec((None, 1, d), per_batch),
            pl.BlockSpec((d, ROUTER_PAD), const2),
            pl.BlockSpec((d, ROUTER_PAD), const2),
            pl.BlockSpec((1, ROUTER_PAD), const2),
        ],
        out_specs=[
            pl.BlockSpec((None, tm, d), row),
            pl.BlockSpec((tm * PACK_SLABS, LANES), lambda bi, i: (bi * (s // tm) + i, 0)),
            pl.BlockSpec((None, tm, ROUTER_PAD), row),
        ],
        out_shape=[
            jax.ShapeDtypeStruct((b, s, d), F32),
            jax.ShapeDtypeStruct((b * s * PACK_SLABS, LANES), U32),
            jax.ShapeDtypeStruct((b, s, ROUTER_PAD), F32),
        ],
        compiler_params=pltpu.CompilerParams(
            dimension_semantics=("arbitrary", "arbitrary"), vmem_limit_bytes=VMEM_LIMIT_BYTES),
        name="outproj",
    )(fo, ao, w_out_b, x, gpost, gate, gpre, shift, scale, wr_hi, wr_lo, br)


def _tile_rows(row, n_rows=1):
    return pl.ds(pl.multiple_of(row * PACK_SLABS, PACK_SLABS), n_rows * PACK_SLABS)


def _dispatch_kernel(dest_ref, fill_lo_ref, fill_hi_ref, used_tiles_ref, h_ref, xs_hbm, zeros, sem, fill_sem):
    step = pl.program_id(0)
    base = step * (DISPATCH_TT * TOP_K)

    def row_copy(i, kk):
        return pltpu.make_async_copy(
            h_ref.at[_tile_rows(i)], xs_hbm.at[_tile_rows(dest_ref[base + i * TOP_K + kk])], sem)

    def start(i, c):
        for kk in range(TOP_K):
            row_copy(i, kk).start()
        return c

    def wait(i, c):
        for kk in range(TOP_K):
            row_copy(i, kk).wait()
        return c

    lax.fori_loop(0, DISPATCH_TT, start, 0, unroll=4)

    @pl.when(step == 0)
    def _fill():
        zeros[...] = jnp.zeros(zeros.shape, U32)
        n_tiles = xs_hbm.shape[0] // (MOE_TM * PACK_SLABS)

        def pad_chunks(e, fn):
            lo = fill_lo_ref[e]
            length = fill_hi_ref[e] - lo
            for bit in (128, 64, 32, 16, 8, 4, 2, 1):
                @pl.when((length & bit) != 0)
                def _():
                    row = lo + (length & ~(2 * bit - 1))
                    fn(pltpu.make_async_copy(
                        zeros.at[pl.ds(0, bit * PACK_SLABS)], xs_hbm.at[_tile_rows(row, bit)], fill_sem))

        def tail_copy(t):
            return pltpu.make_async_copy(zeros, xs_hbm.at[_tile_rows(t * MOE_TM, MOE_TM)], fill_sem)

        def start_e(e, c):
            pad_chunks(e, lambda cp: cp.start())
            return c

        def wait_e(e, c):
            pad_chunks(e, lambda cp: cp.wait())
            return c

        def start_t(t, c):
            tail_copy(t).start()
            return c

        def wait_t(t, c):
            tail_copy(t).wait()
            return c

        lax.fori_loop(0, N_EXPERTS, start_e, 0)
        lax.fori_loop(used_tiles_ref[0], n_tiles, start_t, 0)
        lax.fori_loop(0, N_EXPERTS, wait_e, 0)
        lax.fori_loop(used_tiles_ref[0], n_tiles, wait_t, 0)

    lax.fori_loop(0, DISPATCH_TT, wait, 0, unroll=4)


def _dispatch(dest, fill_lo, fill_hi, used_tiles, h2p, n_rows):
    n_tok = h2p.shape[0] // PACK_SLABS
    return pl.pallas_call(
        _dispatch_kernel,
        grid_spec=pltpu.PrefetchScalarGridSpec(
            num_scalar_prefetch=4,
            grid=(n_tok // DISPATCH_TT,),
            in_specs=[pl.BlockSpec((DISPATCH_TT * PACK_SLABS, LANES), lambda i, *_: (i, 0))],
            out_specs=pl.BlockSpec(memory_space=pl.ANY),
            scratch_shapes=[
                pltpu.VMEM((MOE_TM * PACK_SLABS, LANES), U32),
                pltpu.SemaphoreType.DMA,
                pltpu.SemaphoreType.DMA,
            ],
        ),
        out_shape=jax.ShapeDtypeStruct((n_rows * PACK_SLABS, LANES), U32),
        compiler_params=pltpu.CompilerParams(
            dimension_semantics=("arbitrary",), vmem_limit_bytes=VMEM_LIMIT_BYTES),
        name="dispatch",
    )(dest, fill_lo, fill_hi, used_tiles, h2p)


def _experts_kernel(item_e_ref, item_row0_ref, item_nt_ref, item_nz_ref,
                    xs_hbm, wg_hbm, wu_hbm, wd_hbm, bg_ref, bu_ref, bd_ref, out_hbm,
                    stage, x_b, acc, wf_g0, wf_u0, wf_d0, wf_g1, wf_u1, wf_d1,
                    wb_g0, wb_u0, wb_d0, wb_g1, wb_u1, wb_d1, x_sems, o_sems, w_sems):
    w = pl.program_id(0)
    nt = item_nt_ref[w]
    nz = item_nz_ref[w]
    e = item_e_ref[w]
    row0 = item_row0_ref[w]
    w_prev = jnp.maximum(w - 1, 0)
    row0_prev, nt_prev = item_row0_ref[w_prev], item_nt_ref[w_prev]
    e_next, row0_next, nt_next = item_e_ref[w + 1], item_row0_ref[w + 1], item_nt_ref[w + 1]
    has_next = nt_next > 0
    slot = w & 1
    wf = ((wf_g0, wf_u0, wf_d0), (wf_g1, wf_u1, wf_d1))
    wb = ((wb_g0, wb_u0, wb_d0), (wb_g1, wb_u1, wb_d1))

    def w_copies(ee, j, ws):
        col = pl.ds(pl.multiple_of(j * MOE_TF, MOE_TF), MOE_TF)
        f_g, f_u, f_d = wf[ws]
        return (pltpu.make_async_copy(wg_hbm.at[ee, :, col], f_g, w_sems.at[ws, 0]),
                pltpu.make_async_copy(wu_hbm.at[ee, :, col], f_u, w_sems.at[ws, 1]),
                pltpu.make_async_copy(wd_hbm.at[ee, col, :], f_d, w_sems.at[ws, 2]))

    def w_start(ee, j, ws):
        for cp in w_copies(ee, j, ws):
            cp.start()

    def w_wait(ee, j, ws):
        for cp in w_copies(ee, j, ws):
            cp.wait()

    def w_cast(ws):
        for f_ref, b_ref in zip(wf[ws], wb[ws]):
            b_ref[...] = f_ref[...].astype(BF16)

    def x_copy(r0, t, ss):
        return pltpu.make_async_copy(
            xs_hbm.at[_tile_rows(r0 + t * MOE_TM, MOE_TM)], stage.at[ss, t], x_sems.at[ss, t])

    def out_copy(r0, t, ss):
        return pltpu.make_async_copy(
            stage.at[ss, t], out_hbm.at[_tile_rows(r0 + t * MOE_TM, MOE_TM)], o_sems.at[ss, t])

    def for_tiles(n, fn):
        for t in range(MOE_R):
            @pl.when(t < n)
            def _():
                fn(t)

    def gate_up(t, j, ws):
        b_g, b_u, _ = wb[ws]
        x = x_b[t]
        g = jnp.dot(x, b_g[...], preferred_element_type=F32) + bg_ref[j]
        u = jnp.dot(x, b_u[...], preferred_element_type=F32) + bu_ref[j]
        g = jnp.minimum(g, SWIGLU_LIMIT)
        u = jnp.clip(u, -SWIGLU_LIMIT, SWIGLU_LIMIT)
        return ((u + 1.0) * (g * jax.nn.sigmoid(SWIGLU_ALPHA * g))).astype(BF16)

    def down(t, hidden, ws):
        acc[t] += jnp.dot(hidden, wb[ws][2][...], preferred_element_type=F32)

    def ff_step(j, ws):
        @pl.when(j + 2 < MOE_N_FF)
        def _():
            w_start(e, j + 2, ws)

        @pl.when(jnp.logical_and(j + 2 >= MOE_N_FF, has_next))
        def _():
            w_start(e_next, j + 2 - MOE_N_FF, ws)

        @pl.when(j + 1 < MOE_N_FF)
        def _():
            w_wait(e, j + 1, 1 - ws)

        @pl.when(jnp.logical_and(j + 1 >= MOE_N_FF, has_next))
        def _():
            w_wait(e_next, 0, 1 - ws)

        w_cast(1 - ws)

        done = 0
        for size in MOE_GROUPS:
            def group(i, c, size=size, done=done):
                ts = [done + size * i + k for k in range(size)]
                hs = [gate_up(t, j, ws) for t in ts]
                for t, h in zip(ts, hs):
                    down(t, h, ws)
                return c

            n_group = (nt - done) // size
            lax.fori_loop(0, n_group, group, 0)
            done = done + n_group * size

    @pl.when(nt > 0)
    def _active():
        @pl.when(w == 0)
        def _prologue():
            w_start(e, 0, 0)
            w_start(e, 1, 1)
            for_tiles(nt, lambda t: x_copy(row0, t, 0).start())
            w_wait(e, 0, 0)
            w_cast(0)

        def load_tile(t):
            x_copy(row0, t, slot).wait()
            for a in range(PACK_SLABS):
                hi, lo = _unpack_slab(stage.at[slot, t], a, MOE_TM)
                x_b[t, :, a * LANES:(a + 1) * LANES] = hi.astype(BF16)
                x_b[t, :, PACK_HALF + a * LANES:PACK_HALF + (a + 1) * LANES] = lo.astype(BF16)
            acc[t] = jnp.broadcast_to(bd_ref[...], acc.shape[1:])

        for_tiles(nt, load_tile)

        def ff_pair(jj, c):
            ff_step(2 * jj, 0)

            @pl.when(jj == 0)
            def _():
                @pl.when(w > 0)
                def _():
                    for_tiles(nt_prev, lambda t: out_copy(row0_prev, t, 1 - slot).wait())

                @pl.when(has_next)
                def _():
                    for_tiles(nt_next, lambda t: x_copy(row0_next, t, 1 - slot).start())

            ff_step(2 * jj + 1, 1)
            return c

        lax.fori_loop(0, MOE_N_FF // 2, ff_pair, 0)

        def store_tile(t):
            _pack_rows(acc[t], stage.at[slot, t])
            out_copy(row0, t, slot).start()

        for_tiles(nt, store_tile)

        @pl.when(jnp.logical_not(has_next))
        def _():
            for_tiles(nt, lambda t: out_copy(row0, t, slot).wait())

    @pl.when(nz > 0)
    def _zero_tail():
        def zero_tile(t):
            stage[0, t] = jnp.zeros(stage.shape[2:], U32)
            out_copy(row0, t, 0).start()

        for_tiles(nz, zero_tile)
        for_tiles(nz, lambda t: out_copy(row0, t, 0).wait())


def _experts(item_e, item_row0, item_nt, item_nz, xs, w_gate, b_gate, w_up, b_up, w_down, b_down, n_items):
    d = D_MODEL
    w_shapes = [(d, MOE_TF), (d, MOE_TF), (MOE_TF, d)]
    by_expert = lambda w, ie, r0, nt, nz: (ie[w], 0, 0, 0)
    return pl.pallas_call(
        _experts_kernel,
        grid_spec=pltpu.PrefetchScalarGridSpec(
            num_scalar_prefetch=4,
            grid=(n_items,),
            in_specs=[
                pl.BlockSpec(memory_space=pl.ANY),
                pl.BlockSpec(memory_space=pl.ANY),
                pl.BlockSpec(memory_space=pl.ANY),
                pl.BlockSpec(memory_space=pl.ANY),
                pl.BlockSpec((None, MOE_N_FF, 1, MOE_TF), by_expert),
                pl.BlockSpec((None, MOE_N_FF, 1, MOE_TF), by_expert),
                pl.BlockSpec((None, 1, d), lambda w, ie, r0, nt, nz: (ie[w], 0, 0)),
            ],
            out_specs=pl.BlockSpec(memory_space=pl.ANY),
            scratch_shapes=[
                pltpu.VMEM((2, MOE_R, MOE_TM * PACK_SLABS, LANES), U32),
                pltpu.VMEM((MOE_R, MOE_TM, d), BF16),
                pltpu.VMEM((MOE_R, MOE_TM, d), F32),
                *[pltpu.VMEM(shape, F32) for shape in w_shapes * 2],
                *[pltpu.VMEM(shape, BF16) for shape in w_shapes * 2],
                pltpu.SemaphoreType.DMA((2, MOE_R)),
                pltpu.SemaphoreType.DMA((2, MOE_R)),
                pltpu.SemaphoreType.DMA((2, 3)),
            ],
        ),
        out_shape=jax.ShapeDtypeStruct(xs.shape, U32),
        compiler_params=pltpu.CompilerParams(
            dimension_semantics=("arbitrary",), vmem_limit_bytes=VMEM_LIMIT_BYTES),
        name="experts",
    )(item_e, item_row0, item_nt, item_nz, xs, w_gate, w_up, w_down,
      b_gate.reshape(N_EXPERTS, MOE_N_FF, 1, MOE_TF), b_up.reshape(N_EXPERTS, MOE_N_FF, 1, MOE_TF),
      b_down.reshape(N_EXPERTS, 1, d))


def _combine_kernel(dest_ref, rows_hbm, gates_ref, x1_ref, gpost_ref, gate_ref, o_ref, buf, sem):
    tt = COMBINE_TT
    base = (pl.program_id(0) * pl.num_programs(1) + pl.program_id(1)) * tt * TOP_K

    def row_copy(i, kk):
        return pltpu.make_async_copy(
            rows_hbm.at[_tile_rows(dest_ref[base + i * TOP_K + kk])], buf.at[kk, _tile_rows(i)], sem)

    def start(i, c):
        for kk in range(TOP_K):
            row_copy(i, kk).start()
        return c

    def wait(i, c):
        for kk in range(TOP_K):
            row_copy(i, kk).wait()
        return c

    lax.fori_loop(0, tt, start, 0, unroll=4)
    lax.fori_loop(0, tt, wait, 0, unroll=4)
    gates = gates_ref[...]
    y_hi, y_lo = [], []
    for a in range(PACK_SLABS):
        acc_hi = acc_lo = None
        for kk in range(TOP_K):
            hi, lo = _unpack_slab(buf.at[kk], a, tt)
            gk = gates[:, kk:kk + 1]
            acc_hi = gk * hi if acc_hi is None else acc_hi + gk * hi
            acc_lo = gk * lo if acc_lo is None else acc_lo + gk * lo
        y_hi.append(acc_hi)
        y_lo.append(acc_lo)
    y = jnp.concatenate(y_hi + y_lo, axis=1)
    o_ref[...] = x1_ref[...] + gate_ref[...] * _rms(y, gpost_ref[...])


def _combine(dest, rows, gates, x1, gpost, gate):
    b, s, d = x1.shape
    tt = COMBINE_TT
    row = lambda bi, i, dst: (bi, i, 0)
    return pl.pallas_call(
        _combine_kernel,
        grid_spec=pltpu.PrefetchScalarGridSpec(
            num_scalar_prefetch=1,
            grid=(b, s // tt),
            in_specs=[
                pl.BlockSpec(memory_space=pl.ANY),
                pl.BlockSpec((None, tt, TOP_K), row),
                pl.BlockSpec((None, tt, d), row),
                pl.BlockSpec((1, d), lambda bi, i, dst: (0, 0)),
                pl.BlockSpec((None, 1, d), lambda bi, i, dst: (bi, 0, 0)),
            ],
            out_specs=pl.BlockSpec((None, tt, d), row),
            scratch_shapes=[pltpu.VMEM((TOP_K, tt * PACK_SLABS, LANES), U32), pltpu.SemaphoreType.DMA],
        ),
        out_shape=jax.ShapeDtypeStruct((b, s, d), F32),
        compiler_params=pltpu.CompilerParams(
            dimension_semantics=("arbitrary", "arbitrary"), vmem_limit_bytes=VMEM_LIMIT_BYTES),
        name="combine",
    )(dest, rows, gates, x1, gpost, gate)


def _rope_tables(seq_len):
    rows = seq_len // GRID_W
    row_idx = jnp.repeat(jnp.arange(rows, dtype=F32), GRID_W)
    col_idx = jnp.tile(jnp.arange(GRID_W, dtype=F32), rows)
    half = HEAD_DIM // 2
    inv_freq = 1.0 / (ROPE_THETA ** (jnp.arange(0, half, 2, dtype=F32) / half))
    ang = jnp.concatenate([row_idx[:, None] * inv_freq, col_idx[:, None] * inv_freq], axis=-1)
    cos = jnp.repeat(jnp.cos(ang), 2, axis=-1)
    sin = jnp.sin(ang)
    sin_signed = jnp.stack([-sin, sin], axis=-1).reshape(seq_len, HEAD_DIM)
    return cos, sin_signed


def _dft_tables(n):
    idx = np.arange(n, dtype=np.int64)
    ang = 2.0 * np.pi * ((idx[:, None] * idx[None, :]) % n).astype(np.float64) / n
    scale = 1.0 / np.sqrt(n)
    return np.cos(ang) * scale, np.sin(ang) * scale


def _route(logits, n_tok):
    top_val, top_idx = lax.top_k(logits, TOP_K)
    gates = jax.nn.softmax(top_val, axis=-1)
    flat_e = top_idx.reshape(-1).astype(jnp.int32)
    onehot = (flat_e[:, None] == jnp.arange(N_EXPERTS, dtype=jnp.int32)[None, :]).astype(jnp.int32)
    csum = jnp.cumsum(onehot, axis=0)
    rank = jnp.sum(csum * onehot, axis=1) - 1
    counts = csum[-1]
    tiles = (counts + MOE_TM - 1) // MOE_TM
    padded = tiles * MOE_TM
    pad_end = jnp.cumsum(padded)
    pad_start = pad_end - padded
    dest = pad_start[flat_e] + rank
    n_rows = n_tok * TOP_K + N_EXPERTS * MOE_TM
    fill_lo = (pad_start + counts).astype(jnp.int32)
    fill_hi = pad_end.astype(jnp.int32)
    used_tiles = jnp.sum(tiles).astype(jnp.int32).reshape(1)
    n_items = N_EXPERTS + n_rows // MOE_RS
    items_per_e = (tiles + MOE_R - 1) // MOE_R
    item_end = jnp.cumsum(items_per_e)
    item_start = item_end - items_per_e
    w = jnp.arange(n_items, dtype=jnp.int32)
    total = item_end[-1]
    e_of_w = jnp.minimum(jnp.searchsorted(item_end, w, side='right'), N_EXPERTS - 1).astype(jnp.int32)
    local = w - item_start[e_of_w]
    active = w < total
    last_e = e_of_w[jnp.maximum(total - 1, 0)]
    item_e = jnp.where(active, e_of_w, last_e).astype(jnp.int32)
    item_nt = jnp.where(active, jnp.minimum(MOE_R, tiles[e_of_w] - local * MOE_R), 0).astype(jnp.int32)
    n_tiles = n_rows // MOE_TM
    fill_tile0 = jnp.sum(tiles) + (w - total) * MOE_R
    item_nz = jnp.where(active, 0, jnp.clip(n_tiles - fill_tile0, 0, MOE_R)).astype(jnp.int32)
    fill_row0 = jnp.minimum(fill_tile0, n_tiles - 1) * MOE_TM
    item_row0 = jnp.where(active, pad_start[e_of_w] + local * MOE_RS, fill_row0).astype(jnp.int32)
    fill = (fill_lo, fill_hi, used_tiles)
    items = tuple(jnp.pad(a, (0, 1)) for a in (item_e, item_row0, item_nt, item_nz))
    return gates, dest.astype(jnp.int32), fill, items, n_rows, n_items


def kernel(x, c, w_ada, b_ada, g_pre_mix, w_in, w_fourier, q_norm_g, k_norm_g, g_fourier_out, g_attn_out,
           w_out, g_post_mix, g_pre_ffn, w_router, b_router, w_gate, b_gate, w_up, b_up, w_down, b_down,
           g_post_ffn):
    b, s, d = x.shape
    n_tok = b * s
    depth = w_ada.shape[0]
    cos, sin_signed = _rope_tables(s)
    cs_np, ss_np = _dft_tables(s)
    cs = jnp.asarray(cs_np, dtype=BF16)
    ss = jnp.asarray(ss_np, dtype=BF16)
    cc_np, sc_np = _dft_tables(FOURIER_GROUP_DIM)
    cc = jnp.asarray(np.concatenate([cc_np, -sc_np], axis=0), dtype=BF16)
    c_pad = jnp.pad(c, ((0, 8 - b), (0, 0)))
    row2 = lambda a: a.reshape(1, -1)

    for l in range(depth):
        mod = _ada(c_pad, w_ada[l], row2(b_ada[l]))[:b].reshape(b, N_MOD, 1, d)
        shift_m, scale_m, gate_m = mod[:, 0], mod[:, 1], mod[:, 2]
        shift_f, scale_f, gate_f = mod[:, 3], mod[:, 4], mod[:, 5]

        f, q, k, v = _inproj(
            x, row2(g_pre_mix[l]), shift_m, scale_m, w_in[l].astype(BF16),
            row2(q_norm_g[l] * (HEAD_DIM ** -0.5)), row2(k_norm_g[l]), cos, sin_signed)
        fo = _fourier(f, cs, ss, cc, w_fourier[l], row2(g_fourier_out[l]))
        ao = _attn(q, k, v, row2(g_attn_out[l]))

        wr = jnp.pad(w_router[l], ((0, 0), (0, ROUTER_PAD - N_EXPERTS)))
        wr_hi = wr.astype(BF16)
        wr_lo = (wr - wr_hi.astype(F32)).astype(BF16)
        br = jnp.pad(b_router[l], (0, ROUTER_PAD - N_EXPERTS)).reshape(1, ROUTER_PAD)
        x1, h2p, logits = _outproj(
            fo, ao, w_out[l].astype(BF16), x, row2(g_post_mix[l]), gate_m, row2(g_pre_ffn[l]),
            shift_f, scale_f, wr_hi, wr_lo, br)

        logits = logits.reshape(n_tok, ROUTER_PAD)[:, :N_EXPERTS]
        gates, dest, fill, items, n_rows, n_items = _route(logits, n_tok)
        xs = _dispatch(dest, *fill, h2p, n_rows)
        rows = _experts(
            *items, xs,
            w_gate[l], b_gate[l], w_up[l], b_up[l], w_down[l], b_down[l], n_items)
        x = _combine(dest, rows, gates.reshape(b, s, TOP_K), x1, row2(g_post_ffn[l]), gate_f)
    return x
```

```python
import functools

import numpy as np
import jax
import jax.numpy as jnp
from jax import lax
from jax.experimental import pallas as pl
from jax.experimental.pallas import tpu as pltpu

F32 = jnp.float32
BF16 = jnp.bfloat16
U32 = jnp.uint32

D_MODEL = 2048
GRID_W = 64
FOURIER_WIDTH = 512
FOURIER_GROUP_DIM = 128
FOURIER_GROUPS = 4
ATTN_WIDTH = 1536
HEAD_DIM = 128
N_Q_HEADS = 12
GQA_GROUP = 3
N_KV_HEADS = 4
KV_WIDTH = 512
IN_WIDTH = 3072
ROPE_THETA = 10000.0
N_EXPERTS = 32
TOP_K = 4
D_FF = 2048
SWIGLU_LIMIT = 7.0
SWIGLU_ALPHA = 1.702
N_MOD = 6
EPS = 1e-6

VMEM_LIMIT_BYTES = 56 * 1024 * 1024
LANES = 128

ADA_TN = 1024
INPROJ_TM = 256
FOURIER_TR = 256
ATTN_TQ = 128
OUTPROJ_TM = 256
ROUTER_PAD = LANES
ROUTE_TB = 512
MOE_TM = 256
MOE_R = 5
MOE_RS = MOE_TM * MOE_R
MOE_TF = 256
MOE_N_FF = D_FF // MOE_TF
MOE_GROUPS = (5, 4, 2, 1)
DISPATCH_TT = 512
COMBINE_TT = 128
PACK_HALF = D_MODEL // 2
PACK_SLABS = PACK_HALF // LANES


def _rms(x, g):
    return x * lax.rsqrt(jnp.mean(x * x, axis=-1, keepdims=True) + EPS) * g


def _ada_kernel(c_ref, w_ref, b_ref, o_ref):
    c = c_ref[...]
    s = (c * jax.nn.sigmoid(c)).astype(BF16)
    o_ref[...] = jnp.dot(s, w_ref[...].astype(BF16), preferred_element_type=F32) + b_ref[...]


def _ada(c_pad, w_ada, b_ada):
    m, d = c_pad.shape
    n = w_ada.shape[1]
    return pl.pallas_call(
        _ada_kernel,
        grid=(n // ADA_TN,),
        in_specs=[
            pl.BlockSpec((m, d), lambda j: (0, 0)),
            pl.BlockSpec((d, ADA_TN), lambda j: (0, j)),
            pl.BlockSpec((1, ADA_TN), lambda j: (0, j)),
        ],
        out_specs=pl.BlockSpec((m, ADA_TN), lambda j: (0, j)),
        out_shape=jax.ShapeDtypeStruct((m, n), F32),
        compiler_params=pltpu.CompilerParams(
            dimension_semantics=("arbitrary",), vmem_limit_bytes=VMEM_LIMIT_BYTES),
        name="ada",
    )(c_pad, w_ada, b_ada)


def _rope(p, cos, sin_signed, even_lane):
    partner = jnp.where(even_lane, pltpu.roll(p, LANES - 1, 1), pltpu.roll(p, 1, 1))
    return p * cos + partner * sin_signed


def _inproj_kernel(x_ref, g_ref, shift_ref, scale_ref, w_ref, qg_ref, kg_ref, cos_ref, sin_ref,
                   f_ref, q_ref, k_ref, v_ref):
    x = x_ref[...]
    h = _rms(x, g_ref[...]) * (1.0 + scale_ref[...]) + shift_ref[...]
    hb = h.astype(BF16)
    cos = cos_ref[...]
    sin = sin_ref[...]
    even_lane = (lax.broadcasted_iota(jnp.int32, cos.shape, 1) % 2) == 0
    chunk = 512
    heads_per_chunk = chunk // HEAD_DIM
    for ci in range(IN_WIDTH // chunk):
        p = jnp.dot(hb, w_ref[:, ci * chunk:(ci + 1) * chunk], preferred_element_type=F32)
        if ci == 0:
            f_ref[...] = p.astype(BF16)
        elif ci == 5:
            v_ref[...] = p.astype(BF16)
        else:
            gain = kg_ref[...] if ci == 4 else qg_ref[...]
            for hh in range(heads_per_chunk):
                ph = p[:, hh * HEAD_DIM:(hh + 1) * HEAD_DIM]
                ph = _rope(_rms(ph, gain), cos, sin, even_lane).astype(BF16)
                if ci == 4:
                    k_ref[:, hh * HEAD_DIM:(hh + 1) * HEAD_DIM] = ph
                else:
                    c0 = (ci - 1) * chunk + hh * HEAD_DIM
                    q_ref[:, c0:c0 + HEAD_DIM] = ph


def _inproj(x, g, shift, scale, w_in_b, qg, kg, cos, sin):
    b, s, d = x.shape
    tm = INPROJ_TM
    row = lambda bi, i: (bi, i, 0)
    per_batch = lambda bi, i: (bi, 0, 0)
    const2 = lambda bi, i: (0, 0)
    return pl.pallas_call(
        _inproj_kernel,
        grid=(b, s // tm),
        in_specs=[
            pl.BlockSpec((None, tm, d), row),
            pl.BlockSpec((1, d), const2),
            pl.BlockSpec((None, 1, d), per_batch),
            pl.BlockSpec((None, 1, d), per_batch),
            pl.BlockSpec((d, IN_WIDTH), const2),
            pl.BlockSpec((1, HEAD_DIM), const2),
            pl.BlockSpec((1, HEAD_DIM), const2),
            pl.BlockSpec((tm, HEAD_DIM), lambda bi, i: (i, 0)),
            pl.BlockSpec((tm, HEAD_DIM), lambda bi, i: (i, 0)),
        ],
        out_specs=[
            pl.BlockSpec((None, tm, FOURIER_WIDTH), row),
            pl.BlockSpec((None, tm, ATTN_WIDTH), row),
            pl.BlockSpec((None, tm, KV_WIDTH), row),
            pl.BlockSpec((None, tm, KV_WIDTH), row),
        ],
        out_shape=[
            jax.ShapeDtypeStruct((b, s, FOURIER_WIDTH), BF16),
            jax.ShapeDtypeStruct((b, s, ATTN_WIDTH), BF16),
            jax.ShapeDtypeStruct((b, s, KV_WIDTH), BF16),
            jax.ShapeDtypeStruct((b, s, KV_WIDTH), BF16),
        ],
        compiler_params=pltpu.CompilerParams(
            dimension_semantics=("arbitrary", "arbitrary"), vmem_limit_bytes=VMEM_LIMIT_BYTES),
        name="inproj",
    )(x, g, shift, scale, w_in_b, qg, kg, cos, sin)


def _fourier_kernel(cs_ref, ss_ref, f_ref, cc_ref, w_ref, g_ref, o_ref):
    f = f_ref[...]
    a = jnp.dot(cs_ref[...], f, preferred_element_type=F32)
    bm = jnp.dot(ss_ref[...], f, preferred_element_type=F32)
    cc = cc_ref[...]
    outs = []
    for gi in range(FOURIER_GROUPS):
        sl = slice(gi * FOURIER_GROUP_DIM, (gi + 1) * FOURIER_GROUP_DIM)
        ab = jnp.concatenate([a[:, sl], bm[:, sl]], axis=1).astype(BF16)
        fm = jnp.dot(ab, cc, preferred_element_type=F32)
        outs.append(jnp.dot(fm.astype(BF16), w_ref[gi].astype(BF16), preferred_element_type=F32))
    fo = jnp.concatenate(outs, axis=1)
    o_ref[...] = _rms(fo, g_ref[...]).astype(BF16)


def _fourier(f, cs, ss, cc, w_fourier, g):
    b, s, fw = f.shape
    tr = FOURIER_TR
    return pl.pallas_call(
        _fourier_kernel,
        grid=(b, s // tr),
        in_specs=[
            pl.BlockSpec((tr, s), lambda bi, i: (i, 0)),
            pl.BlockSpec((tr, s), lambda bi, i: (i, 0)),
            pl.BlockSpec((None, s, fw), lambda bi, i: (bi, 0, 0)),
            pl.BlockSpec((2 * FOURIER_GROUP_DIM, FOURIER_GROUP_DIM), lambda bi, i: (0, 0)),
            pl.BlockSpec((FOURIER_GROUPS, FOURIER_GROUP_DIM, FOURIER_GROUP_DIM), lambda bi, i: (0, 0, 0)),
            pl.BlockSpec((1, fw), lambda bi, i: (0, 0)),
        ],
        out_specs=pl.BlockSpec((None, tr, fw), lambda bi, i: (bi, i, 0)),
        out_shape=jax.ShapeDtypeStruct((b, s, fw), BF16),
        compiler_params=pltpu.CompilerParams(
            dimension_semantics=("arbitrary", "arbitrary"), vmem_limit_bytes=VMEM_LIMIT_BYTES),
        name="fourier",
    )(cs, ss, f, cc, w_fourier, g)


def _attn_kernel(q_ref, k_ref, v_ref, g_ref, o_ref, ao_ref):
    tq = q_ref.shape[0]
    for kv in range(N_KV_HEADS):
        kh = k_ref[:, kv * HEAD_DIM:(kv + 1) * HEAD_DIM]
        vh = v_ref[:, kv * HEAD_DIM:(kv + 1) * HEAD_DIM]
        c0 = kv * GQA_GROUP * HEAD_DIM
        qh = jnp.concatenate(
            [q_ref[:, c0 + gi * HEAD_DIM:c0 + (gi + 1) * HEAD_DIM] for gi in range(GQA_GROUP)], axis=0)
        sc = lax.dot_general(qh, kh, (((1,), (1,)), ((), ())), preferred_element_type=F32)
        m = jnp.max(sc, axis=-1, keepdims=True)
        p = jnp.exp(sc - m)
        l = jnp.sum(p, axis=-1, keepdims=True)
        o = jnp.dot(p.astype(BF16), vh, preferred_element_type=F32) / l
        for gi in range(GQA_GROUP):
            ao_ref[:, c0 + gi * HEAD_DIM:c0 + (gi + 1) * HEAD_DIM] = o[gi * tq:(gi + 1) * tq]
    o_ref[...] = _rms(ao_ref[...], g_ref[...]).astype(BF16)


def _attn(q, k, v, g):
    b, s, _ = q.shape
    tq = ATTN_TQ
    return pl.pallas_call(
        _attn_kernel,
        grid=(b, s // tq),
        in_specs=[
            pl.BlockSpec((None, tq, ATTN_WIDTH), lambda bi, i: (bi, i, 0)),
            pl.BlockSpec((None, s, KV_WIDTH), lambda bi, i: (bi, 0, 0)),
            pl.BlockSpec((None, s, KV_WIDTH), lambda bi, i: (bi, 0, 0)),
            pl.BlockSpec((1, ATTN_WIDTH), lambda bi, i: (0, 0)),
        ],
        out_specs=pl.BlockSpec((None, tq, ATTN_WIDTH), lambda bi, i: (bi, i, 0)),
        out_shape=jax.ShapeDtypeStruct((b, s, ATTN_WIDTH), BF16),
        scratch_shapes=[pltpu.VMEM((tq, ATTN_WIDTH), F32)],
        compiler_params=pltpu.CompilerParams(
            dimension_semantics=("arbitrary", "arbitrary"), vmem_limit_bytes=VMEM_LIMIT_BYTES),
        name="attn",
    )(q, k, v, g)


def _split_bf16(a):
    hi = a.astype(BF16)
    lo = (a - hi.astype(F32)).astype(BF16)
    return hi, lo


def _pack_rows(val, dst_ref):
    n = val.shape[0]
    bits = lax.bitcast_convert_type(val.astype(BF16).astype(F32), U32)
    packed = bits[:, :PACK_HALF] | (bits[:, PACK_HALF:] >> 16)
    for a in range(PACK_SLABS):
        dst_ref[pl.ds(a, n, stride=PACK_SLABS), :] = packed[:, a * LANES:(a + 1) * LANES]


def _unpack_slab(src_ref, a, n):
    word = src_ref[pl.ds(a, n, stride=PACK_SLABS), :]
    hi = lax.bitcast_convert_type(word & jnp.uint32(0xFFFF0000), F32)
    lo = lax.bitcast_convert_type(word << 16, F32)
    return hi, lo


def _outproj_kernel(fo_ref, ao_ref, w_ref, x_ref, gpost_ref, gate_ref, gpre_ref, shift_ref, scale_ref,
                    wr_hi_ref, wr_lo_ref, br_ref, x1_ref, h2p_ref, logit_ref):
    mix = jnp.dot(fo_ref[...], w_ref[:FOURIER_WIDTH, :], preferred_element_type=F32)
    mix = mix + jnp.dot(ao_ref[...], w_ref[FOURIER_WIDTH:, :], preferred_element_type=F32)
    x1 = x_ref[...] + gate_ref[...] * _rms(mix, gpost_ref[...])
    x1_ref[...] = x1
    h2 = _rms(x1, gpre_ref[...]) * (1.0 + scale_ref[...]) + shift_ref[...]
    _pack_rows(h2, h2p_ref)
    h_hi, h_lo = _split_bf16(h2)
    lg = jnp.dot(h_hi, wr_hi_ref[...], preferred_element_type=F32)
    lg = lg + jnp.dot(h_hi, wr_lo_ref[...], preferred_element_type=F32)
    lg = lg + jnp.dot(h_lo, wr_hi_ref[...], preferred_element_type=F32)
    logit_ref[...] = lg + br_ref[...]


def _outproj(fo, ao, w_out_b, x, gpost, gate, gpre, shift, scale, wr_hi, wr_lo, br):
    b, s, d = x.shape
    tm = OUTPROJ_TM
    row = lambda bi, i: (bi, i, 0)
    per_batch = lambda bi, i: (bi, 0, 0)
    const2 = lambda bi, i: (0, 0)
    return pl.pallas_call(
        _outproj_kernel,
        grid=(b, s // tm),
        in_specs=[
            pl.BlockSpec((None, tm, FOURIER_WIDTH), row),
            pl.BlockSpec((None, tm, ATTN_WIDTH), row),
            pl.BlockSpec((d, d), const2),
            pl.BlockSpec((None, tm, d), row),
            pl.BlockSpec((1, d), const2),
            pl.BlockSpec((None, 1, d), per_batch),
            pl.BlockSpec((1, d), const2),
            pl.BlockSpec((None, 1, d), per_batch),
            pl.BlockSpec((None, 1, d), per_batch),
            pl.BlockSpec((d, ROUTER_PAD), const2),
            pl.BlockSpec((d, ROUTER_PAD), const2),
            pl.BlockSpec((1, ROUTER_PAD), const2),
        ],
        out_specs=[
            pl.BlockSpec((None, tm, d), row),
            pl.BlockSpec((tm * PACK_SLABS, LANES), lambda bi, i: (bi * (s // tm) + i, 0)),
            pl.BlockSpec((None, tm, ROUTER_PAD), row),
        ],
        out_shape=[
            jax.ShapeDtypeStruct((b, s, d), F32),
            jax.ShapeDtypeStruct((b * s * PACK_SLABS, LANES), U32),
            jax.ShapeDtypeStruct((b, s, ROUTER_PAD), F32),
        ],
        compiler_params=pltpu.CompilerParams(
            dimension_semantics=("arbitrary", "arbitrary"), vmem_limit_bytes=VMEM_LIMIT_BYTES),
        name="outproj",
    )(fo, ao, w_out_b, x, gpost, gate, gpre, shift, scale, wr_hi, wr_lo, br)


def _tile_rows(row, n_rows=1):
    return pl.ds(pl.multiple_of(row * PACK_SLABS, PACK_SLABS), n_rows * PACK_SLABS)


def _dispatch_kernel(dest_ref, fill_lo_ref, fill_hi_ref, used_tiles_ref, h_ref, xs_hbm, zeros, sem, fill_sem):
    step = pl.program_id(0)
    base = step * (DISPATCH_TT * TOP_K)

    def row_copy(i, kk):
        return pltpu.make_async_copy(
            h_ref.at[_tile_rows(i)], xs_hbm.at[_tile_rows(dest_ref[base + i * TOP_K + kk])], sem)

    def start(i, c):
        for kk in range(TOP_K):
            row_copy(i, kk).start(priority=kk % 2)
        return c

    def wait(i, c):
        for kk in range(TOP_K):
            row_copy(i, kk).wait()
        return c

    lax.fori_loop(0, DISPATCH_TT, start, 0, unroll=4)

    @pl.when(step == 0)
    def _fill():
        zeros[...] = jnp.zeros(zeros.shape, U32)
        n_tiles = xs_hbm.shape[0] // (MOE_TM * PACK_SLABS)

        def pad_chunks(e, fn):
            lo = fill_lo_ref[e]
            length = fill_hi_ref[e] - lo
            for bit in (128, 64, 32, 16, 8, 4, 2, 1):
                @pl.when((length & bit) != 0)
                def _():
                    row = lo + (length & ~(2 * bit - 1))
                    fn(pltpu.make_async_copy(
                        zeros.at[pl.ds(0, bit * PACK_SLABS)], xs_hbm.at[_tile_rows(row, bit)], fill_sem))

        def tail_copy(t):
            return pltpu.make_async_copy(zeros, xs_hbm.at[_tile_rows(t * MOE_TM, MOE_TM)], fill_sem)

        def start_e(e, c):
            pad_chunks(e, lambda cp: cp.start())
            return c

        def wait_e(e, c):
            pad_chunks(e, lambda cp: cp.wait())
            return c

        def start_t(t, c):
            tail_copy(t).start()
            return c

        def wait_t(t, c):
            tail_copy(t).wait()
            return c

        lax.fori_loop(0, N_EXPERTS, start_e, 0)
        lax.fori_loop(used_tiles_ref[0], n_tiles, start_t, 0)
        lax.fori_loop(0, N_EXPERTS, wait_e, 0)
        lax.fori_loop(used_tiles_ref[0], n_tiles, wait_t, 0)

    lax.fori_loop(0, DISPATCH_TT, wait, 0, unroll=4)


def _dispatch(dest, fill_lo, fill_hi, used_tiles, h2p, n_rows):
    n_tok = h2p.shape[0] // PACK_SLABS
    return pl.pallas_call(
        _dispatch_kernel,
        grid_spec=pltpu.PrefetchScalarGridSpec(
            num_scalar_prefetch=4,
            grid=(n_tok // DISPATCH_TT,),
            in_specs=[pl.BlockSpec((DISPATCH_TT * PACK_SLABS, LANES), lambda i, *_: (i, 0))],
            out_specs=pl.BlockSpec(memory_space=pl.ANY),
            scratch_shapes=[
                pltpu.VMEM((MOE_TM * PACK_SLABS, LANES), U32),
                pltpu.SemaphoreType.DMA,
                pltpu.SemaphoreType.DMA,
            ],
        ),
        out_shape=jax.ShapeDtypeStruct((n_rows * PACK_SLABS, LANES), U32),
        compiler_params=pltpu.CompilerParams(
            dimension_semantics=("arbitrary",), vmem_limit_bytes=VMEM_LIMIT_BYTES),
        name="dispatch",
    )(dest, fill_lo, fill_hi, used_tiles, h2p)


def _experts_kernel(item_e_ref, item_row0_ref, item_nt_ref, item_nz_ref,
                    xs_hbm, wg_hbm, wu_hbm, wd_hbm, bg_ref, bu_ref, bd_ref, out_hbm,
                    stage, x_b, acc, wf_g0, wf_u0, wf_d0, wf_g1, wf_u1, wf_d1,
                    wb_g0, wb_u0, wb_d0, wb_g1, wb_u1, wb_d1, x_sems, o_sems, w_sems):
    w = pl.program_id(0)
    nt = item_nt_ref[w]
    nz = item_nz_ref[w]
    e = item_e_ref[w]
    row0 = item_row0_ref[w]
    w_prev = jnp.maximum(w - 1, 0)
    row0_prev, nt_prev = item_row0_ref[w_prev], item_nt_ref[w_prev]
    e_next, row0_next, nt_next = item_e_ref[w + 1], item_row0_ref[w + 1], item_nt_ref[w + 1]
    has_next = nt_next > 0
    slot = w & 1
    wf = ((wf_g0, wf_u0, wf_d0), (wf_g1, wf_u1, wf_d1))
    wb = ((wb_g0, wb_u0, wb_d0), (wb_g1, wb_u1, wb_d1))

    def w_copies(ee, j, ws):
        col = pl.ds(pl.multiple_of(j * MOE_TF, MOE_TF), MOE_TF)
        f_g, f_u, f_d = wf[ws]
        return (pltpu.make_async_copy(wg_hbm.at[ee, :, col], f_g, w_sems.at[ws, 0]),
                pltpu.make_async_copy(wu_hbm.at[ee, :, col], f_u, w_sems.at[ws, 1]),
                pltpu.make_async_copy(wd_hbm.at[ee, col, :], f_d, w_sems.at[ws, 2]))

    def w_start(ee, j, ws):
        for cp in w_copies(ee, j, ws):
            cp.start()

    def w_wait(ee, j, ws):
        for cp in w_copies(ee, j, ws):
            cp.wait()

    def w_cast(ws):
        for f_ref, b_ref in zip(wf[ws], wb[ws]):
            b_ref[...] = f_ref[...].astype(BF16)

    def x_copy(r0, t, ss):
        return pltpu.make_async_copy(
            xs_hbm.at[_tile_rows(r0 + t * MOE_TM, MOE_TM)], stage.at[ss, t], x_sems.at[ss, t])

    def out_copy(r0, t, ss):
        return pltpu.make_async_copy(
            stage.at[ss, t], out_hbm.at[_tile_rows(r0 + t * MOE_TM, MOE_TM)], o_sems.at[ss, t])

    def for_tiles(n, fn):
        for t in range(MOE_R):
            @pl.when(t < n)
            def _():
                fn(t)

    def gate_up(t, j, ws):
        b_g, b_u, _ = wb[ws]
        x = x_b[t]
        g = jnp.dot(x, b_g[...], preferred_element_type=F32) + bg_ref[j]
        u = jnp.dot(x, b_u[...], preferred_element_type=F32) + bu_ref[j]
        g = jnp.minimum(g, SWIGLU_LIMIT)
        u = jnp.clip(u, -SWIGLU_LIMIT, SWIGLU_LIMIT)
        return ((u + 1.0) * (g * jax.nn.sigmoid(SWIGLU_ALPHA * g))).astype(BF16)

    def down(t, hidden, ws):
        acc[t] += jnp.dot(hidden, wb[ws][2][...], preferred_element_type=F32)

    def ff_step(j, ws):
        @pl.when(j + 2 < MOE_N_FF)
        def _():
            w_start(e, j + 2, ws)

        @pl.when(jnp.logical_and(j + 2 >= MOE_N_FF, has_next))
        def _():
            w_start(e_next, j + 2 - MOE_N_FF, ws)

        @pl.when(j + 1 < MOE_N_FF)
        def _():
            w_wait(e, j + 1, 1 - ws)

        @pl.when(jnp.logical_and(j + 1 >= MOE_N_FF, has_next))
        def _():
            w_wait(e_next, 0, 1 - ws)

        w_cast(1 - ws)

        done = 0
        for size in MOE_GROUPS:
            def group(i, c, size=size, done=done):
                ts = [done + size * i + k for k in range(size)]
                hs = [gate_up(t, j, ws) for t in ts]
                for t, h in zip(ts, hs):
                    down(t, h, ws)
                return c

            n_group = (nt - done) // size
            lax.fori_loop(0, n_group, group, 0)
            done = done + n_group * size

    @pl.when(nt > 0)
    def _active():
        @pl.when(w == 0)
        def _prologue():
            w_start(e, 0, 0)
            w_start(e, 1, 1)
            for_tiles(nt, lambda t: x_copy(row0, t, 0).start())
            w_wait(e, 0, 0)
            w_cast(0)

        def load_tile(t):
            x_copy(row0, t, slot).wait()
            for a in range(PACK_SLABS):
                hi, lo = _unpack_slab(stage.at[slot, t], a, MOE_TM)
                x_b[t, :, a * LANES:(a + 1) * LANES] = hi.astype(BF16)
                x_b[t, :, PACK_HALF + a * LANES:PACK_HALF + (a + 1) * LANES] = lo.astype(BF16)
            acc[t] = jnp.broadcast_to(bd_ref[...], acc.shape[1:])

        for_tiles(nt, load_tile)

        def ff_pair(jj, c):
            ff_step(2 * jj, 0)

            @pl.when(jj == 0)
            def _():
                @pl.when(w > 0)
                def _():
                    for_tiles(nt_prev, lambda t: out_copy(row0_prev, t, 1 - slot).wait())

                @pl.when(has_next)
                def _():
                    for_tiles(nt_next, lambda t: x_copy(row0_next, t, 1 - slot).start())

            ff_step(2 * jj + 1, 1)
            return c

        lax.fori_loop(0, MOE_N_FF // 2, ff_pair, 0)

        def store_tile(t):
            _pack_rows(acc[t], stage.at[slot, t])
            out_copy(row0, t, slot).start()

        for_tiles(nt, store_tile)

        @pl.when(jnp.logical_not(has_next))
        def _():
            for_tiles(nt, lambda t: out_copy(row0, t, slot).wait())

    @pl.when(nz > 0)
    def _zero_tail():
        def zero_tile(t):
            stage[0, t] = jnp.zeros(stage.shape[2:], U32)
            out_copy(row0, t, 0).start()

        for_tiles(nz, zero_tile)
        for_tiles(nz, lambda t: out_copy(row0, t, 0).wait())


def _experts(item_e, item_row0, item_nt, item_nz, xs, w_gate, b_gate, w_up, b_up, w_down, b_down, n_items):
    d = D_MODEL
    w_shapes = [(d, MOE_TF), (d, MOE_TF), (MOE_TF, d)]
    by_expert = lambda w, ie, r0, nt, nz: (ie[w], 0, 0, 0)
    return pl.pallas_call(
        _experts_kernel,
        grid_spec=pltpu.PrefetchScalarGridSpec(
            num_scalar_prefetch=4,
            grid=(n_items,),
            in_specs=[
                pl.BlockSpec(memory_space=pl.ANY),
                pl.BlockSpec(memory_space=pl.ANY),
                pl.BlockSpec(memory_space=pl.ANY),
                pl.BlockSpec(memory_space=pl.ANY),
                pl.BlockSpec((None, MOE_N_FF, 1, MOE_TF), by_expert),
                pl.BlockSpec((None, MOE_N_FF, 1, MOE_TF), by_expert),
                pl.BlockSpec((None, 1, d), lambda w, ie, r0, nt, nz: (ie[w], 0, 0)),
            ],
            out_specs=pl.BlockSpec(memory_space=pl.ANY),
            scratch_shapes=[
                pltpu.VMEM((2, MOE_R, MOE_TM * PACK_SLABS, LANES), U32),
                pltpu.VMEM((MOE_R, MOE_TM, d), BF16),
                pltpu.VMEM((MOE_R, MOE_TM, d), F32),
                *[pltpu.VMEM(shape, F32) for shape in w_shapes * 2],
                *[pltpu.VMEM(shape, BF16) for shape in w_shapes * 2],
                pltpu.SemaphoreType.DMA((2, MOE_R)),
                pltpu.SemaphoreType.DMA((2, MOE_R)),
                pltpu.SemaphoreType.DMA((2, 3)),
            ],
        ),
        out_shape=jax.ShapeDtypeStruct(xs.shape, U32),
        compiler_params=pltpu.CompilerParams(
            dimension_semantics=("arbitrary",), vmem_limit_bytes=VMEM_LIMIT_BYTES),
        name="experts",
    )(item_e, item_row0, item_nt, item_nz, xs, w_gate, w_up, w_down,
      b_gate.reshape(N_EXPERTS, MOE_N_FF, 1, MOE_TF), b_up.reshape(N_EXPERTS, MOE_N_FF, 1, MOE_TF),
      b_down.reshape(N_EXPERTS, 1, d))


def _combine_kernel(dest_ref, rows_hbm, gates_ref, x1_ref, gpost_ref, gate_ref, o_ref, buf, sem):
    tt = COMBINE_TT
    base = (pl.program_id(0) * pl.num_programs(1) + pl.program_id(1)) * tt * TOP_K

    def row_copy(i, kk):
        return pltpu.make_async_copy(
            rows_hbm.at[_tile_rows(dest_ref[base + i * TOP_K + kk])], buf.at[kk, _tile_rows(i)], sem)

    def start(i, c):
        for kk in range(TOP_K):
            row_copy(i, kk).start(priority=kk % 2)
        return c

    def wait(i, c):
        for kk in range(TOP_K):
            row_copy(i, kk).wait()
        return c

    lax.fori_loop(0, tt, start, 0, unroll=4)
    lax.fori_loop(0, tt, wait, 0, unroll=4)
    gates = gates_ref[...]
    y_hi, y_lo = [], []
    for a in range(PACK_SLABS):
        acc_hi = acc_lo = None
        for kk in range(TOP_K):
            hi, lo = _unpack_slab(buf.at[kk], a, tt)
            gk = gates[:, kk:kk + 1]
            acc_hi = gk * hi if acc_hi is None else acc_hi + gk * hi
            acc_lo = gk * lo if acc_lo is None else acc_lo + gk * lo
        y_hi.append(acc_hi)
        y_lo.append(acc_lo)
    y = jnp.concatenate(y_hi + y_lo, axis=1)
    o_ref[...] = x1_ref[...] + gate_ref[...] * _rms(y, gpost_ref[...])


def _combine(dest, rows, gates, x1, gpost, gate):
    b, s, d = x1.shape
    tt = COMBINE_TT
    row = lambda bi, i, dst: (bi, i, 0)
    return pl.pallas_call(
        _combine_kernel,
        grid_spec=pltpu.PrefetchScalarGridSpec(
            num_scalar_prefetch=1,
            grid=(b, s // tt),
            in_specs=[
                pl.BlockSpec(memory_space=pl.ANY),
                pl.BlockSpec((None, tt, TOP_K), row),
                pl.BlockSpec((None, tt, d), row),
                pl.BlockSpec((1, d), lambda bi, i, dst: (0, 0)),
                pl.BlockSpec((None, 1, d), lambda bi, i, dst: (bi, 0, 0)),
            ],
            out_specs=pl.BlockSpec((None, tt, d), row),
            scratch_shapes=[pltpu.VMEM((TOP_K, tt * PACK_SLABS, LANES), U32), pltpu.SemaphoreType.DMA],
        ),
        out_shape=jax.ShapeDtypeStruct((b, s, d), F32),
        compiler_params=pltpu.CompilerParams(
            dimension_semantics=("arbitrary", "arbitrary"), vmem_limit_bytes=VMEM_LIMIT_BYTES),
        name="combine",
    )(dest, rows, gates, x1, gpost, gate)


def _rope_tables(seq_len):
    rows = seq_len // GRID_W
    row_idx = jnp.repeat(jnp.arange(rows, dtype=F32), GRID_W)
    col_idx = jnp.tile(jnp.arange(GRID_W, dtype=F32), rows)
    half = HEAD_DIM // 2
    inv_freq = 1.0 / (ROPE_THETA ** (jnp.arange(0, half, 2, dtype=F32) / half))
    ang = jnp.concatenate([row_idx[:, None] * inv_freq, col_idx[:, None] * inv_freq], axis=-1)
    cos = jnp.repeat(jnp.cos(ang), 2, axis=-1)
    sin = jnp.sin(ang)
    sin_signed = jnp.stack([-sin, sin], axis=-1).reshape(seq_len, HEAD_DIM)
    return cos, sin_signed


def _dft_tables(n):
    idx = np.arange(n, dtype=np.int64)
    ang = 2.0 * np.pi * ((idx[:, None] * idx[None, :]) % n).astype(np.float64) / n
    scale = 1.0 / np.sqrt(n)
    return np.cos(ang) * scale, np.sin(ang) * scale


def _topk_kernel(logit_ref, tri_ref, gate_ref, idx_ref, rank_ref, count_ref, base):
    step = pl.program_id(0)

    @pl.when(step == 0)
    def _():
        base[...] = jnp.zeros(base.shape, F32)

    lg = logit_ref[...]
    lane = lax.broadcasted_iota(jnp.int32, lg.shape, 1)
    v = jnp.where(lane < N_EXPERTS, lg, -jnp.inf)
    vals, hits = [], []
    for _ in range(TOP_K):
        m = jnp.max(v, axis=-1, keepdims=True)
        first = jnp.min(jnp.where(v == m, lane, ROUTER_PAD), axis=-1, keepdims=True)
        hit = lane == first
        vals.append(m)
        hits.append(hit)
        v = jnp.where(hit, -jnp.inf, v)
    ex = [jnp.exp(val - vals[0]) for val in vals]
    denom = ex[0]
    for e_k in ex[1:]:
        denom = denom + e_k
    chosen = hits[0]
    for hit in hits[1:]:
        chosen = jnp.logical_or(chosen, hit)
    onehot = chosen.astype(F32)
    before = jnp.dot(tri_ref[...], onehot.astype(BF16), preferred_element_type=F32) + base[...]
    lane_f = lane.astype(F32)
    gate_out = jnp.zeros(lg.shape, F32)
    idx_out = jnp.zeros(lg.shape, F32)
    rank_out = jnp.zeros(lg.shape, F32)
    for kk in range(TOP_K):
        sel = lane == kk
        gate_out = jnp.where(sel, ex[kk] / denom, gate_out)
        idx_out = jnp.where(sel, jnp.sum(jnp.where(hits[kk], lane_f, 0.0), axis=-1, keepdims=True), idx_out)
        rank_out = jnp.where(sel, jnp.sum(jnp.where(hits[kk], before, 0.0), axis=-1, keepdims=True), rank_out)
    gate_ref[...] = gate_out
    idx_ref[...] = idx_out.astype(jnp.int32)
    rank_ref[...] = rank_out.astype(jnp.int32)
    base[...] = base[...] + jnp.sum(onehot, axis=0, keepdims=True)
    count_ref[...] = base[...].astype(jnp.int32)


def _topk(logits):
    n_tok = logits.shape[0]
    tb = ROUTE_TB
    tri = jnp.asarray(np.tril(np.ones((tb, tb), np.float32), -1), dtype=BF16)
    blk = pl.BlockSpec((tb, ROUTER_PAD), lambda i: (i, 0))
    return pl.pallas_call(
        _topk_kernel,
        grid=(n_tok // tb,),
        in_specs=[blk, pl.BlockSpec((tb, tb), lambda i: (0, 0))],
        out_specs=[blk, blk, blk, pl.BlockSpec((1, ROUTER_PAD), lambda i: (0, 0))],
        out_shape=[
            jax.ShapeDtypeStruct((n_tok, ROUTER_PAD), F32),
            jax.ShapeDtypeStruct((n_tok, ROUTER_PAD), jnp.int32),
            jax.ShapeDtypeStruct((n_tok, ROUTER_PAD), jnp.int32),
            jax.ShapeDtypeStruct((1, ROUTER_PAD), jnp.int32),
        ],
        scratch_shapes=[pltpu.VMEM((1, ROUTER_PAD), F32)],
        compiler_params=pltpu.CompilerParams(
            dimension_semantics=("arbitrary",), vmem_limit_bytes=VMEM_LIMIT_BYTES),
        name="topk",
    )(logits, tri)


def _route(logits, n_tok):
    gates, top_idx, rank, counts = _topk(logits)
    gates = gates[:, :TOP_K]
    flat_e = top_idx[:, :TOP_K].reshape(-1)
    rank = rank[:, :TOP_K].reshape(-1)
    counts = counts[0, :N_EXPERTS]
    tiles = (counts + MOE_TM - 1) // MOE_TM
    padded = tiles * MOE_TM
    pad_end = jnp.cumsum(padded)
    pad_start = pad_end - padded
    dest = pad_start[flat_e] + rank
    n_rows = n_tok * TOP_K + N_EXPERTS * MOE_TM
    fill_lo = (pad_start + counts).astype(jnp.int32)
    fill_hi = pad_end.astype(jnp.int32)
    used_tiles = jnp.sum(tiles).astype(jnp.int32).reshape(1)
    n_items = N_EXPERTS + n_rows // MOE_RS
    items_per_e = (tiles + MOE_R - 1) // MOE_R
    item_end = jnp.cumsum(items_per_e)
    item_start = item_end - items_per_e
    w = jnp.arange(n_items, dtype=jnp.int32)
    total = item_end[-1]
    e_of_w = jnp.minimum(jnp.searchsorted(item_end, w, side='right'), N_EXPERTS - 1).astype(jnp.int32)
    local = w - item_start[e_of_w]
    active = w < total
    last_e = e_of_w[jnp.maximum(total - 1, 0)]
    item_e = jnp.where(active, e_of_w, last_e).astype(jnp.int32)
    item_nt = jnp.where(active, jnp.minimum(MOE_R, tiles[e_of_w] - local * MOE_R), 0).astype(jnp.int32)
    n_tiles = n_rows // MOE_TM
    fill_tile0 = jnp.sum(tiles) + (w - total) * MOE_R
    item_nz = jnp.where(active, 0, jnp.clip(n_tiles - fill_tile0, 0, MOE_R)).astype(jnp.int32)
    fill_row0 = jnp.minimum(fill_tile0, n_tiles - 1) * MOE_TM
    item_row0 = jnp.where(active, pad_start[e_of_w] + local * MOE_RS, fill_row0).astype(jnp.int32)
    fill = (fill_lo, fill_hi, used_tiles)
    items = tuple(jnp.pad(a, (0, 1)) for a in (item_e, item_row0, item_nt, item_nz))
    return gates, dest.astype(jnp.int32), fill, items, n_rows, n_items


def kernel(x, c, w_ada, b_ada, g_pre_mix, w_in, w_fourier, q_norm_g, k_norm_g, g_fourier_out, g_attn_out,
           w_out, g_post_mix, g_pre_ffn, w_router, b_router, w_gate, b_gate, w_up, b_up, w_down, b_down,
           g_post_ffn):
    b, s, d = x.shape
    n_tok = b * s
    depth = w_ada.shape[0]
    cos, sin_signed = _rope_tables(s)
    cs_np, ss_np = _dft_tables(s)
    cs = jnp.asarray(cs_np, dtype=BF16)
    ss = jnp.asarray(ss_np, dtype=BF16)
    cc_np, sc_np = _dft_tables(FOURIER_GROUP_DIM)
    cc = jnp.asarray(np.concatenate([cc_np, -sc_np], axis=0), dtype=BF16)
    c_pad = jnp.pad(c, ((0, 8 - b), (0, 0)))
    row2 = lambda a: a.reshape(1, -1)

    for l in range(depth):
        mod = _ada(c_pad, w_ada[l], row2(b_ada[l]))[:b].reshape(b, N_MOD, 1, d)
        shift_m, scale_m, gate_m = mod[:, 0], mod[:, 1], mod[:, 2]
        shift_f, scale_f, gate_f = mod[:, 3], mod[:, 4], mod[:, 5]

        f, q, k, v = _inproj(
            x, row2(g_pre_mix[l]), shift_m, scale_m, w_in[l].astype(BF16),
            row2(q_norm_g[l] * (HEAD_DIM ** -0.5)), row2(k_norm_g[l]), cos, sin_signed)
        fo = _fourier(f, cs, ss, cc, w_fourier[l], row2(g_fourier_out[l]))
        ao = _attn(q, k, v, row2(g_attn_out[l]))

        wr = jnp.pad(w_router[l], ((0, 0), (0, ROUTER_PAD - N_EXPERTS)))
        wr_hi = wr.astype(BF16)
        wr_lo = (wr - wr_hi.astype(F32)).astype(BF16)
        br = jnp.pad(b_router[l], (0, ROUTER_PAD - N_EXPERTS)).reshape(1, ROUTER_PAD)
        x1, h2p, logits = _outproj(
            fo, ao, w_out[l].astype(BF16), x, row2(g_post_mix[l]), gate_m, row2(g_pre_ffn[l]),
            shift_f, scale_f, wr_hi, wr_lo, br)

        gates, dest, fill, items, n_rows, n_items = _route(logits.reshape(n_tok, ROUTER_PAD), n_tok)
        xs = _dispatch(dest, *fill, h2p, n_rows)
        rows = _experts(
            *items, xs,
            w_gate[l], b_gate[l], w_up[l], b_up[l], w_down[l], b_down[l], n_items)
        x = _combine(dest, rows, gates.reshape(b, s, TOP_K), x1, row2(g_post_ffn[l]), gate_f)
    return x
```

```python
import functools

import numpy as np
import jax
import jax.numpy as jnp
from jax import lax
from jax.experimental import pallas as pl
from jax.experimental.pallas import tpu as pltpu

F32 = jnp.float32
BF16 = jnp.bfloat16
U32 = jnp.uint32

D_MODEL = 2048
GRID_W = 64
FOURIER_WIDTH = 512
FOURIER_GROUP_DIM = 128
FOURIER_GROUPS = 4
ATTN_WIDTH = 1536
HEAD_DIM = 128
N_Q_HEADS = 12
GQA_GROUP = 3
N_KV_HEADS = 4
KV_WIDTH = 512
IN_WIDTH = 3072
ROPE_THETA = 10000.0
N_EXPERTS = 32
TOP_K = 4
D_FF = 2048
SWIGLU_LIMIT = 7.0
SWIGLU_ALPHA = 1.702
N_MOD = 6
EPS = 1e-6
LOG2_E = 1.4426950408889634

VMEM_LIMIT_BYTES = 56 * 1024 * 1024
LANES = 128

ADA_TN = 1024
INPROJ_TM = 256
FOURIER_TR = 256
ATTN_TQ = 128
OUTPROJ_TM = 256
ROUTER_PAD = LANES
ROUTE_TB = 512
MOE_TM = 256
MOE_R = 5
MOE_RS = MOE_TM * MOE_R
MOE_TF = 256
MOE_N_FF = D_FF // MOE_TF
MOE_GROUPS = (5, 4, 2, 1)
DISPATCH_TT = 512
COMBINE_TT = 128
PACK_HALF = D_MODEL // 2
PACK_SLABS = PACK_HALF // LANES


def _rms(x, g):
    return x * lax.rsqrt(jnp.mean(x * x, axis=-1, keepdims=True) + EPS) * g


def _ada_kernel(c_ref, w_ref, b_ref, o_ref):
    c = c_ref[...]
    s = (c * jax.nn.sigmoid(c)).astype(BF16)
    o_ref[...] = jnp.dot(s, w_ref[...].astype(BF16), preferred_element_type=F32) + b_ref[...]


def _ada(c_pad, w_ada, b_ada):
    m, d = c_pad.shape
    n = w_ada.shape[1]
    return pl.pallas_call(
        _ada_kernel,
        grid=(n // ADA_TN,),
        in_specs=[
            pl.BlockSpec((m, d), lambda j: (0, 0)),
            pl.BlockSpec((d, ADA_TN), lambda j: (0, j)),
            pl.BlockSpec((1, ADA_TN), lambda j: (0, j)),
        ],
        out_specs=pl.BlockSpec((m, ADA_TN), lambda j: (0, j)),
        out_shape=jax.ShapeDtypeStruct((m, n), F32),
        compiler_params=pltpu.CompilerParams(
            dimension_semantics=("arbitrary",), vmem_limit_bytes=VMEM_LIMIT_BYTES),
        name="ada",
    )(c_pad, w_ada, b_ada)


def _rope(p, cos, sin_signed, even_lane):
    partner = jnp.where(even_lane, pltpu.roll(p, LANES - 1, 1), pltpu.roll(p, 1, 1))
    return p * cos + partner * sin_signed


def _inproj_kernel(x_ref, g_ref, shift_ref, scale_ref, w_ref, qg_ref, kg_ref, cos_ref, sin_ref,
                   f_ref, q_ref, k_ref, v_ref):
    x = x_ref[...]
    h = _rms(x, g_ref[...]) * (1.0 + scale_ref[...]) + shift_ref[...]
    hb = h.astype(BF16)
    cos = cos_ref[...]
    sin = sin_ref[...]
    even_lane = (lax.broadcasted_iota(jnp.int32, cos.shape, 1) % 2) == 0
    chunk = 512
    heads_per_chunk = chunk // HEAD_DIM
    for ci in range(IN_WIDTH // chunk):
        p = jnp.dot(hb, w_ref[:, ci * chunk:(ci + 1) * chunk], preferred_element_type=F32)
        if ci == 0:
            f_ref[...] = p.astype(BF16)
        elif ci == 5:
            v_ref[...] = p.astype(BF16)
        else:
            gain = kg_ref[...] if ci == 4 else qg_ref[...]
            for hh in range(heads_per_chunk):
                ph = p[:, hh * HEAD_DIM:(hh + 1) * HEAD_DIM]
                ph = _rope(_rms(ph, gain), cos, sin, even_lane).astype(BF16)
                if ci == 4:
                    k_ref[:, hh * HEAD_DIM:(hh + 1) * HEAD_DIM] = ph
                else:
                    c0 = (ci - 1) * chunk + hh * HEAD_DIM
                    q_ref[:, c0:c0 + HEAD_DIM] = ph


def _inproj(x, g, shift, scale, w_in_b, qg, kg, cos, sin):
    b, s, d = x.shape
    tm = INPROJ_TM
    row = lambda bi, i: (bi, i, 0)
    per_batch = lambda bi, i: (bi, 0, 0)
    const2 = lambda bi, i: (0, 0)
    return pl.pallas_call(
        _inproj_kernel,
        grid=(b, s // tm),
        in_specs=[
            pl.BlockSpec((None, tm, d), row),
            pl.BlockSpec((1, d), const2),
            pl.BlockSpec((None, 1, d), per_batch),
            pl.BlockSpec((None, 1, d), per_batch),
            pl.BlockSpec((d, IN_WIDTH), const2),
            pl.BlockSpec((1, HEAD_DIM), const2),
            pl.BlockSpec((1, HEAD_DIM), const2),
            pl.BlockSpec((tm, HEAD_DIM), lambda bi, i: (i, 0)),
            pl.BlockSpec((tm, HEAD_DIM), lambda bi, i: (i, 0)),
        ],
        out_specs=[
            pl.BlockSpec((None, tm, FOURIER_WIDTH), row),
            pl.BlockSpec((None, tm, ATTN_WIDTH), row),
            pl.BlockSpec((None, tm, KV_WIDTH), row),
            pl.BlockSpec((None, tm, KV_WIDTH), row),
        ],
        out_shape=[
            jax.ShapeDtypeStruct((b, s, FOURIER_WIDTH), BF16),
            jax.ShapeDtypeStruct((b, s, ATTN_WIDTH), BF16),
            jax.ShapeDtypeStruct((b, s, KV_WIDTH), BF16),
            jax.ShapeDtypeStruct((b, s, KV_WIDTH), BF16),
        ],
        compiler_params=pltpu.CompilerParams(
            dimension_semantics=("arbitrary", "arbitrary"), vmem_limit_bytes=VMEM_LIMIT_BYTES),
        name="inproj",
    )(x, g, shift, scale, w_in_b, qg, kg, cos, sin)


def _fourier_kernel(cs_ref, ss_ref, f_ref, cc_ref, w_ref, g_ref, o_ref):
    f = f_ref[...]
    a = jnp.dot(cs_ref[...], f, preferred_element_type=F32)
    bm = jnp.dot(ss_ref[...], f, preferred_element_type=F32)
    cc = cc_ref[...]
    outs = []
    for gi in range(FOURIER_GROUPS):
        sl = slice(gi * FOURIER_GROUP_DIM, (gi + 1) * FOURIER_GROUP_DIM)
        ab = jnp.concatenate([a[:, sl], bm[:, sl]], axis=1).astype(BF16)
        fm = jnp.dot(ab, cc, preferred_element_type=F32)
        outs.append(jnp.dot(fm.astype(BF16), w_ref[gi].astype(BF16), preferred_element_type=F32))
    fo = jnp.concatenate(outs, axis=1)
    o_ref[...] = _rms(fo, g_ref[...]).astype(BF16)


def _fourier(f, cs, ss, cc, w_fourier, g):
    b, s, fw = f.shape
    tr = FOURIER_TR
    return pl.pallas_call(
        _fourier_kernel,
        grid=(b, s // tr),
        in_specs=[
            pl.BlockSpec((tr, s), lambda bi, i: (i, 0)),
            pl.BlockSpec((tr, s), lambda bi, i: (i, 0)),
            pl.BlockSpec((None, s, fw), lambda bi, i: (bi, 0, 0)),
            pl.BlockSpec((2 * FOURIER_GROUP_DIM, FOURIER_GROUP_DIM), lambda bi, i: (0, 0)),
            pl.BlockSpec((FOURIER_GROUPS, FOURIER_GROUP_DIM, FOURIER_GROUP_DIM), lambda bi, i: (0, 0, 0)),
            pl.BlockSpec((1, fw), lambda bi, i: (0, 0)),
        ],
        out_specs=pl.BlockSpec((None, tr, fw), lambda bi, i: (bi, i, 0)),
        out_shape=jax.ShapeDtypeStruct((b, s, fw), BF16),
        compiler_params=pltpu.CompilerParams(
            dimension_semantics=("arbitrary", "arbitrary"), vmem_limit_bytes=VMEM_LIMIT_BYTES),
        name="fourier",
    )(cs, ss, f, cc, w_fourier, g)


def _attn_kernel(q_ref, k_ref, v_ref, g_ref, o_ref, ao_ref):
    tq = q_ref.shape[0]
    for kv in range(N_KV_HEADS):
        kh = k_ref[:, kv * HEAD_DIM:(kv + 1) * HEAD_DIM]
        vh = v_ref[:, kv * HEAD_DIM:(kv + 1) * HEAD_DIM]
        c0 = kv * GQA_GROUP * HEAD_DIM
        qh = jnp.concatenate(
            [q_ref[:, c0 + gi * HEAD_DIM:c0 + (gi + 1) * HEAD_DIM] for gi in range(GQA_GROUP)], axis=0)
        sc = lax.dot_general(qh, kh, (((1,), (1,)), ((), ())), preferred_element_type=F32)
        m = jnp.max(sc, axis=-1, keepdims=True)
        p = jnp.exp2(sc - m)
        l = jnp.sum(p, axis=-1, keepdims=True)
        o = jnp.dot(p.astype(BF16), vh, preferred_element_type=F32) / l
        for gi in range(GQA_GROUP):
            ao_ref[:, c0 + gi * HEAD_DIM:c0 + (gi + 1) * HEAD_DIM] = o[gi * tq:(gi + 1) * tq]
    o_ref[...] = _rms(ao_ref[...], g_ref[...]).astype(BF16)


def _attn(q, k, v, g):
    b, s, _ = q.shape
    tq = ATTN_TQ
    return pl.pallas_call(
        _attn_kernel,
        grid=(b, s // tq),
        in_specs=[
            pl.BlockSpec((None, tq, ATTN_WIDTH), lambda bi, i: (bi, i, 0)),
            pl.BlockSpec((None, s, KV_WIDTH), lambda bi, i: (bi, 0, 0)),
            pl.BlockSpec((None, s, KV_WIDTH), lambda bi, i: (bi, 0, 0)),
            pl.BlockSpec((1, ATTN_WIDTH), lambda bi, i: (0, 0)),
        ],
        out_specs=pl.BlockSpec((None, tq, ATTN_WIDTH), lambda bi, i: (bi, i, 0)),
        out_shape=jax.ShapeDtypeStruct((b, s, ATTN_WIDTH), BF16),
        scratch_shapes=[pltpu.VMEM((tq, ATTN_WIDTH), F32)],
        compiler_params=pltpu.CompilerParams(
            dimension_semantics=("arbitrary", "arbitrary"), vmem_limit_bytes=VMEM_LIMIT_BYTES),
        name="attn",
    )(q, k, v, g)


def _split_bf16(a):
    hi = a.astype(BF16)
    lo = (a - hi.astype(F32)).astype(BF16)
    return hi, lo


def _pack_rows(val, dst_ref):
    n = val.shape[0]
    bits = lax.bitcast_convert_type(val.astype(BF16).astype(F32), U32)
    packed = bits[:, :PACK_HALF] | (bits[:, PACK_HALF:] >> 16)
    for a in range(PACK_SLABS):
        dst_ref[pl.ds(a, n, stride=PACK_SLABS), :] = packed[:, a * LANES:(a + 1) * LANES]


def _unpack_slab(src_ref, a, n):
    word = src_ref[pl.ds(a, n, stride=PACK_SLABS), :]
    hi = lax.bitcast_convert_type(word & jnp.uint32(0xFFFF0000), F32)
    lo = lax.bitcast_convert_type(word << 16, F32)
    return hi, lo


def _outproj_kernel(fo_ref, ao_ref, w_ref, x_ref, gpost_ref, gate_ref, gpre_ref, shift_ref, scale_ref,
                    wr_hi_ref, wr_lo_ref, br_ref, x1_ref, h2p_ref, logit_ref):
    mix = jnp.dot(fo_ref[...], w_ref[:FOURIER_WIDTH, :], preferred_element_type=F32)
    mix = mix + jnp.dot(ao_ref[...], w_ref[FOURIER_WIDTH:, :], preferred_element_type=F32)
    x1 = x_ref[...] + gate_ref[...] * _rms(mix, gpost_ref[...])
    x1_ref[...] = x1
    h2 = _rms(x1, gpre_ref[...]) * (1.0 + scale_ref[...]) + shift_ref[...]
    _pack_rows(h2, h2p_ref)
    h_hi, h_lo = _split_bf16(h2)
    lg = jnp.dot(h_hi, wr_hi_ref[...], preferred_element_type=F32)
    lg = lg + jnp.dot(h_hi, wr_lo_ref[...], preferred_element_type=F32)
    lg = lg + jnp.dot(h_lo, wr_hi_ref[...], preferred_element_type=F32)
    logit_ref[...] = lg + br_ref[...]


def _outproj(fo, ao, w_out_b, x, gpost, gate, gpre, shift, scale, wr_hi, wr_lo, br):
    b, s, d = x.shape
    tm = OUTPROJ_TM
    row = lambda bi, i: (bi, i, 0)
    per_batch = lambda bi, i: (bi, 0, 0)
    const2 = lambda bi, i: (0, 0)
    return pl.pallas_call(
        _outproj_kernel,
        grid=(b, s // tm),
        in_specs=[
            pl.BlockSpec((None, tm, FOURIER_WIDTH), row),
            pl.BlockSpec((None, tm, ATTN_WIDTH), row),
            pl.BlockSpec((d, d), const2),
            pl.BlockSpec((None, tm, d), row),
            pl.BlockSpec((1, d), const2),
            pl.BlockSpec((None, 1, d), per_batch),
            pl.BlockSpec((1, d), const2),
            pl.BlockSpec((None, 1, d), per_batch),
            pl.BlockSpec((None, 1, d), per_batch),
            pl.BlockSpec((d, ROUTER_PAD), const2),
            pl.BlockSpec((d, ROUTER_PAD), const2),
            pl.BlockSpec((1, ROUTER_PAD), const2),
        ],
        out_specs=[
            pl.BlockSpec((None, tm, d), row),
            pl.BlockSpec((tm * PACK_SLABS, LANES), lambda bi, i: (bi * (s // tm) + i, 0)),
            pl.BlockSpec((None, tm, ROUTER_PAD), row),
        ],
        out_shape=[
            jax.ShapeDtypeStruct((b, s, d), F32),
            jax.ShapeDtypeStruct((b * s * PACK_SLABS, LANES), U32),
            jax.ShapeDtypeStruct((b, s, ROUTER_PAD), F32),
        ],
        compiler_params=pltpu.CompilerParams(
            dimension_semantics=("arbitrary", "arbitrary"), vmem_limit_bytes=VMEM_LIMIT_BYTES),
        name="outproj",
    )(fo, ao, w_out_b, x, gpost, gate, gpre, shift, scale, wr_hi, wr_lo, br)


def _tile_rows(row, n_rows=1):
    return pl.ds(pl.multiple_of(row * PACK_SLABS, PACK_SLABS), n_rows * PACK_SLABS)


def _dispatch_kernel(dest_ref, fill_lo_ref, fill_hi_ref, used_tiles_ref, h_ref, xs_hbm, zeros, sem, fill_sem):
    step = pl.program_id(0)
    base = step * (DISPATCH_TT * TOP_K)

    def row_copy(i, kk):
        return pltpu.make_async_copy(
            h_ref.at[_tile_rows(i)], xs_hbm.at[_tile_rows(dest_ref[base + i * TOP_K + kk])], sem)

    def start(i, c):
        for kk in range(TOP_K):
            row_copy(i, kk).start(priority=kk % 2)
        return c

    def wait(i, c):
        for kk in range(TOP_K):
            row_copy(i, kk).wait()
        return c

    lax.fori_loop(0, DISPATCH_TT, start, 0, unroll=4)

    @pl.when(step == 0)
    def _fill():
        zeros[...] = jnp.zeros(zeros.shape, U32)
        n_tiles = xs_hbm.shape[0] // (MOE_TM * PACK_SLABS)

        def pad_chunks(e, fn):
            lo = fill_lo_ref[e]
            length = fill_hi_ref[e] - lo
            for bit in (128, 64, 32, 16, 8, 4, 2, 1):
                @pl.when((length & bit) != 0)
                def _():
                    row = lo + (length & ~(2 * bit - 1))
                    fn(pltpu.make_async_copy(
                        zeros.at[pl.ds(0, bit * PACK_SLABS)], xs_hbm.at[_tile_rows(row, bit)], fill_sem))

        def tail_copy(t):
            return pltpu.make_async_copy(zeros, xs_hbm.at[_tile_rows(t * MOE_TM, MOE_TM)], fill_sem)

        def start_e(e, c):
            pad_chunks(e, lambda cp: cp.start())
            return c

        def wait_e(e, c):
            pad_chunks(e, lambda cp: cp.wait())
            return c

        def start_t(t, c):
            tail_copy(t).start()
            return c

        def wait_t(t, c):
            tail_copy(t).wait()
            return c

        lax.fori_loop(0, N_EXPERTS, start_e, 0)
        lax.fori_loop(used_tiles_ref[0], n_tiles, start_t, 0)
        lax.fori_loop(0, N_EXPERTS, wait_e, 0)
        lax.fori_loop(used_tiles_ref[0], n_tiles, wait_t, 0)

    lax.fori_loop(0, DISPATCH_TT, wait, 0, unroll=4)


def _dispatch(dest, fill_lo, fill_hi, used_tiles, h2p, n_rows):
    n_tok = h2p.shape[0] // PACK_SLABS
    return pl.pallas_call(
        _dispatch_kernel,
        grid_spec=pltpu.PrefetchScalarGridSpec(
            num_scalar_prefetch=4,
            grid=(n_tok // DISPATCH_TT,),
            in_specs=[pl.BlockSpec((DISPATCH_TT * PACK_SLABS, LANES), lambda i, *_: (i, 0))],
            out_specs=pl.BlockSpec(memory_space=pl.ANY),
            scratch_shapes=[
                pltpu.VMEM((MOE_TM * PACK_SLABS, LANES), U32),
                pltpu.SemaphoreType.DMA,
                pltpu.SemaphoreType.DMA,
            ],
        ),
        out_shape=jax.ShapeDtypeStruct((n_rows * PACK_SLABS, LANES), U32),
        compiler_params=pltpu.CompilerParams(
            dimension_semantics=("arbitrary",), vmem_limit_bytes=VMEM_LIMIT_BYTES),
        name="dispatch",
    )(dest, fill_lo, fill_hi, used_tiles, h2p)


def _experts_kernel(item_e_ref, item_row0_ref, item_nt_ref, item_nz_ref,
                    xs_hbm, wg_hbm, wu_hbm, wd_hbm, bg_ref, bu_ref, bd_ref, out_hbm,
                    stage, x_b, acc, wf_g0, wf_u0, wf_d0, wf_g1, wf_u1, wf_d1,
                    wb_g0, wb_u0, wb_d0, wb_g1, wb_u1, wb_d1, x_sems, o_sems, w_sems):
    w = pl.program_id(0)
    nt = item_nt_ref[w]
    nz = item_nz_ref[w]
    e = item_e_ref[w]
    row0 = item_row0_ref[w]
    w_prev = jnp.maximum(w - 1, 0)
    row0_prev, nt_prev = item_row0_ref[w_prev], item_nt_ref[w_prev]
    e_next, row0_next, nt_next = item_e_ref[w + 1], item_row0_ref[w + 1], item_nt_ref[w + 1]
    has_next = nt_next > 0
    slot = w & 1
    wf = ((wf_g0, wf_u0, wf_d0), (wf_g1, wf_u1, wf_d1))
    wb = ((wb_g0, wb_u0, wb_d0), (wb_g1, wb_u1, wb_d1))

    def w_copies(ee, j, ws):
        col = pl.ds(pl.multiple_of(j * MOE_TF, MOE_TF), MOE_TF)
        f_g, f_u, f_d = wf[ws]
        return (pltpu.make_async_copy(wg_hbm.at[ee, :, col], f_g, w_sems.at[ws, 0]),
                pltpu.make_async_copy(wu_hbm.at[ee, :, col], f_u, w_sems.at[ws, 1]),
                pltpu.make_async_copy(wd_hbm.at[ee, col, :], f_d, w_sems.at[ws, 2]))

    def w_start(ee, j, ws):
        for cp in w_copies(ee, j, ws):
            cp.start()

    def w_wait(ee, j, ws):
        for cp in w_copies(ee, j, ws):
            cp.wait()

    def w_cast(ws):
        for f_ref, b_ref in zip(wf[ws], wb[ws]):
            b_ref[...] = f_ref[...].astype(BF16)

    def x_copy(r0, t, ss):
        return pltpu.make_async_copy(
            xs_hbm.at[_tile_rows(r0 + t * MOE_TM, MOE_TM)], stage.at[ss, t], x_sems.at[ss, t])

    def out_copy(r0, t, ss):
        return pltpu.make_async_copy(
            stage.at[ss, t], out_hbm.at[_tile_rows(r0 + t * MOE_TM, MOE_TM)], o_sems.at[ss, t])

    def for_tiles(n, fn):
        for t in range(MOE_R):
            @pl.when(t < n)
            def _():
                fn(t)

    def gate_up(t, j, ws):
        b_g, b_u, _ = wb[ws]
        x = x_b[t]
        g = jnp.dot(x, b_g[...], preferred_element_type=F32) + bg_ref[j]
        u = jnp.dot(x, b_u[...], preferred_element_type=F32) + bu_ref[j]
        g = jnp.minimum(g, SWIGLU_LIMIT)
        u = jnp.clip(u, -SWIGLU_LIMIT, SWIGLU_LIMIT)
        return ((u + 1.0) * (g * jax.nn.sigmoid(SWIGLU_ALPHA * g))).astype(BF16)

    def down(t, hidden, ws):
        acc[t] += jnp.dot(hidden, wb[ws][2][...], preferred_element_type=F32)

    def ff_step(j, ws):
        @pl.when(j + 2 < MOE_N_FF)
        def _():
            w_start(e, j + 2, ws)

        @pl.when(jnp.logical_and(j + 2 >= MOE_N_FF, has_next))
        def _():
            w_start(e_next, j + 2 - MOE_N_FF, ws)

        @pl.when(j + 1 < MOE_N_FF)
        def _():
            w_wait(e, j + 1, 1 - ws)

        @pl.when(jnp.logical_and(j + 1 >= MOE_N_FF, has_next))
        def _():
            w_wait(e_next, 0, 1 - ws)

        w_cast(1 - ws)

        done = 0
        for size in MOE_GROUPS:
            def group(i, c, size=size, done=done):
                ts = [done + size * i + k for k in range(size)]
                hs = [gate_up(t, j, ws) for t in ts]
                for t, h in zip(ts, hs):
                    down(t, h, ws)
                return c

            n_group = (nt - done) // size
            lax.fori_loop(0, n_group, group, 0)
            done = done + n_group * size

    @pl.when(nt > 0)
    def _active():
        @pl.when(w == 0)
        def _prologue():
            w_start(e, 0, 0)
            w_start(e, 1, 1)
            for_tiles(nt, lambda t: x_copy(row0, t, 0).start())
            w_wait(e, 0, 0)
            w_cast(0)

        def load_tile(t):
            x_copy(row0, t, slot).wait()
            for a in range(PACK_SLABS):
                hi, lo = _unpack_slab(stage.at[slot, t], a, MOE_TM)
                x_b[t, :, a * LANES:(a + 1) * LANES] = hi.astype(BF16)
                x_b[t, :, PACK_HALF + a * LANES:PACK_HALF + (a + 1) * LANES] = lo.astype(BF16)
            acc[t] = jnp.broadcast_to(bd_ref[...], acc.shape[1:])

        for_tiles(nt, load_tile)

        def ff_pair(jj, c):
            ff_step(2 * jj, 0)

            @pl.when(jj == 0)
            def _():
                @pl.when(w > 0)
                def _():
                    for_tiles(nt_prev, lambda t: out_copy(row0_prev, t, 1 - slot).wait())

                @pl.when(has_next)
                def _():
                    for_tiles(nt_next, lambda t: x_copy(row0_next, t, 1 - slot).start())

            ff_step(2 * jj + 1, 1)
            return c

        lax.fori_loop(0, MOE_N_FF // 2, ff_pair, 0)

        def store_tile(t):
            _pack_rows(acc[t], stage.at[slot, t])
            out_copy(row0, t, slot).start()

        for_tiles(nt, store_tile)

        @pl.when(jnp.logical_not(has_next))
        def _():
            for_tiles(nt, lambda t: out_copy(row0, t, slot).wait())

    @pl.when(nz > 0)
    def _zero_tail():
        def zero_tile(t):
            stage[0, t] = jnp.zeros(stage.shape[2:], U32)
            out_copy(row0, t, 0).start()

        for_tiles(nz, zero_tile)
        for_tiles(nz, lambda t: out_copy(row0, t, 0).wait())


def _experts(item_e, item_row0, item_nt, item_nz, xs, w_gate, b_gate, w_up, b_up, w_down, b_down, n_items):
    d = D_MODEL
    w_shapes = [(d, MOE_TF), (d, MOE_TF), (MOE_TF, d)]
    by_expert = lambda w, ie, r0, nt, nz: (ie[w], 0, 0, 0)
    return pl.pallas_call(
        _experts_kernel,
        grid_spec=pltpu.PrefetchScalarGridSpec(
            num_scalar_prefetch=4,
            grid=(n_items,),
            in_specs=[
                pl.BlockSpec(memory_space=pl.ANY),
                pl.BlockSpec(memory_space=pl.ANY),
                pl.BlockSpec(memory_space=pl.ANY),
                pl.BlockSpec(memory_space=pl.ANY),
                pl.BlockSpec((None, MOE_N_FF, 1, MOE_TF), by_expert),
                pl.BlockSpec((None, MOE_N_FF, 1, MOE_TF), by_expert),
                pl.BlockSpec((None, 1, d), lambda w, ie, r0, nt, nz: (ie[w], 0, 0)),
            ],
            out_specs=pl.BlockSpec(memory_space=pl.ANY),
            scratch_shapes=[
                pltpu.VMEM((2, MOE_R, MOE_TM * PACK_SLABS, LANES), U32),
                pltpu.VMEM((MOE_R, MOE_TM, d), BF16),
                pltpu.VMEM((MOE_R, MOE_TM, d), F32),
                *[pltpu.VMEM(shape, F32) for shape in w_shapes * 2],
                *[pltpu.VMEM(shape, BF16) for shape in w_shapes * 2],
                pltpu.SemaphoreType.DMA((2, MOE_R)),
                pltpu.SemaphoreType.DMA((2, MOE_R)),
                pltpu.SemaphoreType.DMA((2, 3)),
            ],
        ),
        out_shape=jax.ShapeDtypeStruct(xs.shape, U32),
        compiler_params=pltpu.CompilerParams(
            dimension_semantics=("arbitrary",), vmem_limit_bytes=VMEM_LIMIT_BYTES),
        name="experts",
    )(item_e, item_row0, item_nt, item_nz, xs, w_gate, w_up, w_down,
      b_gate.reshape(N_EXPERTS, MOE_N_FF, 1, MOE_TF), b_up.reshape(N_EXPERTS, MOE_N_FF, 1, MOE_TF),
      b_down.reshape(N_EXPERTS, 1, d))


def _combine_kernel(dest_ref, rows_hbm, gates_ref, x1_ref, gpost_ref, gate_ref, o_ref, buf, sems):
    tt = COMBINE_TT
    step = pl.program_id(0) * pl.num_programs(1) + pl.program_id(1)
    n_steps = pl.num_programs(0) * pl.num_programs(1)
    slot = step & 1

    def row_copy(s, ss, i, kk):
        row = dest_ref[(s * tt + i) * TOP_K + kk]
        return pltpu.make_async_copy(rows_hbm.at[_tile_rows(row)], buf.at[ss, kk, _tile_rows(i)], sems.at[ss])

    def start_step(s, ss):
        def start(i, c):
            for kk in range(TOP_K):
                row_copy(s, ss, i, kk).start(priority=kk % 2)
            return c

        lax.fori_loop(0, tt, start, 0, unroll=4)

    @pl.when(step == 0)
    def _():
        start_step(step, slot)

    @pl.when(step + 1 < n_steps)
    def _():
        start_step(step + 1, 1 - slot)

    def wait(i, c):
        for kk in range(TOP_K):
            row_copy(step, slot, i, kk).wait()
        return c

    lax.fori_loop(0, tt, wait, 0, unroll=4)
    gates = gates_ref[...]
    y_hi, y_lo = [], []
    for a in range(PACK_SLABS):
        acc_hi = acc_lo = None
        for kk in range(TOP_K):
            hi, lo = _unpack_slab(buf.at[slot, kk], a, tt)
            gk = gates[:, kk:kk + 1]
            acc_hi = gk * hi if acc_hi is None else acc_hi + gk * hi
            acc_lo = gk * lo if acc_lo is None else acc_lo + gk * lo
        y_hi.append(acc_hi)
        y_lo.append(acc_lo)
    y = jnp.concatenate(y_hi + y_lo, axis=1)
    o_ref[...] = x1_ref[...] + gate_ref[...] * _rms(y, gpost_ref[...])


def _combine(dest, rows, gates, x1, gpost, gate):
    b, s, d = x1.shape
    tt = COMBINE_TT
    row = lambda bi, i, dst: (bi, i, 0)
    return pl.pallas_call(
        _combine_kernel,
        grid_spec=pltpu.PrefetchScalarGridSpec(
            num_scalar_prefetch=1,
            grid=(b, s // tt),
            in_specs=[
                pl.BlockSpec(memory_space=pl.ANY),
                pl.BlockSpec((None, tt, TOP_K), row),
                pl.BlockSpec((None, tt, d), row),
                pl.BlockSpec((1, d), lambda bi, i, dst: (0, 0)),
                pl.BlockSpec((None, 1, d), lambda bi, i, dst: (bi, 0, 0)),
            ],
            out_specs=pl.BlockSpec((None, tt, d), row),
            scratch_shapes=[pltpu.VMEM((2, TOP_K, tt * PACK_SLABS, LANES), U32),
                            pltpu.SemaphoreType.DMA((2,))],
        ),
        out_shape=jax.ShapeDtypeStruct((b, s, d), F32),
        compiler_params=pltpu.CompilerParams(
            dimension_semantics=("arbitrary", "arbitrary"), vmem_limit_bytes=VMEM_LIMIT_BYTES),
        name="combine",
    )(dest, rows, gates, x1, gpost, gate)


def _rope_tables(seq_len):
    rows = seq_len // GRID_W
    row_idx = jnp.repeat(jnp.arange(rows, dtype=F32), GRID_W)
    col_idx = jnp.tile(jnp.arange(GRID_W, dtype=F32), rows)
    half = HEAD_DIM // 2
    inv_freq = 1.0 / (ROPE_THETA ** (jnp.arange(0, half, 2, dtype=F32) / half))
    ang = jnp.concatenate([row_idx[:, None] * inv_freq, col_idx[:, None] * inv_freq], axis=-1)
    cos = jnp.repeat(jnp.cos(ang), 2, axis=-1)
    sin = jnp.sin(ang)
    sin_signed = jnp.stack([-sin, sin], axis=-1).reshape(seq_len, HEAD_DIM)
    return cos, sin_signed


def _dft_tables(n):
    idx = np.arange(n, dtype=np.int64)
    ang = 2.0 * np.pi * ((idx[:, None] * idx[None, :]) % n).astype(np.float64) / n
    scale = 1.0 / np.sqrt(n)
    return np.cos(ang) * scale, np.sin(ang) * scale


def _topk_kernel(logit_ref, tri_ref, gate_ref, idx_ref, rank_ref, count_ref, base):
    step = pl.program_id(0)

    @pl.when(step == 0)
    def _():
        base[...] = jnp.zeros(base.shape, F32)

    lg = logit_ref[...]
    lane = lax.broadcasted_iota(jnp.int32, lg.shape, 1)
    v = jnp.where(lane < N_EXPERTS, lg, -jnp.inf)
    vals, hits = [], []
    for _ in range(TOP_K):
        m = jnp.max(v, axis=-1, keepdims=True)
        first = jnp.min(jnp.where(v == m, lane, ROUTER_PAD), axis=-1, keepdims=True)
        hit = lane == first
        vals.append(m)
        hits.append(hit)
        v = jnp.where(hit, -jnp.inf, v)
    ex = [jnp.exp(val - vals[0]) for val in vals]
    denom = ex[0]
    for e_k in ex[1:]:
        denom = denom + e_k
    chosen = hits[0]
    for hit in hits[1:]:
        chosen = jnp.logical_or(chosen, hit)
    onehot = chosen.astype(F32)
    before = jnp.dot(tri_ref[...], onehot.astype(BF16), preferred_element_type=F32) + base[...]
    lane_f = lane.astype(F32)
    gate_out = jnp.zeros(lg.shape, F32)
    idx_out = jnp.zeros(lg.shape, F32)
    rank_out = jnp.zeros(lg.shape, F32)
    for kk in range(TOP_K):
        sel = lane == kk
        gate_out = jnp.where(sel, ex[kk] / denom, gate_out)
        idx_out = jnp.where(sel, jnp.sum(jnp.where(hits[kk], lane_f, 0.0), axis=-1, keepdims=True), idx_out)
        rank_out = jnp.where(sel, jnp.sum(jnp.where(hits[kk], before, 0.0), axis=-1, keepdims=True), rank_out)
    gate_ref[...] = gate_out
    idx_ref[...] = idx_out.astype(jnp.int32)
    rank_ref[...] = rank_out.astype(jnp.int32)
    base[...] = base[...] + jnp.sum(onehot, axis=0, keepdims=True)
    count_ref[...] = base[...].astype(jnp.int32)


def _topk(logits):
    n_tok = logits.shape[0]
    tb = ROUTE_TB
    tri = jnp.asarray(np.tril(np.ones((tb, tb), np.float32), -1), dtype=BF16)
    blk = pl.BlockSpec((tb, ROUTER_PAD), lambda i: (i, 0))
    return pl.pallas_call(
        _topk_kernel,
        grid=(n_tok // tb,),
        in_specs=[blk, pl.BlockSpec((tb, tb), lambda i: (0, 0))],
        out_specs=[blk, blk, blk, pl.BlockSpec((1, ROUTER_PAD), lambda i: (0, 0))],
        out_shape=[
            jax.ShapeDtypeStruct((n_tok, ROUTER_PAD), F32),
            jax.ShapeDtypeStruct((n_tok, ROUTER_PAD), jnp.int32),
            jax.ShapeDtypeStruct((n_tok, ROUTER_PAD), jnp.int32),
            jax.ShapeDtypeStruct((1, ROUTER_PAD), jnp.int32),
        ],
        scratch_shapes=[pltpu.VMEM((1, ROUTER_PAD), F32)],
        compiler_params=pltpu.CompilerParams(
            dimension_semantics=("arbitrary",), vmem_limit_bytes=VMEM_LIMIT_BYTES),
        name="topk",
    )(logits, tri)


def _route(logits, n_tok):
    gates, top_idx, rank, counts = _topk(logits)
    gates = gates[:, :TOP_K]
    flat_e = top_idx[:, :TOP_K].reshape(-1)
    rank = rank[:, :TOP_K].reshape(-1)
    counts = counts[0, :N_EXPERTS]
    tiles = (counts + MOE_TM - 1) // MOE_TM
    padded = tiles * MOE_TM
    pad_end = jnp.cumsum(padded)
    pad_start = pad_end - padded
    dest = pad_start[flat_e] + rank
    n_rows = n_tok * TOP_K + N_EXPERTS * MOE_TM
    fill_lo = (pad_start + counts).astype(jnp.int32)
    fill_hi = pad_end.astype(jnp.int32)
    used_tiles = jnp.sum(tiles).astype(jnp.int32).reshape(1)
    n_items = N_EXPERTS + n_rows // MOE_RS
    items_per_e = (tiles + MOE_R - 1) // MOE_R
    item_end = jnp.cumsum(items_per_e)
    item_start = item_end - items_per_e
    w = jnp.arange(n_items, dtype=jnp.int32)
    total = item_end[-1]
    e_of_w = jnp.minimum(jnp.searchsorted(item_end, w, side='right'), N_EXPERTS - 1).astype(jnp.int32)
    local = w - item_start[e_of_w]
    active = w < total
    last_e = e_of_w[jnp.maximum(total - 1, 0)]
    item_e = jnp.where(active, e_of_w, last_e).astype(jnp.int32)
    item_nt = jnp.where(active, jnp.minimum(MOE_R, tiles[e_of_w] - local * MOE_R), 0).astype(jnp.int32)
    n_tiles = n_rows // MOE_TM
    fill_tile0 = jnp.sum(tiles) + (w - total) * MOE_R
    item_nz = jnp.where(active, 0, jnp.clip(n_tiles - fill_tile0, 0, MOE_R)).astype(jnp.int32)
    fill_row0 = jnp.minimum(fill_tile0, n_tiles - 1) * MOE_TM
    item_row0 = jnp.where(active, pad_start[e_of_w] + local * MOE_RS, fill_row0).astype(jnp.int32)
    fill = (fill_lo, fill_hi, used_tiles)
    items = tuple(jnp.pad(a, (0, 1)) for a in (item_e, item_row0, item_nt, item_nz))
    return gates, dest.astype(jnp.int32), fill, items, n_rows, n_items


def kernel(x, c, w_ada, b_ada, g_pre_mix, w_in, w_fourier, q_norm_g, k_norm_g, g_fourier_out, g_attn_out,
           w_out, g_post_mix, g_pre_ffn, w_router, b_router, w_gate, b_gate, w_up, b_up, w_down, b_down,
           g_post_ffn):
    b, s, d = x.shape
    n_tok = b * s
    depth = w_ada.shape[0]
    cos, sin_signed = _rope_tables(s)
    cs_np, ss_np = _dft_tables(s)
    cs = jnp.asarray(cs_np, dtype=BF16)
    ss = jnp.asarray(ss_np, dtype=BF16)
    cc_np, sc_np = _dft_tables(FOURIER_GROUP_DIM)
    cc = jnp.asarray(np.concatenate([cc_np, -sc_np], axis=0), dtype=BF16)
    c_pad = jnp.pad(c, ((0, 8 - b), (0, 0)))
    row2 = lambda a: a.reshape(1, -1)

    for l in range(depth):
        mod = _ada(c_pad, w_ada[l], row2(b_ada[l]))[:b].reshape(b, N_MOD, 1, d)
        shift_m, scale_m, gate_m = mod[:, 0], mod[:, 1], mod[:, 2]
        shift_f, scale_f, gate_f = mod[:, 3], mod[:, 4], mod[:, 5]

        f, q, k, v = _inproj(
            x, row2(g_pre_mix[l]), shift_m, scale_m, w_in[l].astype(BF16),
            row2(q_norm_g[l] * (HEAD_DIM ** -0.5 * LOG2_E)), row2(k_norm_g[l]), cos, sin_signed)
        fo = _fourier(f, cs, ss, cc, w_fourier[l], row2(g_fourier_out[l]))
        ao = _attn(q, k, v, row2(g_attn_out[l]))

        wr = jnp.pad(w_router[l], ((0, 0), (0, ROUTER_PAD - N_EXPERTS)))
        wr_hi = wr.astype(BF16)
        wr_lo = (wr - wr_hi.astype(F32)).astype(BF16)
        br = jnp.pad(b_router[l], (0, ROUTER_PAD - N_EXPERTS)).reshape(1, ROUTER_PAD)
        x1, h2p, logits = _outproj(
            fo, ao, w_out[l].astype(BF16), x, row2(g_post_mix[l]), gate_m, row2(g_pre_ffn[l]),
            shift_f, scale_f, wr_hi, wr_lo, br)

        gates, dest, fill, items, n_rows, n_items = _route(logits.reshape(n_tok, ROUTER_PAD), n_tok)
        xs = _dispatch(dest, *fill, h2p, n_rows)
        rows = _experts(
            *items, xs,
            w_gate[l], b_gate[l], w_up[l], b_up[l], w_down[l], b_down[l], n_items)
        x = _combine(dest, rows, gates.reshape(b, s, TOP_K), x1, row2(g_post_ffn[l]), gate_f)
    return x
```

```python
import functools

import numpy as np
import jax
import jax.numpy as jnp
from jax import lax
from jax.experimental import pallas as pl
from jax.experimental.pallas import tpu as pltpu

F32 = jnp.float32
BF16 = jnp.bfloat16
U32 = jnp.uint32

D_MODEL = 2048
GRID_W = 64
FOURIER_WIDTH = 512
FOURIER_GROUP_DIM = 128
FOURIER_GROUPS = 4
ATTN_WIDTH = 1536
HEAD_DIM = 128
N_Q_HEADS = 12
GQA_GROUP = 3
N_KV_HEADS = 4
KV_WIDTH = 512
IN_WIDTH = 3072
ROPE_THETA = 10000.0
N_EXPERTS = 32
TOP_K = 4
D_FF = 2048
SWIGLU_LIMIT = 7.0
SWIGLU_ALPHA = 1.702
N_MOD = 6
EPS = 1e-6
LOG2_E = 1.4426950408889634

VMEM_LIMIT_BYTES = 56 * 1024 * 1024
LANES = 128

ADA_TN = 1024
INPROJ_TM = 256
FOURIER_TR = 256
ATTN_TQ = 128
ATTN_LOOKAHEAD = 1
OUTPROJ_CHAIN_ROWS = 256
OUTPROJ_CHAINS = 2
OUTPROJ_TM = OUTPROJ_CHAIN_ROWS * OUTPROJ_CHAINS
ROUTER_PAD = LANES
ROUTE_TB = 512
MOE_TM = 256
MOE_R = 5
MOE_RS = MOE_TM * MOE_R
MOE_TF = 256
MOE_N_FF = D_FF // MOE_TF
MOE_GROUPS = (5, 4, 2, 1)
DISPATCH_TT = 512
COMBINE_TT = 128
PACK_HALF = D_MODEL // 2
PACK_SLABS = PACK_HALF // LANES


def _rms(x, g):
    return x * lax.rsqrt(jnp.mean(x * x, axis=-1, keepdims=True) + EPS) * g


def _ada_kernel(c_ref, w_ref, b_ref, o_ref):
    c = c_ref[...]
    s = (c * jax.nn.sigmoid(c)).astype(BF16)
    o_ref[...] = jnp.dot(s, w_ref[...].astype(BF16), preferred_element_type=F32) + b_ref[...]


def _ada(c_pad, w_ada, b_ada):
    m, d = c_pad.shape
    n = w_ada.shape[1]
    return pl.pallas_call(
        _ada_kernel,
        grid=(n // ADA_TN,),
        in_specs=[
            pl.BlockSpec((m, d), lambda j: (0, 0)),
            pl.BlockSpec((d, ADA_TN), lambda j: (0, j)),
            pl.BlockSpec((1, ADA_TN), lambda j: (0, j)),
        ],
        out_specs=pl.BlockSpec((m, ADA_TN), lambda j: (0, j)),
        out_shape=jax.ShapeDtypeStruct((m, n), F32),
        compiler_params=pltpu.CompilerParams(
            dimension_semantics=("arbitrary",), vmem_limit_bytes=VMEM_LIMIT_BYTES),
        name="ada",
    )(c_pad, w_ada, b_ada)


def _rope(p, cos, sin_signed, even_lane):
    partner = jnp.where(even_lane, pltpu.roll(p, LANES - 1, 1), pltpu.roll(p, 1, 1))
    return p * cos + partner * sin_signed


def _inproj_kernel(x_ref, g_ref, shift_ref, scale_ref, w_ref, qg_ref, kg_ref, cos_ref, sin_ref,
                   f_ref, q_ref, k_ref, v_ref):
    x = x_ref[...]
    h = _rms(x, g_ref[...]) * (1.0 + scale_ref[...]) + shift_ref[...]
    hb = h.astype(BF16)
    cos = cos_ref[...]
    sin = sin_ref[...]
    even_lane = (lax.broadcasted_iota(jnp.int32, cos.shape, 1) % 2) == 0
    chunk = 512
    heads_per_chunk = chunk // HEAD_DIM
    for ci in range(IN_WIDTH // chunk):
        p = jnp.dot(hb, w_ref[:, ci * chunk:(ci + 1) * chunk], preferred_element_type=F32)
        if ci == 0:
            f_ref[...] = p.astype(BF16)
        elif ci == 5:
            v_ref[...] = p.astype(BF16)
        else:
            gain = kg_ref[...] if ci == 4 else qg_ref[...]
            for hh in range(heads_per_chunk):
                ph = p[:, hh * HEAD_DIM:(hh + 1) * HEAD_DIM]
                ph = _rope(_rms(ph, gain), cos, sin, even_lane).astype(BF16)
                if ci == 4:
                    k_ref[:, hh * HEAD_DIM:(hh + 1) * HEAD_DIM] = ph
                else:
                    c0 = (ci - 1) * chunk + hh * HEAD_DIM
                    q_ref[:, c0:c0 + HEAD_DIM] = ph


def _inproj(x, g, shift, scale, w_in_b, qg, kg, cos, sin):
    b, s, d = x.shape
    tm = INPROJ_TM
    row = lambda bi, i: (bi, i, 0)
    per_batch = lambda bi, i: (bi, 0, 0)
    const2 = lambda bi, i: (0, 0)
    return pl.pallas_call(
        _inproj_kernel,
        grid=(b, s // tm),
        in_specs=[
            pl.BlockSpec((None, tm, d), row),
            pl.BlockSpec((1, d), const2),
            pl.BlockSpec((None, 1, d), per_batch),
            pl.BlockSpec((None, 1, d), per_batch),
            pl.BlockSpec((d, IN_WIDTH), const2),
            pl.BlockSpec((1, HEAD_DIM), const2),
            pl.BlockSpec((1, HEAD_DIM), const2),
            pl.BlockSpec((tm, HEAD_DIM), lambda bi, i: (i, 0)),
            pl.BlockSpec((tm, HEAD_DIM), lambda bi, i: (i, 0)),
        ],
        out_specs=[
            pl.BlockSpec((None, tm, FOURIER_WIDTH), row),
            pl.BlockSpec((None, tm, ATTN_WIDTH), row),
            pl.BlockSpec((None, tm, KV_WIDTH), row),
            pl.BlockSpec((None, tm, KV_WIDTH), row),
        ],
        out_shape=[
            jax.ShapeDtypeStruct((b, s, FOURIER_WIDTH), BF16),
            jax.ShapeDtypeStruct((b, s, ATTN_WIDTH), BF16),
            jax.ShapeDtypeStruct((b, s, KV_WIDTH), BF16),
            jax.ShapeDtypeStruct((b, s, KV_WIDTH), BF16),
        ],
        compiler_params=pltpu.CompilerParams(
            dimension_semantics=("arbitrary", "arbitrary"), vmem_limit_bytes=VMEM_LIMIT_BYTES),
        name="inproj",
    )(x, g, shift, scale, w_in_b, qg, kg, cos, sin)


def _fourier_kernel(cs_ref, ss_ref, f_ref, cc_ref, w_ref, g_ref, o_ref):
    f = f_ref[...]
    a = jnp.dot(cs_ref[...], f, preferred_element_type=F32)
    bm = jnp.dot(ss_ref[...], f, preferred_element_type=F32)
    cc = cc_ref[...]
    outs = []
    for gi in range(FOURIER_GROUPS):
        sl = slice(gi * FOURIER_GROUP_DIM, (gi + 1) * FOURIER_GROUP_DIM)
        ab = jnp.concatenate([a[:, sl], bm[:, sl]], axis=1).astype(BF16)
        fm = jnp.dot(ab, cc, preferred_element_type=F32)
        outs.append(jnp.dot(fm.astype(BF16), w_ref[gi].astype(BF16), preferred_element_type=F32))
    fo = jnp.concatenate(outs, axis=1)
    o_ref[...] = _rms(fo, g_ref[...]).astype(BF16)


def _fourier(f, cs, ss, cc, w_fourier, g):
    b, s, fw = f.shape
    tr = FOURIER_TR
    return pl.pallas_call(
        _fourier_kernel,
        grid=(b, s // tr),
        in_specs=[
            pl.BlockSpec((tr, s), lambda bi, i: (i, 0)),
            pl.BlockSpec((tr, s), lambda bi, i: (i, 0)),
            pl.BlockSpec((None, s, fw), lambda bi, i: (bi, 0, 0)),
            pl.BlockSpec((2 * FOURIER_GROUP_DIM, FOURIER_GROUP_DIM), lambda bi, i: (0, 0)),
            pl.BlockSpec((FOURIER_GROUPS, FOURIER_GROUP_DIM, FOURIER_GROUP_DIM), lambda bi, i: (0, 0, 0)),
            pl.BlockSpec((1, fw), lambda bi, i: (0, 0)),
        ],
        out_specs=pl.BlockSpec((None, tr, fw), lambda bi, i: (bi, i, 0)),
        out_shape=jax.ShapeDtypeStruct((b, s, fw), BF16),
        compiler_params=pltpu.CompilerParams(
            dimension_semantics=("arbitrary", "arbitrary"), vmem_limit_bytes=VMEM_LIMIT_BYTES),
        name="fourier",
    )(cs, ss, f, cc, w_fourier, g)


def _attn_kernel(q_ref, k_ref, v_ref, g_ref, o_ref, ao_ref):
    tq = q_ref.shape[0]

    def scores(kv):
        kh = k_ref[:, kv * HEAD_DIM:(kv + 1) * HEAD_DIM]
        c0 = kv * GQA_GROUP * HEAD_DIM
        qh = jnp.concatenate(
            [q_ref[:, c0 + gi * HEAD_DIM:c0 + (gi + 1) * HEAD_DIM] for gi in range(GQA_GROUP)], axis=0)
        return lax.dot_general(qh, kh, (((1,), (1,)), ((), ())), preferred_element_type=F32)

    def attend(kv, sc):
        vh = v_ref[:, kv * HEAD_DIM:(kv + 1) * HEAD_DIM]
        c0 = kv * GQA_GROUP * HEAD_DIM
        m = jnp.max(sc, axis=-1, keepdims=True)
        p = jnp.exp2(sc - m)
        l = jnp.sum(p, axis=-1, keepdims=True)
        o = jnp.dot(p.astype(BF16), vh, preferred_element_type=F32) / l
        for gi in range(GQA_GROUP):
            ao_ref[:, c0 + gi * HEAD_DIM:c0 + (gi + 1) * HEAD_DIM] = o[gi * tq:(gi + 1) * tq]

    pending = [scores(kv) for kv in range(ATTN_LOOKAHEAD)]
    for kv in range(N_KV_HEADS):
        if kv + ATTN_LOOKAHEAD < N_KV_HEADS:
            pending.append(scores(kv + ATTN_LOOKAHEAD))
        attend(kv, pending.pop(0))
    o_ref[...] = _rms(ao_ref[...], g_ref[...]).astype(BF16)


def _attn(q, k, v, g):
    b, s, _ = q.shape
    tq = ATTN_TQ
    return pl.pallas_call(
        _attn_kernel,
        grid=(b, s // tq),
        in_specs=[
            pl.BlockSpec((None, tq, ATTN_WIDTH), lambda bi, i: (bi, i, 0)),
            pl.BlockSpec((None, s, KV_WIDTH), lambda bi, i: (bi, 0, 0)),
            pl.BlockSpec((None, s, KV_WIDTH), lambda bi, i: (bi, 0, 0)),
            pl.BlockSpec((1, ATTN_WIDTH), lambda bi, i: (0, 0)),
        ],
        out_specs=pl.BlockSpec((None, tq, ATTN_WIDTH), lambda bi, i: (bi, i, 0)),
        out_shape=jax.ShapeDtypeStruct((b, s, ATTN_WIDTH), BF16),
        scratch_shapes=[pltpu.VMEM((tq, ATTN_WIDTH), F32)],
        compiler_params=pltpu.CompilerParams(
            dimension_semantics=("arbitrary", "arbitrary"), vmem_limit_bytes=VMEM_LIMIT_BYTES),
        name="attn",
    )(q, k, v, g)


def _split_bf16(a):
    hi = a.astype(BF16)
    lo = (a - hi.astype(F32)).astype(BF16)
    return hi, lo


def _pack_rows(val, dst_ref):
    n = val.shape[0]
    bits = lax.bitcast_convert_type(val.astype(BF16).astype(F32), U32)
    packed = bits[:, :PACK_HALF] | (bits[:, PACK_HALF:] >> 16)
    for a in range(PACK_SLABS):
        dst_ref[pl.ds(a, n, stride=PACK_SLABS), :] = packed[:, a * LANES:(a + 1) * LANES]


def _unpack_slab(src_ref, a, n):
    word = src_ref[pl.ds(a, n, stride=PACK_SLABS), :]
    hi = lax.bitcast_convert_type(word & jnp.uint32(0xFFFF0000), F32)
    lo = lax.bitcast_convert_type(word << 16, F32)
    return hi, lo


def _outproj_kernel(fo_ref, ao_ref, w_ref, x_ref, gpost_ref, gate_ref, gpre_ref, shift_ref, scale_ref,
                    wr_hi_ref, wr_lo_ref, br_ref, x1_ref, h2p_ref, logit_ref):
    chain_rows = [pl.ds(c * OUTPROJ_CHAIN_ROWS, OUTPROJ_CHAIN_ROWS) for c in range(OUTPROJ_CHAINS)]
    mixes = []
    for rows in chain_rows:
        mix = jnp.dot(fo_ref[rows, :], w_ref[:FOURIER_WIDTH, :], preferred_element_type=F32)
        mixes.append(mix + jnp.dot(ao_ref[rows, :], w_ref[FOURIER_WIDTH:, :], preferred_element_type=F32))
    for c, (rows, mix) in enumerate(zip(chain_rows, mixes)):
        x1 = x_ref[rows, :] + gate_ref[...] * _rms(mix, gpost_ref[...])
        x1_ref[rows, :] = x1
        h2 = _rms(x1, gpre_ref[...]) * (1.0 + scale_ref[...]) + shift_ref[...]
        _pack_rows(h2, h2p_ref.at[pl.ds(c * OUTPROJ_CHAIN_ROWS * PACK_SLABS, OUTPROJ_CHAIN_ROWS * PACK_SLABS)])
        h_hi, h_lo = _split_bf16(h2)
        lg = jnp.dot(h_hi, wr_hi_ref[...], preferred_element_type=F32)
        lg = lg + jnp.dot(h_hi, wr_lo_ref[...], preferred_element_type=F32)
        lg = lg + jnp.dot(h_lo, wr_hi_ref[...], preferred_element_type=F32)
        logit_ref[rows, :] = lg + br_ref[...]


def _outproj(fo, ao, w_out_b, x, gpost, gate, gpre, shift, scale, wr_hi, wr_lo, br):
    b, s, d = x.shape
    tm = OUTPROJ_TM
    row = lambda bi, i: (bi, i, 0)
    per_batch = lambda bi, i: (bi, 0, 0)
    const2 = lambda bi, i: (0, 0)
    return pl.pallas_call(
        _outproj_kernel,
        grid=(b, s // tm),
        in_specs=[
            pl.BlockSpec((None, tm, FOURIER_WIDTH), row),
            pl.BlockSpec((None, tm, ATTN_WIDTH), row),
            pl.BlockSpec((d, d), const2),
            pl.BlockSpec((None, tm, d), row),
            pl.BlockSpec((1, d), const2),
            pl.BlockSpec((None, 1, d), per_batch),
            pl.BlockSpec((1, d), const2),
            pl.BlockSpec((None, 1, d), per_batch),
            pl.BlockSpec((None, 1, d), per_batch),
            pl.BlockSpec((d, ROUTER_PAD), const2),
            pl.BlockSpec((d, ROUTER_PAD), const2),
            pl.BlockSpec((1, ROUTER_PAD), const2),
        ],
        out_specs=[
            pl.BlockSpec((None, tm, d), row),
            pl.BlockSpec((tm * PACK_SLABS, LANES), lambda bi, i: (bi * (s // tm) + i, 0)),
            pl.BlockSpec((None, tm, ROUTER_PAD), row),
        ],
        out_shape=[
            jax.ShapeDtypeStruct((b, s, d), F32),
            jax.ShapeDtypeStruct((b * s * PACK_SLABS, LANES), U32),
            jax.ShapeDtypeStruct((b, s, ROUTER_PAD), F32),
        ],
        compiler_params=pltpu.CompilerParams(
            dimension_semantics=("arbitrary", "arbitrary"), vmem_limit_bytes=VMEM_LIMIT_BYTES),
        name="outproj",
    )(fo, ao, w_out_b, x, gpost, gate, gpre, shift, scale, wr_hi, wr_lo, br)


def _tile_rows(row, n_rows=1):
    return pl.ds(pl.multiple_of(row * PACK_SLABS, PACK_SLABS), n_rows * PACK_SLABS)


def _dispatch_kernel(dest_ref, fill_lo_ref, fill_hi_ref, used_tiles_ref, h_ref, xs_hbm, zeros, sem, fill_sem):
    step = pl.program_id(0)
    base = step * (DISPATCH_TT * TOP_K)

    def row_copy(i, kk):
        return pltpu.make_async_copy(
            h_ref.at[_tile_rows(i)], xs_hbm.at[_tile_rows(dest_ref[base + i * TOP_K + kk])], sem)

    def start(i, c):
        for kk in range(TOP_K):
            row_copy(i, kk).start(priority=kk % 2)
        return c

    def wait(i, c):
        for kk in range(TOP_K):
            row_copy(i, kk).wait()
        return c

    lax.fori_loop(0, DISPATCH_TT, start, 0, unroll=4)

    @pl.when(step == 0)
    def _fill():
        zeros[...] = jnp.zeros(zeros.shape, U32)
        n_tiles = xs_hbm.shape[0] // (MOE_TM * PACK_SLABS)

        def pad_chunks(e, fn):
            lo = fill_lo_ref[e]
            length = fill_hi_ref[e] - lo
            for bit in (128, 64, 32, 16, 8, 4, 2, 1):
                @pl.when((length & bit) != 0)
                def _():
                    row = lo + (length & ~(2 * bit - 1))
                    fn(pltpu.make_async_copy(
                        zeros.at[pl.ds(0, bit * PACK_SLABS)], xs_hbm.at[_tile_rows(row, bit)], fill_sem))

        def tail_copy(t):
            return pltpu.make_async_copy(zeros, xs_hbm.at[_tile_rows(t * MOE_TM, MOE_TM)], fill_sem)

        def start_e(e, c):
            pad_chunks(e, lambda cp: cp.start())
            return c

        def wait_e(e, c):
            pad_chunks(e, lambda cp: cp.wait())
            return c

        def start_t(t, c):
            tail_copy(t).start()
            return c

        def wait_t(t, c):
            tail_copy(t).wait()
            return c

        lax.fori_loop(0, N_EXPERTS, start_e, 0)
        lax.fori_loop(used_tiles_ref[0], n_tiles, start_t, 0)
        lax.fori_loop(0, N_EXPERTS, wait_e, 0)
        lax.fori_loop(used_tiles_ref[0], n_tiles, wait_t, 0)

    lax.fori_loop(0, DISPATCH_TT, wait, 0, unroll=4)


def _dispatch(dest, fill_lo, fill_hi, used_tiles, h2p, n_rows):
    n_tok = h2p.shape[0] // PACK_SLABS
    return pl.pallas_call(
        _dispatch_kernel,
        grid_spec=pltpu.PrefetchScalarGridSpec(
            num_scalar_prefetch=4,
            grid=(n_tok // DISPATCH_TT,),
            in_specs=[pl.BlockSpec((DISPATCH_TT * PACK_SLABS, LANES), lambda i, *_: (i, 0))],
            out_specs=pl.BlockSpec(memory_space=pl.ANY),
            scratch_shapes=[
                pltpu.VMEM((MOE_TM * PACK_SLABS, LANES), U32),
                pltpu.SemaphoreType.DMA,
                pltpu.SemaphoreType.DMA,
            ],
        ),
        out_shape=jax.ShapeDtypeStruct((n_rows * PACK_SLABS, LANES), U32),
        compiler_params=pltpu.CompilerParams(
            dimension_semantics=("arbitrary",), vmem_limit_bytes=VMEM_LIMIT_BYTES),
        name="dispatch",
    )(dest, fill_lo, fill_hi, used_tiles, h2p)


def _experts_kernel(item_e_ref, item_row0_ref, item_nt_ref, item_nz_ref,
                    xs_hbm, wg_hbm, wu_hbm, wd_hbm, bg_ref, bu_ref, bd_ref, out_hbm,
                    stage, x_b, acc, wf_g0, wf_u0, wf_d0, wf_g1, wf_u1, wf_d1,
                    wb_g0, wb_u0, wb_d0, wb_g1, wb_u1, wb_d1, x_sems, o_sems, w_sems):
    w = pl.program_id(0)
    nt = item_nt_ref[w]
    nz = item_nz_ref[w]
    e = item_e_ref[w]
    row0 = item_row0_ref[w]
    w_prev = jnp.maximum(w - 1, 0)
    row0_prev, nt_prev = item_row0_ref[w_prev], item_nt_ref[w_prev]
    e_next, row0_next, nt_next = item_e_ref[w + 1], item_row0_ref[w + 1], item_nt_ref[w + 1]
    has_next = nt_next > 0
    slot = w & 1
    wf = ((wf_g0, wf_u0, wf_d0), (wf_g1, wf_u1, wf_d1))
    wb = ((wb_g0, wb_u0, wb_d0), (wb_g1, wb_u1, wb_d1))

    def w_copies(ee, j, ws):
        col = pl.ds(pl.multiple_of(j * MOE_TF, MOE_TF), MOE_TF)
        f_g, f_u, f_d = wf[ws]
        return (pltpu.make_async_copy(wg_hbm.at[ee, :, col], f_g, w_sems.at[ws, 0]),
                pltpu.make_async_copy(wu_hbm.at[ee, :, col], f_u, w_sems.at[ws, 1]),
                pltpu.make_async_copy(wd_hbm.at[ee, col, :], f_d, w_sems.at[ws, 2]))

    def w_start(ee, j, ws):
        for cp in w_copies(ee, j, ws):
            cp.start()

    def w_wait(ee, j, ws):
        for cp in w_copies(ee, j, ws):
            cp.wait()

    def w_cast(ws):
        for f_ref, b_ref in zip(wf[ws], wb[ws]):
            b_ref[...] = f_ref[...].astype(BF16)

    def x_copy(r0, t, ss):
        return pltpu.make_async_copy(
            xs_hbm.at[_tile_rows(r0 + t * MOE_TM, MOE_TM)], stage.at[ss, t], x_sems.at[ss, t])

    def out_copy(r0, t, ss):
        return pltpu.make_async_copy(
            stage.at[ss, t], out_hbm.at[_tile_rows(r0 + t * MOE_TM, MOE_TM)], o_sems.at[ss, t])

    def for_tiles(n, fn):
        for t in range(MOE_R):
            @pl.when(t < n)
            def _():
                fn(t)

    def gate_up(t, j, ws):
        b_g, b_u, _ = wb[ws]
        x = x_b[t]
        g = jnp.dot(x, b_g[...], preferred_element_type=F32) + bg_ref[j]
        u = jnp.dot(x, b_u[...], preferred_element_type=F32) + bu_ref[j]
        g = jnp.minimum(g, SWIGLU_LIMIT)
        u = jnp.clip(u, -SWIGLU_LIMIT, SWIGLU_LIMIT)
        return ((u + 1.0) * (g * jax.nn.sigmoid(SWIGLU_ALPHA * g))).astype(BF16)

    def down(t, hidden, ws):
        acc[t] += jnp.dot(hidden, wb[ws][2][...], preferred_element_type=F32)

    def ff_step(j, ws):
        @pl.when(j + 2 < MOE_N_FF)
        def _():
            w_start(e, j + 2, ws)

        @pl.when(jnp.logical_and(j + 2 >= MOE_N_FF, has_next))
        def _():
            w_start(e_next, j + 2 - MOE_N_FF, ws)

        @pl.when(j + 1 < MOE_N_FF)
        def _():
            w_wait(e, j + 1, 1 - ws)

        @pl.when(jnp.logical_and(j + 1 >= MOE_N_FF, has_next))
        def _():
            w_wait(e_next, 0, 1 - ws)

        w_cast(1 - ws)

        done = 0
        for size in MOE_GROUPS:
            def group(i, c, size=size, done=done):
                ts = [done + size * i + k for k in range(size)]
                hs = [gate_up(t, j, ws) for t in ts]
                for t, h in zip(ts, hs):
                    down(t, h, ws)
                return c

            n_group = (nt - done) // size
            lax.fori_loop(0, n_group, group, 0)
            done = done + n_group * size

    @pl.when(nt > 0)
    def _active():
        @pl.when(w == 0)
        def _prologue():
            w_start(e, 0, 0)
            w_start(e, 1, 1)
            for_tiles(nt, lambda t: x_copy(row0, t, 0).start())
            w_wait(e, 0, 0)
            w_cast(0)

        def load_tile(t):
            x_copy(row0, t, slot).wait()
            for a in range(PACK_SLABS):
                hi, lo = _unpack_slab(stage.at[slot, t], a, MOE_TM)
                x_b[t, :, a * LANES:(a + 1) * LANES] = hi.astype(BF16)
                x_b[t, :, PACK_HALF + a * LANES:PACK_HALF + (a + 1) * LANES] = lo.astype(BF16)
            acc[t] = jnp.broadcast_to(bd_ref[...], acc.shape[1:])

        for_tiles(nt, load_tile)

        def ff_pair(jj, c):
            ff_step(2 * jj, 0)

            @pl.when(jj == 0)
            def _():
                @pl.when(w > 0)
                def _():
                    for_tiles(nt_prev, lambda t: out_copy(row0_prev, t, 1 - slot).wait())

                @pl.when(has_next)
                def _():
                    for_tiles(nt_next, lambda t: x_copy(row0_next, t, 1 - slot).start())

            ff_step(2 * jj + 1, 1)
            return c

        lax.fori_loop(0, MOE_N_FF // 2, ff_pair, 0)

        def store_tile(t):
            _pack_rows(acc[t], stage.at[slot, t])
            out_copy(row0, t, slot).start()

        for_tiles(nt, store_tile)

        @pl.when(jnp.logical_not(has_next))
        def _():
            for_tiles(nt, lambda t: out_copy(row0, t, slot).wait())

    @pl.when(nz > 0)
    def _zero_tail():
        def zero_tile(t):
            stage[0, t] = jnp.zeros(stage.shape[2:], U32)
            out_copy(row0, t, 0).start()

        for_tiles(nz, zero_tile)
        for_tiles(nz, lambda t: out_copy(row0, t, 0).wait())


def _experts(item_e, item_row0, item_nt, item_nz, xs, w_gate, b_gate, w_up, b_up, w_down, b_down, n_items):
    d = D_MODEL
    w_shapes = [(d, MOE_TF), (d, MOE_TF), (MOE_TF, d)]
    by_expert = lambda w, ie, r0, nt, nz: (ie[w], 0, 0, 0)
    return pl.pallas_call(
        _experts_kernel,
        grid_spec=pltpu.PrefetchScalarGridSpec(
            num_scalar_prefetch=4,
            grid=(n_items,),
            in_specs=[
                pl.BlockSpec(memory_space=pl.ANY),
                pl.BlockSpec(memory_space=pl.ANY),
                pl.BlockSpec(memory_space=pl.ANY),
                pl.BlockSpec(memory_space=pl.ANY),
                pl.BlockSpec((None, MOE_N_FF, 1, MOE_TF), by_expert),
                pl.BlockSpec((None, MOE_N_FF, 1, MOE_TF), by_expert),
                pl.BlockSpec((None, 1, d), lambda w, ie, r0, nt, nz: (ie[w], 0, 0)),
            ],
            out_specs=pl.BlockSpec(memory_space=pl.ANY),
            scratch_shapes=[
                pltpu.VMEM((2, MOE_R, MOE_TM * PACK_SLABS, LANES), U32),
                pltpu.VMEM((MOE_R, MOE_TM, d), BF16),
                pltpu.VMEM((MOE_R, MOE_TM, d), F32),
                *[pltpu.VMEM(shape, F32) for shape in w_shapes * 2],
                *[pltpu.VMEM(shape, BF16) for shape in w_shapes * 2],
                pltpu.SemaphoreType.DMA((2, MOE_R)),
                pltpu.SemaphoreType.DMA((2, MOE_R)),
                pltpu.SemaphoreType.DMA((2, 3)),
            ],
        ),
        out_shape=jax.ShapeDtypeStruct(xs.shape, U32),
        compiler_params=pltpu.CompilerParams(
            dimension_semantics=("arbitrary",), vmem_limit_bytes=VMEM_LIMIT_BYTES),
        name="experts",
    )(item_e, item_row0, item_nt, item_nz, xs, w_gate, w_up, w_down,
      b_gate.reshape(N_EXPERTS, MOE_N_FF, 1, MOE_TF), b_up.reshape(N_EXPERTS, MOE_N_FF, 1, MOE_TF),
      b_down.reshape(N_EXPERTS, 1, d))


def _combine_kernel(dest_ref, rows_hbm, gates_ref, x1_ref, gpost_ref, gate_ref, o_ref, buf, sems):
    tt = COMBINE_TT
    step = pl.program_id(0) * pl.num_programs(1) + pl.program_id(1)
    n_steps = pl.num_programs(0) * pl.num_programs(1)
    slot = step & 1

    def row_copy(s, ss, i, kk):
        row = dest_ref[(s * tt + i) * TOP_K + kk]
        return pltpu.make_async_copy(rows_hbm.at[_tile_rows(row)], buf.at[ss, kk, _tile_rows(i)], sems.at[ss])

    def start_step(s, ss):
        def start(i, c):
            for kk in range(TOP_K):
                row_copy(s, ss, i, kk).start(priority=kk % 2)
            return c

        lax.fori_loop(0, tt, start, 0, unroll=4)

    @pl.when(step == 0)
    def _():
        start_step(step, slot)

    @pl.when(step + 1 < n_steps)
    def _():
        start_step(step + 1, 1 - slot)

    def wait(i, c):
        for kk in range(TOP_K):
            row_copy(step, slot, i, kk).wait()
        return c

    lax.fori_loop(0, tt, wait, 0, unroll=4)
    gates = gates_ref[...]
    y_hi, y_lo = [], []
    for a in range(PACK_SLABS):
        acc_hi = acc_lo = None
        for kk in range(TOP_K):
            hi, lo = _unpack_slab(buf.at[slot, kk], a, tt)
            gk = gates[:, kk:kk + 1]
            acc_hi = gk * hi if acc_hi is None else acc_hi + gk * hi
            acc_lo = gk * lo if acc_lo is None else acc_lo + gk * lo
        y_hi.append(acc_hi)
        y_lo.append(acc_lo)
    y = jnp.concatenate(y_hi + y_lo, axis=1)
    o_ref[...] = x1_ref[...] + gate_ref[...] * _rms(y, gpost_ref[...])


def _combine(dest, rows, gates, x1, gpost, gate):
    b, s, d = x1.shape
    tt = COMBINE_TT
    row = lambda bi, i, dst: (bi, i, 0)
    return pl.pallas_call(
        _combine_kernel,
        grid_spec=pltpu.PrefetchScalarGridSpec(
            num_scalar_prefetch=1,
            grid=(b, s // tt),
            in_specs=[
                pl.BlockSpec(memory_space=pl.ANY),
                pl.BlockSpec((None, tt, TOP_K), row),
                pl.BlockSpec((None, tt, d), row),
                pl.BlockSpec((1, d), lambda bi, i, dst: (0, 0)),
                pl.BlockSpec((None, 1, d), lambda bi, i, dst: (bi, 0, 0)),
            ],
            out_specs=pl.BlockSpec((None, tt, d), row),
            scratch_shapes=[pltpu.VMEM((2, TOP_K, tt * PACK_SLABS, LANES), U32),
                            pltpu.SemaphoreType.DMA((2,))],
        ),
        out_shape=jax.ShapeDtypeStruct((b, s, d), F32),
        compiler_params=pltpu.CompilerParams(
            dimension_semantics=("arbitrary", "arbitrary"), vmem_limit_bytes=VMEM_LIMIT_BYTES),
        name="combine",
    )(dest, rows, gates, x1, gpost, gate)


def _rope_tables(seq_len):
    rows = seq_len // GRID_W
    row_idx = jnp.repeat(jnp.arange(rows, dtype=F32), GRID_W)
    col_idx = jnp.tile(jnp.arange(GRID_W, dtype=F32), rows)
    half = HEAD_DIM // 2
    inv_freq = 1.0 / (ROPE_THETA ** (jnp.arange(0, half, 2, dtype=F32) / half))
    ang = jnp.concatenate([row_idx[:, None] * inv_freq, col_idx[:, None] * inv_freq], axis=-1)
    cos = jnp.repeat(jnp.cos(ang), 2, axis=-1)
    sin = jnp.sin(ang)
    sin_signed = jnp.stack([-sin, sin], axis=-1).reshape(seq_len, HEAD_DIM)
    return cos, sin_signed


def _dft_tables(n):
    idx = np.arange(n, dtype=np.int64)
    ang = 2.0 * np.pi * ((idx[:, None] * idx[None, :]) % n).astype(np.float64) / n
    scale = 1.0 / np.sqrt(n)
    return np.cos(ang) * scale, np.sin(ang) * scale


def _topk_kernel(logit_ref, tri_ref, gate_ref, idx_ref, rank_ref, count_ref, base):
    step = pl.program_id(0)

    @pl.when(step == 0)
    def _():
        base[...] = jnp.zeros(base.shape, F32)

    lg = logit_ref[...]
    lane = lax.broadcasted_iota(jnp.int32, lg.shape, 1)
    v = jnp.where(lane < N_EXPERTS, lg, -jnp.inf)
    vals, hits = [], []
    for _ in range(TOP_K):
        m = jnp.max(v, axis=-1, keepdims=True)
        first = jnp.min(jnp.where(v == m, lane, ROUTER_PAD), axis=-1, keepdims=True)
        hit = lane == first
        vals.append(m)
        hits.append(hit)
        v = jnp.where(hit, -jnp.inf, v)
    ex = [jnp.exp(val - vals[0]) for val in vals]
    denom = ex[0]
    for e_k in ex[1:]:
        denom = denom + e_k
    chosen = hits[0]
    for hit in hits[1:]:
        chosen = jnp.logical_or(chosen, hit)
    onehot = chosen.astype(F32)
    before = jnp.dot(tri_ref[...], onehot.astype(BF16), preferred_element_type=F32) + base[...]
    lane_f = lane.astype(F32)
    gate_out = jnp.zeros(lg.shape, F32)
    idx_out = jnp.zeros(lg.shape, F32)
    rank_out = jnp.zeros(lg.shape, F32)
    for kk in range(TOP_K):
        sel = lane == kk
        gate_out = jnp.where(sel, ex[kk] / denom, gate_out)
        idx_out = jnp.where(sel, jnp.sum(jnp.where(hits[kk], lane_f, 0.0), axis=-1, keepdims=True), idx_out)
        rank_out = jnp.where(sel, jnp.sum(jnp.where(hits[kk], before, 0.0), axis=-1, keepdims=True), rank_out)
    gate_ref[...] = gate_out
    idx_ref[...] = idx_out.astype(jnp.int32)
    rank_ref[...] = rank_out.astype(jnp.int32)
    base[...] = base[...] + jnp.sum(onehot, axis=0, keepdims=True)
    count_ref[...] = base[...].astype(jnp.int32)


def _topk(logits):
    n_tok = logits.shape[0]
    tb = ROUTE_TB
    tri = jnp.asarray(np.tril(np.ones((tb, tb), np.float32), -1), dtype=BF16)
    blk = pl.BlockSpec((tb, ROUTER_PAD), lambda i: (i, 0))
    return pl.pallas_call(
        _topk_kernel,
        grid=(n_tok // tb,),
        in_specs=[blk, pl.BlockSpec((tb, tb), lambda i: (0, 0))],
        out_specs=[blk, blk, blk, pl.BlockSpec((1, ROUTER_PAD), lambda i: (0, 0))],
        out_shape=[
            jax.ShapeDtypeStruct((n_tok, ROUTER_PAD), F32),
            jax.ShapeDtypeStruct((n_tok, ROUTER_PAD), jnp.int32),
            jax.ShapeDtypeStruct((n_tok, ROUTER_PAD), jnp.int32),
            jax.ShapeDtypeStruct((1, ROUTER_PAD), jnp.int32),
        ],
        scratch_shapes=[pltpu.VMEM((1, ROUTER_PAD), F32)],
        compiler_params=pltpu.CompilerParams(
            dimension_semantics=("arbitrary",), vmem_limit_bytes=VMEM_LIMIT_BYTES),
        name="topk",
    )(logits, tri)


def _plan_kernel(count_ref, e_ref, rank_ref, dest_ref, fill_lo_ref, fill_hi_ref, used_ref,
                 item_e_ref, item_row0_ref, item_nt_ref, item_nz_ref, pad_start, *, n_tiles, n_items):
    def per_expert(e, carry):
        start, w = carry
        count = count_ref[e]
        tiles = (count + (MOE_TM - 1)) // MOE_TM
        pad_start[e] = start
        fill_lo_ref[e] = start + count
        fill_hi_ref[e] = start + tiles * MOE_TM

        def per_item(k, w):
            item_e_ref[w] = e
            item_row0_ref[w] = start + k * MOE_RS
            item_nt_ref[w] = jnp.minimum(MOE_R, tiles - k * MOE_R)
            item_nz_ref[w] = 0
            return w + 1

        w = lax.fori_loop(0, (tiles + (MOE_R - 1)) // MOE_R, per_item, w)
        return start + tiles * MOE_TM, w

    used_rows, total = lax.fori_loop(0, N_EXPERTS, per_expert, (jnp.int32(0), jnp.int32(0)))
    used_tiles = used_rows // MOE_TM
    used_ref[0] = used_tiles
    last_e = item_e_ref[jnp.maximum(total - 1, 0)]

    def per_filler(w, c):
        tile0 = used_tiles + (w - total) * MOE_R
        item_e_ref[w] = last_e
        item_row0_ref[w] = jnp.minimum(tile0, n_tiles - 1) * MOE_TM
        item_nt_ref[w] = 0
        item_nz_ref[w] = jnp.clip(n_tiles - tile0, 0, MOE_R)
        return c

    lax.fori_loop(total, n_items + 1, per_filler, 0)

    e = e_ref[...]
    dest = rank_ref[...]
    for ee in range(N_EXPERTS):
        dest = dest + jnp.where(e == ee, pad_start[ee], 0)
    dest_ref[...] = dest


def _route(logits, n_tok):
    gates, top_idx, rank, counts = _topk(logits)
    n_rows = n_tok * TOP_K + N_EXPERTS * MOE_TM
    n_items = N_EXPERTS + n_rows // MOE_RS
    flat = (n_tok * TOP_K // LANES, LANES)
    smem = pl.BlockSpec(memory_space=pltpu.SMEM)
    vmem = pl.BlockSpec(flat, lambda: (0, 0))
    i32 = lambda n: jax.ShapeDtypeStruct((n,), jnp.int32)
    dest, fill_lo, fill_hi, used_tiles, item_e, item_row0, item_nt, item_nz = pl.pallas_call(
        functools.partial(_plan_kernel, n_tiles=n_rows // MOE_TM, n_items=n_items),
        in_specs=[smem, vmem, vmem],
        out_specs=[vmem] + [smem] * 7,
        out_shape=[jax.ShapeDtypeStruct(flat, jnp.int32), i32(N_EXPERTS), i32(N_EXPERTS), i32(1)]
        + [i32(n_items + 1)] * 4,
        scratch_shapes=[pltpu.SMEM((N_EXPERTS,), jnp.int32)],
        name="plan",
    )(counts.reshape(ROUTER_PAD), top_idx[:, :TOP_K].reshape(flat), rank[:, :TOP_K].reshape(flat))
    fill = (fill_lo, fill_hi, used_tiles)
    items = (item_e, item_row0, item_nt, item_nz)
    return gates[:, :TOP_K], dest.reshape(-1), fill, items, n_rows, n_items


def kernel(x, c, w_ada, b_ada, g_pre_mix, w_in, w_fourier, q_norm_g, k_norm_g, g_fourier_out, g_attn_out,
           w_out, g_post_mix, g_pre_ffn, w_router, b_router, w_gate, b_gate, w_up, b_up, w_down, b_down,
           g_post_ffn):
    b, s, d = x.shape
    n_tok = b * s
    depth = w_ada.shape[0]
    cos, sin_signed = _rope_tables(s)
    cs_np, ss_np = _dft_tables(s)
    cs = jnp.asarray(cs_np, dtype=BF16)
    ss = jnp.asarray(ss_np, dtype=BF16)
    cc_np, sc_np = _dft_tables(FOURIER_GROUP_DIM)
    cc = jnp.asarray(np.concatenate([cc_np, -sc_np], axis=0), dtype=BF16)
    c_pad = jnp.pad(c, ((0, 8 - b), (0, 0)))
    row2 = lambda a: a.reshape(1, -1)

    for l in range(depth):
        mod = _ada(c_pad, w_ada[l], row2(b_ada[l]))[:b].reshape(b, N_MOD, 1, d)
        shift_m, scale_m, gate_m = mod[:, 0], mod[:, 1], mod[:, 2]
        shift_f, scale_f, gate_f = mod[:, 3], mod[:, 4], mod[:, 5]

        f, q, k, v = _inproj(
            x, row2(g_pre_mix[l]), shift_m, scale_m, w_in[l].astype(BF16),
            row2(q_norm_g[l] * (HEAD_DIM ** -0.5 * LOG2_E)), row2(k_norm_g[l]), cos, sin_signed)
        fo = _fourier(f, cs, ss, cc, w_fourier[l], row2(g_fourier_out[l]))
        ao = _attn(q, k, v, row2(g_attn_out[l]))

        wr = jnp.pad(w_router[l], ((0, 0), (0, ROUTER_PAD - N_EXPERTS)))
        wr_hi = wr.astype(BF16)
        wr_lo = (wr - wr_hi.astype(F32)).astype(BF16)
        br = jnp.pad(b_router[l], (0, ROUTER_PAD - N_EXPERTS)).reshape(1, ROUTER_PAD)
        x1, h2p, logits = _outproj(
            fo, ao, w_out[l].astype(BF16), x, row2(g_post_mix[l]), gate_m, row2(g_pre_ffn[l]),
            shift_f, scale_f, wr_hi, wr_lo, br)

        gates, dest, fill, items, n_rows, n_items = _route(logits.reshape(n_tok, ROUTER_PAD), n_tok)
        xs = _dispatch(dest, *fill, h2p, n_rows)
        rows = _experts(
            *items, xs,
            w_gate[l], b_gate[l], w_up[l], b_up[l], w_down[l], b_down[l], n_items)
        x = _combine(dest, rows, gates.reshape(b, s, TOP_K), x1, row2(g_post_ffn[l]), gate_f)
    return x
```

```python
import functools

import numpy as np
import jax
import jax.numpy as jnp
from jax import lax
from jax.experimental import pallas as pl
from jax.experimental.pallas import tpu as pltpu

F32 = jnp.float32
BF16 = jnp.bfloat16
U32 = jnp.uint32

D_MODEL = 2048
GRID_W = 64
FOURIER_WIDTH = 512
FOURIER_GROUP_DIM = 128
FOURIER_GROUPS = 4
ATTN_WIDTH = 1536
HEAD_DIM = 128
N_Q_HEADS = 12
GQA_GROUP = 3
N_KV_HEADS = 4
KV_WIDTH = 512
IN_WIDTH = 3072
ROPE_THETA = 10000.0
N_EXPERTS = 32
TOP_K = 4
D_FF = 2048
SWIGLU_LIMIT = 7.0
SWIGLU_ALPHA = 1.702
N_MOD = 6
EPS = 1e-6
LOG2_E = 1.4426950408889634

VMEM_LIMIT_BYTES = 56 * 1024 * 1024
LANES = 128

ADA_TN = 1024
INPROJ_TM = 256
FOURIER_TR = 256
ATTN_TQ = 128
ATTN_LOOKAHEAD = 1
OUTPROJ_CHAIN_ROWS = 256
OUTPROJ_CHAINS = 2
OUTPROJ_TM = OUTPROJ_CHAIN_ROWS * OUTPROJ_CHAINS
ROUTER_PAD = LANES
ROUTE_TB = 512
MOE_TM = 272
PAD_CHUNKS = tuple(1 << k for k in reversed(range((MOE_TM - 1).bit_length())))
MOE_R = 5
MOE_RS = MOE_TM * MOE_R
MOE_TF = 256
MOE_N_FF = D_FF // MOE_TF
MOE_GROUPS = (5, 4, 2, 1)
DISPATCH_TT = 512
COMBINE_TT = 128
PACK_HALF = D_MODEL // 2
PACK_SLABS = PACK_HALF // LANES


def _rms(x, g):
    return x * lax.rsqrt(jnp.mean(x * x, axis=-1, keepdims=True) + EPS) * g


def _ada_kernel(c_ref, w_ref, b_ref, o_ref):
    c = c_ref[...]
    s = (c * jax.nn.sigmoid(c)).astype(BF16)
    o_ref[...] = jnp.dot(s, w_ref[...].astype(BF16), preferred_element_type=F32) + b_ref[...]


def _ada(c_pad, w_ada, b_ada):
    m, d = c_pad.shape
    n = w_ada.shape[1]
    return pl.pallas_call(
        _ada_kernel,
        grid=(n // ADA_TN,),
        in_specs=[
            pl.BlockSpec((m, d), lambda j: (0, 0)),
            pl.BlockSpec((d, ADA_TN), lambda j: (0, j)),
            pl.BlockSpec((1, ADA_TN), lambda j: (0, j)),
        ],
        out_specs=pl.BlockSpec((m, ADA_TN), lambda j: (0, j)),
        out_shape=jax.ShapeDtypeStruct((m, n), F32),
        compiler_params=pltpu.CompilerParams(
            dimension_semantics=("arbitrary",), vmem_limit_bytes=VMEM_LIMIT_BYTES),
        name="ada",
    )(c_pad, w_ada, b_ada)


def _rope(p, cos, sin_signed, even_lane):
    partner = jnp.where(even_lane, pltpu.roll(p, LANES - 1, 1), pltpu.roll(p, 1, 1))
    return p * cos + partner * sin_signed


def _inproj_kernel(x_ref, g_ref, shift_ref, scale_ref, w_ref, qg_ref, kg_ref, cos_ref, sin_ref,
                   f_ref, q_ref, k_ref, v_ref):
    x = x_ref[...]
    h = _rms(x, g_ref[...]) * (1.0 + scale_ref[...]) + shift_ref[...]
    hb = h.astype(BF16)
    cos = cos_ref[...]
    sin = sin_ref[...]
    even_lane = (lax.broadcasted_iota(jnp.int32, cos.shape, 1) % 2) == 0
    chunk = 512
    heads_per_chunk = chunk // HEAD_DIM
    for ci in range(IN_WIDTH // chunk):
        p = jnp.dot(hb, w_ref[:, ci * chunk:(ci + 1) * chunk], preferred_element_type=F32)
        if ci == 0:
            f_ref[...] = p.astype(BF16)
        elif ci == 5:
            v_ref[...] = p.astype(BF16)
        else:
            gain = kg_ref[...] if ci == 4 else qg_ref[...]
            for hh in range(heads_per_chunk):
                ph = p[:, hh * HEAD_DIM:(hh + 1) * HEAD_DIM]
                ph = _rope(_rms(ph, gain), cos, sin, even_lane).astype(BF16)
                if ci == 4:
                    k_ref[:, hh * HEAD_DIM:(hh + 1) * HEAD_DIM] = ph
                else:
                    c0 = (ci - 1) * chunk + hh * HEAD_DIM
                    q_ref[:, c0:c0 + HEAD_DIM] = ph


def _inproj(x, g, shift, scale, w_in_b, qg, kg, cos, sin):
    b, s, d = x.shape
    tm = INPROJ_TM
    row = lambda bi, i: (bi, i, 0)
    per_batch = lambda bi, i: (bi, 0, 0)
    const2 = lambda bi, i: (0, 0)
    return pl.pallas_call(
        _inproj_kernel,
        grid=(b, s // tm),
        in_specs=[
            pl.BlockSpec((None, tm, d), row),
            pl.BlockSpec((1, d), const2),
            pl.BlockSpec((None, 1, d), per_batch),
            pl.BlockSpec((None, 1, d), per_batch),
            pl.BlockSpec((d, IN_WIDTH), const2),
            pl.BlockSpec((1, HEAD_DIM), const2),
            pl.BlockSpec((1, HEAD_DIM), const2),
            pl.BlockSpec((tm, HEAD_DIM), lambda bi, i: (i, 0)),
            pl.BlockSpec((tm, HEAD_DIM), lambda bi, i: (i, 0)),
        ],
        out_specs=[
            pl.BlockSpec((None, tm, FOURIER_WIDTH), row),
            pl.BlockSpec((None, tm, ATTN_WIDTH), row),
            pl.BlockSpec((None, tm, KV_WIDTH), row),
            pl.BlockSpec((None, tm, KV_WIDTH), row),
        ],
        out_shape=[
            jax.ShapeDtypeStruct((b, s, FOURIER_WIDTH), BF16),
            jax.ShapeDtypeStruct((b, s, ATTN_WIDTH), BF16),
            jax.ShapeDtypeStruct((b, s, KV_WIDTH), BF16),
            jax.ShapeDtypeStruct((b, s, KV_WIDTH), BF16),
        ],
        compiler_params=pltpu.CompilerParams(
            dimension_semantics=("arbitrary", "arbitrary"), vmem_limit_bytes=VMEM_LIMIT_BYTES),
        name="inproj",
    )(x, g, shift, scale, w_in_b, qg, kg, cos, sin)


def _fourier_kernel(cs_ref, ss_ref, f_ref, cc_ref, w_ref, g_ref, o_ref):
    f = f_ref[...]
    a = jnp.dot(cs_ref[...], f, preferred_element_type=F32)
    bm = jnp.dot(ss_ref[...], f, preferred_element_type=F32)
    cc = cc_ref[...]
    outs = []
    for gi in range(FOURIER_GROUPS):
        sl = slice(gi * FOURIER_GROUP_DIM, (gi + 1) * FOURIER_GROUP_DIM)
        ab = jnp.concatenate([a[:, sl], bm[:, sl]], axis=1).astype(BF16)
        fm = jnp.dot(ab, cc, preferred_element_type=F32)
        outs.append(jnp.dot(fm.astype(BF16), w_ref[gi].astype(BF16), preferred_element_type=F32))
    fo = jnp.concatenate(outs, axis=1)
    o_ref[...] = _rms(fo, g_ref[...]).astype(BF16)


def _fourier(f, cs, ss, cc, w_fourier, g):
    b, s, fw = f.shape
    tr = FOURIER_TR
    return pl.pallas_call(
        _fourier_kernel,
        grid=(b, s // tr),
        in_specs=[
            pl.BlockSpec((tr, s), lambda bi, i: (i, 0)),
            pl.BlockSpec((tr, s), lambda bi, i: (i, 0)),
            pl.BlockSpec((None, s, fw), lambda bi, i: (bi, 0, 0)),
            pl.BlockSpec((2 * FOURIER_GROUP_DIM, FOURIER_GROUP_DIM), lambda bi, i: (0, 0)),
            pl.BlockSpec((FOURIER_GROUPS, FOURIER_GROUP_DIM, FOURIER_GROUP_DIM), lambda bi, i: (0, 0, 0)),
            pl.BlockSpec((1, fw), lambda bi, i: (0, 0)),
        ],
        out_specs=pl.BlockSpec((None, tr, fw), lambda bi, i: (bi, i, 0)),
        out_shape=jax.ShapeDtypeStruct((b, s, fw), BF16),
        compiler_params=pltpu.CompilerParams(
            dimension_semantics=("arbitrary", "arbitrary"), vmem_limit_bytes=VMEM_LIMIT_BYTES),
        name="fourier",
    )(cs, ss, f, cc, w_fourier, g)


def _attn_kernel(q_ref, k_ref, v_ref, g_ref, o_ref, ao_ref):
    tq = q_ref.shape[0]

    def scores(kv):
        kh = k_ref[:, kv * HEAD_DIM:(kv + 1) * HEAD_DIM]
        c0 = kv * GQA_GROUP * HEAD_DIM
        qh = jnp.concatenate(
            [q_ref[:, c0 + gi * HEAD_DIM:c0 + (gi + 1) * HEAD_DIM] for gi in range(GQA_GROUP)], axis=0)
        return lax.dot_general(qh, kh, (((1,), (1,)), ((), ())), preferred_element_type=F32)

    def attend(kv, sc):
        vh = v_ref[:, kv * HEAD_DIM:(kv + 1) * HEAD_DIM]
        c0 = kv * GQA_GROUP * HEAD_DIM
        m = jnp.max(sc, axis=-1, keepdims=True)
        p = jnp.exp2(sc - m)
        l = jnp.sum(p, axis=-1, keepdims=True)
        o = jnp.dot(p.astype(BF16), vh, preferred_element_type=F32) / l
        for gi in range(GQA_GROUP):
            ao_ref[:, c0 + gi * HEAD_DIM:c0 + (gi + 1) * HEAD_DIM] = o[gi * tq:(gi + 1) * tq]

    pending = [scores(kv) for kv in range(ATTN_LOOKAHEAD)]
    for kv in range(N_KV_HEADS):
        if kv + ATTN_LOOKAHEAD < N_KV_HEADS:
            pending.append(scores(kv + ATTN_LOOKAHEAD))
        attend(kv, pending.pop(0))
    o_ref[...] = _rms(ao_ref[...], g_ref[...]).astype(BF16)


def _attn(q, k, v, g):
    b, s, _ = q.shape
    tq = ATTN_TQ
    return pl.pallas_call(
        _attn_kernel,
        grid=(b, s // tq),
        in_specs=[
            pl.BlockSpec((None, tq, ATTN_WIDTH), lambda bi, i: (bi, i, 0)),
            pl.BlockSpec((None, s, KV_WIDTH), lambda bi, i: (bi, 0, 0)),
            pl.BlockSpec((None, s, KV_WIDTH), lambda bi, i: (bi, 0, 0)),
            pl.BlockSpec((1, ATTN_WIDTH), lambda bi, i: (0, 0)),
        ],
        out_specs=pl.BlockSpec((None, tq, ATTN_WIDTH), lambda bi, i: (bi, i, 0)),
        out_shape=jax.ShapeDtypeStruct((b, s, ATTN_WIDTH), BF16),
        scratch_shapes=[pltpu.VMEM((tq, ATTN_WIDTH), F32)],
        compiler_params=pltpu.CompilerParams(
            dimension_semantics=("arbitrary", "arbitrary"), vmem_limit_bytes=VMEM_LIMIT_BYTES),
        name="attn",
    )(q, k, v, g)


def _split_bf16(a):
    hi = a.astype(BF16)
    lo = (a - hi.astype(F32)).astype(BF16)
    return hi, lo


def _pack_rows(val, dst_ref):
    n = val.shape[0]
    bits = lax.bitcast_convert_type(val.astype(BF16).astype(F32), U32)
    packed = bits[:, :PACK_HALF] | (bits[:, PACK_HALF:] >> 16)
    for a in range(PACK_SLABS):
        dst_ref[pl.ds(a, n, stride=PACK_SLABS), :] = packed[:, a * LANES:(a + 1) * LANES]


def _unpack_slab(src_ref, a, n):
    word = src_ref[pl.ds(a, n, stride=PACK_SLABS), :]
    hi = lax.bitcast_convert_type(word & jnp.uint32(0xFFFF0000), F32)
    lo = lax.bitcast_convert_type(word << 16, F32)
    return hi, lo


def _outproj_kernel(fo_ref, ao_ref, w_ref, x_ref, gpost_ref, gate_ref, gpre_ref, shift_ref, scale_ref,
                    wr_hi_ref, wr_lo_ref, br_ref, x1_ref, h2p_ref, logit_ref):
    chain_rows = [pl.ds(c * OUTPROJ_CHAIN_ROWS, OUTPROJ_CHAIN_ROWS) for c in range(OUTPROJ_CHAINS)]
    mixes = []
    for rows in chain_rows:
        mix = jnp.dot(fo_ref[rows, :], w_ref[:FOURIER_WIDTH, :], preferred_element_type=F32)
        mixes.append(mix + jnp.dot(ao_ref[rows, :], w_ref[FOURIER_WIDTH:, :], preferred_element_type=F32))
    for c, (rows, mix) in enumerate(zip(chain_rows, mixes)):
        x1 = x_ref[rows, :] + gate_ref[...] * _rms(mix, gpost_ref[...])
        x1_ref[rows, :] = x1
        h2 = _rms(x1, gpre_ref[...]) * (1.0 + scale_ref[...]) + shift_ref[...]
        _pack_rows(h2, h2p_ref.at[pl.ds(c * OUTPROJ_CHAIN_ROWS * PACK_SLABS, OUTPROJ_CHAIN_ROWS * PACK_SLABS)])
        h_hi, h_lo = _split_bf16(h2)
        lg = jnp.dot(h_hi, wr_hi_ref[...], preferred_element_type=F32)
        lg = lg + jnp.dot(h_hi, wr_lo_ref[...], preferred_element_type=F32)
        lg = lg + jnp.dot(h_lo, wr_hi_ref[...], preferred_element_type=F32)
        logit_ref[rows, :] = lg + br_ref[...]


def _outproj(fo, ao, w_out_b, x, gpost, gate, gpre, shift, scale, wr_hi, wr_lo, br):
    b, s, d = x.shape
    tm = OUTPROJ_TM
    row = lambda bi, i: (bi, i, 0)
    per_batch = lambda bi, i: (bi, 0, 0)
    const2 = lambda bi, i: (0, 0)
    return pl.pallas_call(
        _outproj_kernel,
        grid=(b, s // tm),
        in_specs=[
            pl.BlockSpec((None, tm, FOURIER_WIDTH), row),
            pl.BlockSpec((None, tm, ATTN_WIDTH), row),
            pl.BlockSpec((d, d), const2),
            pl.BlockSpec((None, tm, d), row),
            pl.BlockSpec((1, d), const2),
            pl.BlockSpec((None, 1, d), per_batch),
            pl.BlockSpec((1, d), const2),
            pl.BlockSpec((None, 1, d), per_batch),
            pl.BlockSpec((None, 1, d), per_batch),
            pl.BlockSpec((d, ROUTER_PAD), const2),
            pl.BlockSpec((d, ROUTER_PAD), const2),
            pl.BlockSpec((1, ROUTER_PAD), const2),
        ],
        out_specs=[
            pl.BlockSpec((None, tm, d), row),
            pl.BlockSpec((tm * PACK_SLABS, LANES), lambda bi, i: (bi * (s // tm) + i, 0)),
            pl.BlockSpec((None, tm, ROUTER_PAD), row),
        ],
        out_shape=[
            jax.ShapeDtypeStruct((b, s, d), F32),
            jax.ShapeDtypeStruct((b * s * PACK_SLABS, LANES), U32),
            jax.ShapeDtypeStruct((b, s, ROUTER_PAD), F32),
        ],
        compiler_params=pltpu.CompilerParams(
            dimension_semantics=("arbitrary", "arbitrary"), vmem_limit_bytes=VMEM_LIMIT_BYTES),
        name="outproj",
    )(fo, ao, w_out_b, x, gpost, gate, gpre, shift, scale, wr_hi, wr_lo, br)


def _tile_rows(row, n_rows=1):
    return pl.ds(pl.multiple_of(row * PACK_SLABS, PACK_SLABS), n_rows * PACK_SLABS)


def _dispatch_kernel(dest_ref, fill_lo_ref, fill_hi_ref, used_tiles_ref, h_ref, xs_hbm, zeros, sem, fill_sem):
    step = pl.program_id(0)
    base = step * (DISPATCH_TT * TOP_K)

    def row_copy(i, kk):
        return pltpu.make_async_copy(
            h_ref.at[_tile_rows(i)], xs_hbm.at[_tile_rows(dest_ref[base + i * TOP_K + kk])], sem)

    def start(i, c):
        for kk in range(TOP_K):
            row_copy(i, kk).start(priority=kk % 2)
        return c

    def wait(i, c):
        for kk in range(TOP_K):
            row_copy(i, kk).wait()
        return c

    lax.fori_loop(0, DISPATCH_TT, start, 0, unroll=4)

    @pl.when(step == 0)
    def _fill():
        zeros[...] = jnp.zeros(zeros.shape, U32)
        n_tiles = xs_hbm.shape[0] // (MOE_TM * PACK_SLABS)

        def pad_chunks(e, fn):
            lo = fill_lo_ref[e]
            length = fill_hi_ref[e] - lo
            for bit in PAD_CHUNKS:
                @pl.when((length & bit) != 0)
                def _():
                    row = lo + (length & ~(2 * bit - 1))
                    fn(pltpu.make_async_copy(
                        zeros.at[pl.ds(0, bit * PACK_SLABS)], xs_hbm.at[_tile_rows(row, bit)], fill_sem))

        def tail_copy(t):
            return pltpu.make_async_copy(zeros, xs_hbm.at[_tile_rows(t * MOE_TM, MOE_TM)], fill_sem)

        def start_e(e, c):
            pad_chunks(e, lambda cp: cp.start())
            return c

        def wait_e(e, c):
            pad_chunks(e, lambda cp: cp.wait())
            return c

        def start_t(t, c):
            tail_copy(t).start()
            return c

        def wait_t(t, c):
            tail_copy(t).wait()
            return c

        lax.fori_loop(0, N_EXPERTS, start_e, 0)
        lax.fori_loop(used_tiles_ref[0], n_tiles, start_t, 0)
        lax.fori_loop(0, N_EXPERTS, wait_e, 0)
        lax.fori_loop(used_tiles_ref[0], n_tiles, wait_t, 0)

    lax.fori_loop(0, DISPATCH_TT, wait, 0, unroll=4)


def _dispatch(dest, fill_lo, fill_hi, used_tiles, h2p, n_rows):
    n_tok = h2p.shape[0] // PACK_SLABS
    return pl.pallas_call(
        _dispatch_kernel,
        grid_spec=pltpu.PrefetchScalarGridSpec(
            num_scalar_prefetch=4,
            grid=(n_tok // DISPATCH_TT,),
            in_specs=[pl.BlockSpec((DISPATCH_TT * PACK_SLABS, LANES), lambda i, *_: (i, 0))],
            out_specs=pl.BlockSpec(memory_space=pl.ANY),
            scratch_shapes=[
                pltpu.VMEM((MOE_TM * PACK_SLABS, LANES), U32),
                pltpu.SemaphoreType.DMA,
                pltpu.SemaphoreType.DMA,
            ],
        ),
        out_shape=jax.ShapeDtypeStruct((n_rows * PACK_SLABS, LANES), U32),
        compiler_params=pltpu.CompilerParams(
            dimension_semantics=("arbitrary",), vmem_limit_bytes=VMEM_LIMIT_BYTES),
        name="dispatch",
    )(dest, fill_lo, fill_hi, used_tiles, h2p)


def _experts_kernel(item_e_ref, item_row0_ref, item_nt_ref, item_nz_ref,
                    xs_hbm, wg_hbm, wu_hbm, wd_hbm, bg_ref, bu_ref, bd_ref, out_hbm,
                    stage, x_b, acc, wf_g0, wf_u0, wf_d0, wf_g1, wf_u1, wf_d1,
                    wb_g0, wb_u0, wb_d0, wb_g1, wb_u1, wb_d1, x_sems, o_sems, w_sems):
    w = pl.program_id(0)
    nt = item_nt_ref[w]
    nz = item_nz_ref[w]
    e = item_e_ref[w]
    row0 = item_row0_ref[w]
    w_prev = jnp.maximum(w - 1, 0)
    row0_prev, nt_prev = item_row0_ref[w_prev], item_nt_ref[w_prev]
    e_next, row0_next, nt_next = item_e_ref[w + 1], item_row0_ref[w + 1], item_nt_ref[w + 1]
    has_next = nt_next > 0
    slot = w & 1
    wf = ((wf_g0, wf_u0, wf_d0), (wf_g1, wf_u1, wf_d1))
    wb = ((wb_g0, wb_u0, wb_d0), (wb_g1, wb_u1, wb_d1))

    def w_copies(ee, j, ws):
        col = pl.ds(pl.multiple_of(j * MOE_TF, MOE_TF), MOE_TF)
        f_g, f_u, f_d = wf[ws]
        return (pltpu.make_async_copy(wg_hbm.at[ee, :, col], f_g, w_sems.at[ws, 0]),
                pltpu.make_async_copy(wu_hbm.at[ee, :, col], f_u, w_sems.at[ws, 1]),
                pltpu.make_async_copy(wd_hbm.at[ee, col, :], f_d, w_sems.at[ws, 2]))

    def w_start(ee, j, ws):
        for cp in w_copies(ee, j, ws):
            cp.start()

    def w_wait(ee, j, ws):
        for cp in w_copies(ee, j, ws):
            cp.wait()

    def w_cast(ws):
        for f_ref, b_ref in zip(wf[ws], wb[ws]):
            b_ref[...] = f_ref[...].astype(BF16)

    def x_copy(r0, t, ss):
        return pltpu.make_async_copy(
            xs_hbm.at[_tile_rows(r0 + t * MOE_TM, MOE_TM)], stage.at[ss, t], x_sems.at[ss, t])

    def out_copy(r0, t, ss):
        return pltpu.make_async_copy(
            stage.at[ss, t], out_hbm.at[_tile_rows(r0 + t * MOE_TM, MOE_TM)], o_sems.at[ss, t])

    def for_tiles(n, fn):
        for t in range(MOE_R):
            @pl.when(t < n)
            def _():
                fn(t)

    def gate_up(t, j, ws):
        b_g, b_u, _ = wb[ws]
        x = x_b[t]
        g = jnp.dot(x, b_g[...], preferred_element_type=F32) + bg_ref[j]
        u = jnp.dot(x, b_u[...], preferred_element_type=F32) + bu_ref[j]
        g = jnp.minimum(g, SWIGLU_LIMIT)
        u = jnp.clip(u, -SWIGLU_LIMIT, SWIGLU_LIMIT)
        return ((u + 1.0) * (g * jax.nn.sigmoid(SWIGLU_ALPHA * g))).astype(BF16)

    def down(t, hidden, ws):
        acc[t] += jnp.dot(hidden, wb[ws][2][...], preferred_element_type=F32)

    def unpack_tile(t):
        for a in range(PACK_SLABS):
            hi, lo = _unpack_slab(stage.at[slot, t], a, MOE_TM)
            x_b[t, :, a * LANES:(a + 1) * LANES] = hi.astype(BF16)
            x_b[t, :, PACK_HALF + a * LANES:PACK_HALF + (a + 1) * LANES] = lo.astype(BF16)

    def store_tile(t, rows):
        _pack_rows(rows, stage.at[slot, t])
        out_copy(row0, t, slot).start()

    def ff_step(j, ws, first=False, last=False):
        @pl.when(j + 2 < MOE_N_FF)
        def _():
            w_start(e, j + 2, ws)

        @pl.when(jnp.logical_and(j + 2 >= MOE_N_FF, has_next))
        def _():
            w_start(e_next, j + 2 - MOE_N_FF, ws)

        @pl.when(j + 1 < MOE_N_FF)
        def _():
            w_wait(e, j + 1, 1 - ws)

        @pl.when(jnp.logical_and(j + 1 >= MOE_N_FF, has_next))
        def _():
            w_wait(e_next, 0, 1 - ws)

        w_cast(1 - ws)

        done = 0
        for size in MOE_GROUPS:
            def group(i, c, size=size, done=done):
                ts = [done + size * i + k for k in range(size)]
                hs = []
                for t in ts:
                    if first:
                        unpack_tile(t)
                    hs.append(gate_up(t, j, ws))
                for t, h in zip(ts, hs):
                    if first:
                        acc[t] = bd_ref[...] + jnp.dot(h, wb[ws][2][...], preferred_element_type=F32)
                    elif last:
                        store_tile(t, acc[t] + jnp.dot(h, wb[ws][2][...], preferred_element_type=F32))
                    else:
                        down(t, h, ws)
                return c

            n_group = (nt - done) // size
            lax.fori_loop(0, n_group, group, 0)
            done = done + n_group * size

    @pl.when(nt > 0)
    def _active():
        @pl.when(w == 0)
        def _prologue():
            w_start(e, 0, 0)
            w_start(e, 1, 1)
            for_tiles(nt, lambda t: x_copy(row0, t, 0).start())
            w_wait(e, 0, 0)
            w_cast(0)

        for_tiles(nt, lambda t: x_copy(row0, t, slot).wait())
        n_pair = MOE_N_FF // 2

        def ff_pair(jj, c):
            pl.when(jj == 0)(lambda: ff_step(2 * jj, 0, first=True))
            pl.when(jj > 0)(lambda: ff_step(2 * jj, 0))

            @pl.when(jj == 0)
            def _():
                @pl.when(w > 0)
                def _():
                    for_tiles(nt_prev, lambda t: out_copy(row0_prev, t, 1 - slot).wait())

                @pl.when(has_next)
                def _():
                    for_tiles(nt_next, lambda t: x_copy(row0_next, t, 1 - slot).start())

            pl.when(jj < n_pair - 1)(lambda: ff_step(2 * jj + 1, 1))
            pl.when(jj == n_pair - 1)(lambda: ff_step(2 * jj + 1, 1, last=True))
            return c

        lax.fori_loop(0, n_pair, ff_pair, 0)

        @pl.when(jnp.logical_not(has_next))
        def _():
            for_tiles(nt, lambda t: out_copy(row0, t, slot).wait())

    @pl.when(nz > 0)
    def _zero_tail():
        def zero_tile(t):
            stage[0, t] = jnp.zeros(stage.shape[2:], U32)
            out_copy(row0, t, 0).start()

        for_tiles(nz, zero_tile)
        for_tiles(nz, lambda t: out_copy(row0, t, 0).wait())


def _experts(item_e, item_row0, item_nt, item_nz, xs, w_gate, b_gate, w_up, b_up, w_down, b_down, n_items):
    d = D_MODEL
    w_shapes = [(d, MOE_TF), (d, MOE_TF), (MOE_TF, d)]
    by_expert = lambda w, ie, r0, nt, nz: (ie[w], 0, 0, 0)
    return pl.pallas_call(
        _experts_kernel,
        grid_spec=pltpu.PrefetchScalarGridSpec(
            num_scalar_prefetch=4,
            grid=(n_items,),
            in_specs=[
                pl.BlockSpec(memory_space=pl.ANY),
                pl.BlockSpec(memory_space=pl.ANY),
                pl.BlockSpec(memory_space=pl.ANY),
                pl.BlockSpec(memory_space=pl.ANY),
                pl.BlockSpec((None, MOE_N_FF, 1, MOE_TF), by_expert),
                pl.BlockSpec((None, MOE_N_FF, 1, MOE_TF), by_expert),
                pl.BlockSpec((None, 1, d), lambda w, ie, r0, nt, nz: (ie[w], 0, 0)),
            ],
            out_specs=pl.BlockSpec(memory_space=pl.ANY),
            scratch_shapes=[
                pltpu.VMEM((2, MOE_R, MOE_TM * PACK_SLABS, LANES), U32),
                pltpu.VMEM((MOE_R, MOE_TM, d), BF16),
                pltpu.VMEM((MOE_R, MOE_TM, d), F32),
                *[pltpu.VMEM(shape, F32) for shape in w_shapes * 2],
                *[pltpu.VMEM(shape, BF16) for shape in w_shapes * 2],
                pltpu.SemaphoreType.DMA((2, MOE_R)),
                pltpu.SemaphoreType.DMA((2, MOE_R)),
                pltpu.SemaphoreType.DMA((2, 3)),
            ],
        ),
        out_shape=jax.ShapeDtypeStruct(xs.shape, U32),
        compiler_params=pltpu.CompilerParams(
            dimension_semantics=("arbitrary",), vmem_limit_bytes=VMEM_LIMIT_BYTES),
        name="experts",
    )(item_e, item_row0, item_nt, item_nz, xs, w_gate, w_up, w_down,
      b_gate.reshape(N_EXPERTS, MOE_N_FF, 1, MOE_TF), b_up.reshape(N_EXPERTS, MOE_N_FF, 1, MOE_TF),
      b_down.reshape(N_EXPERTS, 1, d))


def _combine_kernel(dest_ref, rows_hbm, gates_ref, x1_ref, gpost_ref, gate_ref, o_ref, buf, sems):
    tt = COMBINE_TT
    step = pl.program_id(0) * pl.num_programs(1) + pl.program_id(1)
    n_steps = pl.num_programs(0) * pl.num_programs(1)
    slot = step & 1

    def row_copy(s, ss, i, kk):
        row = dest_ref[(s * tt + i) * TOP_K + kk]
        return pltpu.make_async_copy(rows_hbm.at[_tile_rows(row)], buf.at[ss, kk, _tile_rows(i)], sems.at[ss])

    def start_step(s, ss):
        def start(i, c):
            for kk in range(TOP_K):
                row_copy(s, ss, i, kk).start(priority=kk % 2)
            return c

        lax.fori_loop(0, tt, start, 0, unroll=4)

    @pl.when(step == 0)
    def _():
        start_step(step, slot)

    @pl.when(step + 1 < n_steps)
    def _():
        start_step(step + 1, 1 - slot)

    def wait(i, c):
        for kk in range(TOP_K):
            row_copy(step, slot, i, kk).wait()
        return c

    lax.fori_loop(0, tt, wait, 0, unroll=4)
    gates = gates_ref[...]
    y_hi, y_lo = [], []
    for a in range(PACK_SLABS):
        acc_hi = acc_lo = None
        for kk in range(TOP_K):
            hi, lo = _unpack_slab(buf.at[slot, kk], a, tt)
            gk = gates[:, kk:kk + 1]
            acc_hi = gk * hi if acc_hi is None else acc_hi + gk * hi
            acc_lo = gk * lo if acc_lo is None else acc_lo + gk * lo
        y_hi.append(acc_hi)
        y_lo.append(acc_lo)
    y = jnp.concatenate(y_hi + y_lo, axis=1)
    o_ref[...] = x1_ref[...] + gate_ref[...] * _rms(y, gpost_ref[...])


def _combine(dest, rows, gates, x1, gpost, gate):
    b, s, d = x1.shape
    tt = COMBINE_TT
    row = lambda bi, i, dst: (bi, i, 0)
    return pl.pallas_call(
        _combine_kernel,
        grid_spec=pltpu.PrefetchScalarGridSpec(
            num_scalar_prefetch=1,
            grid=(b, s // tt),
            in_specs=[
                pl.BlockSpec(memory_space=pl.ANY),
                pl.BlockSpec((None, tt, TOP_K), row),
                pl.BlockSpec((None, tt, d), row),
                pl.BlockSpec((1, d), lambda bi, i, dst: (0, 0)),
                pl.BlockSpec((None, 1, d), lambda bi, i, dst: (bi, 0, 0)),
            ],
            out_specs=pl.BlockSpec((None, tt, d), row),
            scratch_shapes=[pltpu.VMEM((2, TOP_K, tt * PACK_SLABS, LANES), U32),
                            pltpu.SemaphoreType.DMA((2,))],
        ),
        out_shape=jax.ShapeDtypeStruct((b, s, d), F32),
        compiler_params=pltpu.CompilerParams(
            dimension_semantics=("arbitrary", "arbitrary"), vmem_limit_bytes=VMEM_LIMIT_BYTES),
        name="combine",
    )(dest, rows, gates, x1, gpost, gate)


def _rope_tables(seq_len):
    rows = seq_len // GRID_W
    row_idx = jnp.repeat(jnp.arange(rows, dtype=F32), GRID_W)
    col_idx = jnp.tile(jnp.arange(GRID_W, dtype=F32), rows)
    half = HEAD_DIM // 2
    inv_freq = 1.0 / (ROPE_THETA ** (jnp.arange(0, half, 2, dtype=F32) / half))
    ang = jnp.concatenate([row_idx[:, None] * inv_freq, col_idx[:, None] * inv_freq], axis=-1)
    cos = jnp.repeat(jnp.cos(ang), 2, axis=-1)
    sin = jnp.sin(ang)
    sin_signed = jnp.stack([-sin, sin], axis=-1).reshape(seq_len, HEAD_DIM)
    return cos, sin_signed


def _dft_tables(n):
    idx = np.arange(n, dtype=np.int64)
    ang = 2.0 * np.pi * ((idx[:, None] * idx[None, :]) % n).astype(np.float64) / n
    scale = 1.0 / np.sqrt(n)
    return np.cos(ang) * scale, np.sin(ang) * scale


def _topk_kernel(logit_ref, tri_ref, gate_ref, idx_ref, rank_ref, count_ref, base):
    step = pl.program_id(0)

    @pl.when(step == 0)
    def _():
        base[...] = jnp.zeros(base.shape, F32)

    lg = logit_ref[...]
    lane = lax.broadcasted_iota(jnp.int32, lg.shape, 1)
    v = jnp.where(lane < N_EXPERTS, lg, -jnp.inf)
    vals, hits = [], []
    for _ in range(TOP_K):
        m = jnp.max(v, axis=-1, keepdims=True)
        first = jnp.min(jnp.where(v == m, lane, ROUTER_PAD), axis=-1, keepdims=True)
        hit = lane == first
        vals.append(m)
        hits.append(hit)
        v = jnp.where(hit, -jnp.inf, v)
    ex = [jnp.exp(val - vals[0]) for val in vals]
    denom = ex[0]
    for e_k in ex[1:]:
        denom = denom + e_k
    chosen = hits[0]
    for hit in hits[1:]:
        chosen = jnp.logical_or(chosen, hit)
    onehot = chosen.astype(F32)
    before = jnp.dot(tri_ref[...], onehot.astype(BF16), preferred_element_type=F32) + base[...]
    lane_f = lane.astype(F32)
    gate_out = jnp.zeros(lg.shape, F32)
    idx_out = jnp.zeros(lg.shape, F32)
    rank_out = jnp.zeros(lg.shape, F32)
    for kk in range(TOP_K):
        sel = lane == kk
        gate_out = jnp.where(sel, ex[kk] / denom, gate_out)
        idx_out = jnp.where(sel, jnp.sum(jnp.where(hits[kk], lane_f, 0.0), axis=-1, keepdims=True), idx_out)
        rank_out = jnp.where(sel, jnp.sum(jnp.where(hits[kk], before, 0.0), axis=-1, keepdims=True), rank_out)
    gate_ref[...] = gate_out
    idx_ref[...] = idx_out.astype(jnp.int32)
    rank_ref[...] = rank_out.astype(jnp.int32)
    base[...] = base[...] + jnp.sum(onehot, axis=0, keepdims=True)
    count_ref[...] = base[...].astype(jnp.int32)


def _topk(logits):
    n_tok = logits.shape[0]
    tb = ROUTE_TB
    tri = jnp.asarray(np.tril(np.ones((tb, tb), np.float32), -1), dtype=BF16)
    blk = pl.BlockSpec((tb, ROUTER_PAD), lambda i: (i, 0))
    return pl.pallas_call(
        _topk_kernel,
        grid=(n_tok // tb,),
        in_specs=[blk, pl.BlockSpec((tb, tb), lambda i: (0, 0))],
        out_specs=[blk, blk, blk, pl.BlockSpec((1, ROUTER_PAD), lambda i: (0, 0))],
        out_shape=[
            jax.ShapeDtypeStruct((n_tok, ROUTER_PAD), F32),
            jax.ShapeDtypeStruct((n_tok, ROUTER_PAD), jnp.int32),
            jax.ShapeDtypeStruct((n_tok, ROUTER_PAD), jnp.int32),
            jax.ShapeDtypeStruct((1, ROUTER_PAD), jnp.int32),
        ],
        scratch_shapes=[pltpu.VMEM((1, ROUTER_PAD), F32)],
        compiler_params=pltpu.CompilerParams(
            dimension_semantics=("arbitrary",), vmem_limit_bytes=VMEM_LIMIT_BYTES),
        name="topk",
    )(logits, tri)


def _plan_kernel(count_ref, e_ref, rank_ref, dest_ref, fill_lo_ref, fill_hi_ref, used_ref,
                 item_e_ref, item_row0_ref, item_nt_ref, item_nz_ref, pad_start, *, n_tiles, n_items):
    def per_expert(e, carry):
        start, w = carry
        count = count_ref[e]
        tiles = (count + (MOE_TM - 1)) // MOE_TM
        pad_start[e] = start
        fill_lo_ref[e] = start + count
        fill_hi_ref[e] = start + tiles * MOE_TM

        def per_item(k, w):
            item_e_ref[w] = e
            item_row0_ref[w] = start + k * MOE_RS
            item_nt_ref[w] = jnp.minimum(MOE_R, tiles - k * MOE_R)
            item_nz_ref[w] = 0
            return w + 1

        w = lax.fori_loop(0, (tiles + (MOE_R - 1)) // MOE_R, per_item, w)
        return start + tiles * MOE_TM, w

    used_rows, total = lax.fori_loop(0, N_EXPERTS, per_expert, (jnp.int32(0), jnp.int32(0)))
    used_tiles = used_rows // MOE_TM
    used_ref[0] = used_tiles
    last_e = item_e_ref[jnp.maximum(total - 1, 0)]

    def per_filler(w, c):
        tile0 = used_tiles + (w - total) * MOE_R
        item_e_ref[w] = last_e
        item_row0_ref[w] = jnp.minimum(tile0, n_tiles - 1) * MOE_TM
        item_nt_ref[w] = 0
        item_nz_ref[w] = jnp.clip(n_tiles - tile0, 0, MOE_R)
        return c

    lax.fori_loop(total, n_items + 1, per_filler, 0)

    e = e_ref[...]
    dest = rank_ref[...]
    for ee in range(N_EXPERTS):
        dest = dest + jnp.where(e == ee, pad_start[ee], 0)
    dest_ref[...] = dest


def _route(logits, n_tok):
    gates, top_idx, rank, counts = _topk(logits)
    n_tiles = pl.cdiv(n_tok * TOP_K + N_EXPERTS * (MOE_TM - 1), MOE_TM)
    n_rows = n_tiles * MOE_TM
    n_items = N_EXPERTS + n_tiles // MOE_R
    flat = (n_tok * TOP_K // LANES, LANES)
    smem = pl.BlockSpec(memory_space=pltpu.SMEM)
    vmem = pl.BlockSpec(flat, lambda: (0, 0))
    i32 = lambda n: jax.ShapeDtypeStruct((n,), jnp.int32)
    dest, fill_lo, fill_hi, used_tiles, item_e, item_row0, item_nt, item_nz = pl.pallas_call(
        functools.partial(_plan_kernel, n_tiles=n_tiles, n_items=n_items),
        in_specs=[smem, vmem, vmem],
        out_specs=[vmem] + [smem] * 7,
        out_shape=[jax.ShapeDtypeStruct(flat, jnp.int32), i32(N_EXPERTS), i32(N_EXPERTS), i32(1)]
        + [i32(n_items + 1)] * 4,
        scratch_shapes=[pltpu.SMEM((N_EXPERTS,), jnp.int32)],
        name="plan",
    )(counts.reshape(ROUTER_PAD), top_idx[:, :TOP_K].reshape(flat), rank[:, :TOP_K].reshape(flat))
    fill = (fill_lo, fill_hi, used_tiles)
    items = (item_e, item_row0, item_nt, item_nz)
    return gates[:, :TOP_K], dest.reshape(-1), fill, items, n_rows, n_items


def kernel(x, c, w_ada, b_ada, g_pre_mix, w_in, w_fourier, q_norm_g, k_norm_g, g_fourier_out, g_attn_out,
           w_out, g_post_mix, g_pre_ffn, w_router, b_router, w_gate, b_gate, w_up, b_up, w_down, b_down,
           g_post_ffn):
    b, s, d = x.shape
    n_tok = b * s
    depth = w_ada.shape[0]
    cos, sin_signed = _rope_tables(s)
    cs_np, ss_np = _dft_tables(s)
    cs = jnp.asarray(cs_np, dtype=BF16)
    ss = jnp.asarray(ss_np, dtype=BF16)
    cc_np, sc_np = _dft_tables(FOURIER_GROUP_DIM)
    cc = jnp.asarray(np.concatenate([cc_np, -sc_np], axis=0), dtype=BF16)
    c_pad = jnp.pad(c, ((0, 8 - b), (0, 0)))
    row2 = lambda a: a.reshape(1, -1)

    for l in range(depth):
        mod = _ada(c_pad, w_ada[l], row2(b_ada[l]))[:b].reshape(b, N_MOD, 1, d)
        shift_m, scale_m, gate_m = mod[:, 0], mod[:, 1], mod[:, 2]
        shift_f, scale_f, gate_f = mod[:, 3], mod[:, 4], mod[:, 5]

        f, q, k, v = _inproj(
            x, row2(g_pre_mix[l]), shift_m, scale_m, w_in[l].astype(BF16),
            row2(q_norm_g[l] * (HEAD_DIM ** -0.5 * LOG2_E)), row2(k_norm_g[l]), cos, sin_signed)
        fo = _fourier(f, cs, ss, cc, w_fourier[l], row2(g_fourier_out[l]))
        ao = _attn(q, k, v, row2(g_attn_out[l]))

        wr = jnp.pad(w_router[l], ((0, 0), (0, ROUTER_PAD - N_EXPERTS)))
        wr_hi = wr.astype(BF16)
        wr_lo = (wr - wr_hi.astype(F32)).astype(BF16)
        br = jnp.pad(b_router[l], (0, ROUTER_PAD - N_EXPERTS)).reshape(1, ROUTER_PAD)
        x1, h2p, logits = _outproj(
            fo, ao, w_out[l].astype(BF16), x, row2(g_post_mix[l]), gate_m, row2(g_pre_ffn[l]),
            shift_f, scale_f, wr_hi, wr_lo, br)

        gates, dest, fill, items, n_rows, n_items = _route(logits.reshape(n_tok, ROUTER_PAD), n_tok)
        xs = _dispatch(dest, *fill, h2p, n_rows)
        rows = _experts(
            *items, xs,
            w_gate[l], b_gate[l], w_up[l], b_up[l], w_down[l], b_down[l], n_items)
        x = _combine(dest, rows, gates.reshape(b, s, TOP_K), x1, row2(g_post_ffn[l]), gate_f)
    return x
```

```python
import functools

import numpy as np
import jax
import jax.numpy as jnp
from jax import lax
from jax.experimental import pallas as pl
from jax.experimental.pallas import tpu as pltpu

F32 = jnp.float32
BF16 = jnp.bfloat16
U32 = jnp.uint32

D_MODEL = 2048
GRID_W = 64
FOURIER_WIDTH = 512
FOURIER_GROUP_DIM = 128
FOURIER_GROUPS = 4
ATTN_WIDTH = 1536
HEAD_DIM = 128
N_Q_HEADS = 12
GQA_GROUP = 3
N_KV_HEADS = 4
KV_WIDTH = 512
IN_WIDTH = 3072
ROPE_THETA = 10000.0
N_EXPERTS = 32
TOP_K = 4
D_FF = 2048
SWIGLU_LIMIT = 7.0
SWIGLU_ALPHA = 1.702
N_MOD = 6
EPS = 1e-6
LOG2_E = 1.4426950408889634

VMEM_LIMIT_BYTES = 56 * 1024 * 1024
LANES = 128

ADA_TN = 1024
INPROJ_TM = 256
FOURIER_TR = 256
ATTN_TQ = 128
ATTN_LOOKAHEAD = 1
OUTPROJ_CHAIN_ROWS = 256
OUTPROJ_CHAINS = 2
OUTPROJ_TM = OUTPROJ_CHAIN_ROWS * OUTPROJ_CHAINS
ROUTER_PAD = LANES
ROUTE_TB = 512
MOE_TM = 272
PAD_CHUNKS = tuple(1 << k for k in reversed(range((MOE_TM - 1).bit_length())))
MOE_R = 5
MOE_RS = MOE_TM * MOE_R
MOE_TF = 256
MOE_N_FF = D_FF // MOE_TF
MOE_GROUPS = (4, 1)
DISPATCH_TT = 512
COMBINE_TT = 128
PACK_HALF = D_MODEL // 2
PACK_SLABS = PACK_HALF // LANES


def _rms(x, g):
    return x * lax.rsqrt(jnp.mean(x * x, axis=-1, keepdims=True) + EPS) * g


def _ada_kernel(c_ref, w_ref, b_ref, o_ref):
    c = c_ref[...]
    s = (c * jax.nn.sigmoid(c)).astype(BF16)
    o_ref[...] = jnp.dot(s, w_ref[...].astype(BF16), preferred_element_type=F32) + b_ref[...]


def _ada(c_pad, w_ada, b_ada):
    m, d = c_pad.shape
    n = w_ada.shape[1]
    return pl.pallas_call(
        _ada_kernel,
        grid=(n // ADA_TN,),
        in_specs=[
            pl.BlockSpec((m, d), lambda j: (0, 0)),
            pl.BlockSpec((d, ADA_TN), lambda j: (0, j)),
            pl.BlockSpec((1, ADA_TN), lambda j: (0, j)),
        ],
        out_specs=pl.BlockSpec((m, ADA_TN), lambda j: (0, j)),
        out_shape=jax.ShapeDtypeStruct((m, n), F32),
        compiler_params=pltpu.CompilerParams(
            dimension_semantics=("arbitrary",), vmem_limit_bytes=VMEM_LIMIT_BYTES),
        name="ada",
    )(c_pad, w_ada, b_ada)


def _rope(p, cos, sin_signed, even_lane):
    partner = jnp.where(even_lane, pltpu.roll(p, LANES - 1, 1), pltpu.roll(p, 1, 1))
    return p * cos + partner * sin_signed


def _inproj_kernel(x_ref, g_ref, shift_ref, scale_ref, w_ref, qg_ref, kg_ref, cos_ref, sin_ref,
                   f_ref, q_ref, k_ref, v_ref):
    x = x_ref[...]
    h = _rms(x, g_ref[...]) * (1.0 + scale_ref[...]) + shift_ref[...]
    hb = h.astype(BF16)
    cos = cos_ref[...]
    sin = sin_ref[...]
    even_lane = (lax.broadcasted_iota(jnp.int32, cos.shape, 1) % 2) == 0
    chunk = 512
    heads_per_chunk = chunk // HEAD_DIM
    for ci in range(IN_WIDTH // chunk):
        p = jnp.dot(hb, w_ref[:, ci * chunk:(ci + 1) * chunk], preferred_element_type=F32)
        if ci == 0:
            f_ref[...] = p.astype(BF16)
        elif ci == 5:
            v_ref[...] = p.astype(BF16)
        else:
            gain = kg_ref[...] if ci == 4 else qg_ref[...]
            for hh in range(heads_per_chunk):
                ph = p[:, hh * HEAD_DIM:(hh + 1) * HEAD_DIM]
                ph = _rope(_rms(ph, gain), cos, sin, even_lane).astype(BF16)
                if ci == 4:
                    k_ref[:, hh * HEAD_DIM:(hh + 1) * HEAD_DIM] = ph
                else:
                    c0 = (ci - 1) * chunk + hh * HEAD_DIM
                    q_ref[:, c0:c0 + HEAD_DIM] = ph


def _inproj(x, g, shift, scale, w_in_b, qg, kg, cos, sin):
    b, s, d = x.shape
    tm = INPROJ_TM
    row = lambda bi, i: (bi, i, 0)
    per_batch = lambda bi, i: (bi, 0, 0)
    const2 = lambda bi, i: (0, 0)
    return pl.pallas_call(
        _inproj_kernel,
        grid=(b, s // tm),
        in_specs=[
            pl.BlockSpec((None, tm, d), row),
            pl.BlockSpec((1, d), const2),
            pl.BlockSpec((None, 1, d), per_batch),
            pl.BlockSpec((None, 1, d), per_batch),
            pl.BlockSpec((d, IN_WIDTH), const2),
            pl.BlockSpec((1, HEAD_DIM), const2),
            pl.BlockSpec((1, HEAD_DIM), const2),
            pl.BlockSpec((tm, HEAD_DIM), lambda bi, i: (i, 0)),
            pl.BlockSpec((tm, HEAD_DIM), lambda bi, i: (i, 0)),
        ],
        out_specs=[
            pl.BlockSpec((None, tm, FOURIER_WIDTH), row),
            pl.BlockSpec((None, tm, ATTN_WIDTH), row),
            pl.BlockSpec((None, tm, KV_WIDTH), row),
            pl.BlockSpec((None, tm, KV_WIDTH), row),
        ],
        out_shape=[
            jax.ShapeDtypeStruct((b, s, FOURIER_WIDTH), BF16),
            jax.ShapeDtypeStruct((b, s, ATTN_WIDTH), BF16),
            jax.ShapeDtypeStruct((b, s, KV_WIDTH), BF16),
            jax.ShapeDtypeStruct((b, s, KV_WIDTH), BF16),
        ],
        compiler_params=pltpu.CompilerParams(
            dimension_semantics=("arbitrary", "arbitrary"), vmem_limit_bytes=VMEM_LIMIT_BYTES),
        name="inproj",
    )(x, g, shift, scale, w_in_b, qg, kg, cos, sin)


def _fourier_kernel(cs_ref, ss_ref, f_ref, cc_ref, w_ref, g_ref, o_ref):
    f = f_ref[...]
    a = jnp.dot(cs_ref[...], f, preferred_element_type=F32)
    bm = jnp.dot(ss_ref[...], f, preferred_element_type=F32)
    cc = cc_ref[...]
    outs = []
    for gi in range(FOURIER_GROUPS):
        sl = slice(gi * FOURIER_GROUP_DIM, (gi + 1) * FOURIER_GROUP_DIM)
        ab = jnp.concatenate([a[:, sl], bm[:, sl]], axis=1).astype(BF16)
        fm = jnp.dot(ab, cc, preferred_element_type=F32)
        outs.append(jnp.dot(fm.astype(BF16), w_ref[gi].astype(BF16), preferred_element_type=F32))
    fo = jnp.concatenate(outs, axis=1)
    o_ref[...] = _rms(fo, g_ref[...]).astype(BF16)


def _fourier(f, cs, ss, cc, w_fourier, g):
    b, s, fw = f.shape
    tr = FOURIER_TR
    return pl.pallas_call(
        _fourier_kernel,
        grid=(b, s // tr),
        in_specs=[
            pl.BlockSpec((tr, s), lambda bi, i: (i, 0)),
            pl.BlockSpec((tr, s), lambda bi, i: (i, 0)),
            pl.BlockSpec((None, s, fw), lambda bi, i: (bi, 0, 0)),
            pl.BlockSpec((2 * FOURIER_GROUP_DIM, FOURIER_GROUP_DIM), lambda bi, i: (0, 0)),
            pl.BlockSpec((FOURIER_GROUPS, FOURIER_GROUP_DIM, FOURIER_GROUP_DIM), lambda bi, i: (0, 0, 0)),
            pl.BlockSpec((1, fw), lambda bi, i: (0, 0)),
        ],
        out_specs=pl.BlockSpec((None, tr, fw), lambda bi, i: (bi, i, 0)),
        out_shape=jax.ShapeDtypeStruct((b, s, fw), BF16),
        compiler_params=pltpu.CompilerParams(
            dimension_semantics=("arbitrary", "arbitrary"), vmem_limit_bytes=VMEM_LIMIT_BYTES),
        name="fourier",
    )(cs, ss, f, cc, w_fourier, g)


def _attn_kernel(q_ref, k_ref, v_ref, g_ref, o_ref, ao_ref):
    tq = q_ref.shape[0]

    def scores(kv):
        kh = k_ref[:, kv * HEAD_DIM:(kv + 1) * HEAD_DIM]
        c0 = kv * GQA_GROUP * HEAD_DIM
        qh = jnp.concatenate(
            [q_ref[:, c0 + gi * HEAD_DIM:c0 + (gi + 1) * HEAD_DIM] for gi in range(GQA_GROUP)], axis=0)
        return lax.dot_general(qh, kh, (((1,), (1,)), ((), ())), preferred_element_type=F32)

    def attend(kv, sc):
        vh = v_ref[:, kv * HEAD_DIM:(kv + 1) * HEAD_DIM]
        c0 = kv * GQA_GROUP * HEAD_DIM
        m = jnp.max(sc, axis=-1, keepdims=True)
        p = jnp.exp2(sc - m)
        l = jnp.sum(p, axis=-1, keepdims=True)
        o = jnp.dot(p.astype(BF16), vh, preferred_element_type=F32) / l
        for gi in range(GQA_GROUP):
            ao_ref[:, c0 + gi * HEAD_DIM:c0 + (gi + 1) * HEAD_DIM] = o[gi * tq:(gi + 1) * tq]

    pending = [scores(kv) for kv in range(ATTN_LOOKAHEAD)]
    for kv in range(N_KV_HEADS):
        if kv + ATTN_LOOKAHEAD < N_KV_HEADS:
            pending.append(scores(kv + ATTN_LOOKAHEAD))
        attend(kv, pending.pop(0))
    o_ref[...] = _rms(ao_ref[...], g_ref[...]).astype(BF16)


def _attn(q, k, v, g):
    b, s, _ = q.shape
    tq = ATTN_TQ
    return pl.pallas_call(
        _attn_kernel,
        grid=(b, s // tq),
        in_specs=[
            pl.BlockSpec((None, tq, ATTN_WIDTH), lambda bi, i: (bi, i, 0)),
            pl.BlockSpec((None, s, KV_WIDTH), lambda bi, i: (bi, 0, 0)),
            pl.BlockSpec((None, s, KV_WIDTH), lambda bi, i: (bi, 0, 0)),
            pl.BlockSpec((1, ATTN_WIDTH), lambda bi, i: (0, 0)),
        ],
        out_specs=pl.BlockSpec((None, tq, ATTN_WIDTH), lambda bi, i: (bi, i, 0)),
        out_shape=jax.ShapeDtypeStruct((b, s, ATTN_WIDTH), BF16),
        scratch_shapes=[pltpu.VMEM((tq, ATTN_WIDTH), F32)],
        compiler_params=pltpu.CompilerParams(
            dimension_semantics=("arbitrary", "arbitrary"), vmem_limit_bytes=VMEM_LIMIT_BYTES),
        name="attn",
    )(q, k, v, g)


def _split_bf16(a):
    hi = a.astype(BF16)
    lo = (a - hi.astype(F32)).astype(BF16)
    return hi, lo


def _pack_rows(val, dst_ref):
    n = val.shape[0]
    bits = lax.bitcast_convert_type(val.astype(BF16).astype(F32), U32)
    packed = bits[:, :PACK_HALF] | (bits[:, PACK_HALF:] >> 16)
    for a in range(PACK_SLABS):
        dst_ref[pl.ds(a, n, stride=PACK_SLABS), :] = packed[:, a * LANES:(a + 1) * LANES]


def _unpack_slab(src_ref, a, n):
    word = src_ref[pl.ds(a, n, stride=PACK_SLABS), :]
    hi = lax.bitcast_convert_type(word & jnp.uint32(0xFFFF0000), F32)
    lo = lax.bitcast_convert_type(word << 16, F32)
    return hi, lo


def _outproj_kernel(fo_ref, ao_ref, w_ref, x_ref, gpost_ref, gate_ref, gpre_ref, shift_ref, scale_ref,
                    wr_hi_ref, wr_lo_ref, br_ref, x1_ref, h2p_ref, logit_ref):
    chain_rows = [pl.ds(c * OUTPROJ_CHAIN_ROWS, OUTPROJ_CHAIN_ROWS) for c in range(OUTPROJ_CHAINS)]
    mixes = []
    for rows in chain_rows:
        mix = jnp.dot(fo_ref[rows, :], w_ref[:FOURIER_WIDTH, :], preferred_element_type=F32)
        mixes.append(mix + jnp.dot(ao_ref[rows, :], w_ref[FOURIER_WIDTH:, :], preferred_element_type=F32))
    for c, (rows, mix) in enumerate(zip(chain_rows, mixes)):
        x1 = x_ref[rows, :] + gate_ref[...] * _rms(mix, gpost_ref[...])
        x1_ref[rows, :] = x1
        h2 = _rms(x1, gpre_ref[...]) * (1.0 + scale_ref[...]) + shift_ref[...]
        _pack_rows(h2, h2p_ref.at[pl.ds(c * OUTPROJ_CHAIN_ROWS * PACK_SLABS, OUTPROJ_CHAIN_ROWS * PACK_SLABS)])
        h_hi, h_lo = _split_bf16(h2)
        lg = jnp.dot(h_hi, wr_hi_ref[...], preferred_element_type=F32)
        lg = lg + jnp.dot(h_hi, wr_lo_ref[...], preferred_element_type=F32)
        lg = lg + jnp.dot(h_lo, wr_hi_ref[...], preferred_element_type=F32)
        logit_ref[rows, :] = lg + br_ref[...]


def _outproj(fo, ao, w_out_b, x, gpost, gate, gpre, shift, scale, wr_hi, wr_lo, br):
    b, s, d = x.shape
    tm = OUTPROJ_TM
    row = lambda bi, i: (bi, i, 0)
    per_batch = lambda bi, i: (bi, 0, 0)
    const2 = lambda bi, i: (0, 0)
    return pl.pallas_call(
        _outproj_kernel,
        grid=(b, s // tm),
        in_specs=[
            pl.BlockSpec((None, tm, FOURIER_WIDTH), row),
            pl.BlockSpec((None, tm, ATTN_WIDTH), row),
            pl.BlockSpec((d, d), const2),
            pl.BlockSpec((None, tm, d), row),
            pl.BlockSpec((1, d), const2),
            pl.BlockSpec((None, 1, d), per_batch),
            pl.BlockSpec((1, d), const2),
            pl.BlockSpec((None, 1, d), per_batch),
            pl.BlockSpec((None, 1, d), per_batch),
            pl.BlockSpec((d, ROUTER_PAD), const2),
            pl.BlockSpec((d, ROUTER_PAD), const2),
            pl.BlockSpec((1, ROUTER_PAD), const2),
        ],
        out_specs=[
            pl.BlockSpec((None, tm, d), row),
            pl.BlockSpec((tm * PACK_SLABS, LANES), lambda bi, i: (bi * (s // tm) + i, 0)),
            pl.BlockSpec((None, tm, ROUTER_PAD), row),
        ],
        out_shape=[
            jax.ShapeDtypeStruct((b, s, d), F32),
            jax.ShapeDtypeStruct((b * s * PACK_SLABS, LANES), U32),
            jax.ShapeDtypeStruct((b, s, ROUTER_PAD), F32),
        ],
        compiler_params=pltpu.CompilerParams(
            dimension_semantics=("arbitrary", "arbitrary"), vmem_limit_bytes=VMEM_LIMIT_BYTES),
        name="outproj",
    )(fo, ao, w_out_b, x, gpost, gate, gpre, shift, scale, wr_hi, wr_lo, br)


def _tile_rows(row, n_rows=1):
    return pl.ds(pl.multiple_of(row * PACK_SLABS, PACK_SLABS), n_rows * PACK_SLABS)


def _dispatch_kernel(dest_ref, fill_lo_ref, fill_hi_ref, used_tiles_ref, h_ref, xs_hbm, zeros, sem, fill_sem):
    step = pl.program_id(0)
    base = step * (DISPATCH_TT * TOP_K)

    def row_copy(i, kk):
        return pltpu.make_async_copy(
            h_ref.at[_tile_rows(i)], xs_hbm.at[_tile_rows(dest_ref[base + i * TOP_K + kk])], sem)

    def start(i, c):
        for kk in range(TOP_K):
            row_copy(i, kk).start(priority=kk % 2)
        return c

    def wait(i, c):
        for kk in range(TOP_K):
            row_copy(i, kk).wait()
        return c

    lax.fori_loop(0, DISPATCH_TT, start, 0, unroll=4)

    @pl.when(step == 0)
    def _fill():
        zeros[...] = jnp.zeros(zeros.shape, U32)
        n_tiles = xs_hbm.shape[0] // (MOE_TM * PACK_SLABS)

        def pad_chunks(e, fn):
            lo = fill_lo_ref[e]
            length = fill_hi_ref[e] - lo
            for bit in PAD_CHUNKS:
                @pl.when((length & bit) != 0)
                def _():
                    row = lo + (length & ~(2 * bit - 1))
                    fn(pltpu.make_async_copy(
                        zeros.at[pl.ds(0, bit * PACK_SLABS)], xs_hbm.at[_tile_rows(row, bit)], fill_sem))

        def tail_copy(t):
            return pltpu.make_async_copy(zeros, xs_hbm.at[_tile_rows(t * MOE_TM, MOE_TM)], fill_sem)

        def start_e(e, c):
            pad_chunks(e, lambda cp: cp.start())
            return c

        def wait_e(e, c):
            pad_chunks(e, lambda cp: cp.wait())
            return c

        def start_t(t, c):
            tail_copy(t).start()
            return c

        def wait_t(t, c):
            tail_copy(t).wait()
            return c

        lax.fori_loop(0, N_EXPERTS, start_e, 0)
        lax.fori_loop(used_tiles_ref[0], n_tiles, start_t, 0)
        lax.fori_loop(0, N_EXPERTS, wait_e, 0)
        lax.fori_loop(used_tiles_ref[0], n_tiles, wait_t, 0)

    lax.fori_loop(0, DISPATCH_TT, wait, 0, unroll=4)


def _dispatch(dest, fill_lo, fill_hi, used_tiles, h2p, n_rows):
    n_tok = h2p.shape[0] // PACK_SLABS
    return pl.pallas_call(
        _dispatch_kernel,
        grid_spec=pltpu.PrefetchScalarGridSpec(
            num_scalar_prefetch=4,
            grid=(n_tok // DISPATCH_TT,),
            in_specs=[pl.BlockSpec((DISPATCH_TT * PACK_SLABS, LANES), lambda i, *_: (i, 0))],
            out_specs=pl.BlockSpec(memory_space=pl.ANY),
            scratch_shapes=[
                pltpu.VMEM((MOE_TM * PACK_SLABS, LANES), U32),
                pltpu.SemaphoreType.DMA,
                pltpu.SemaphoreType.DMA,
            ],
        ),
        out_shape=jax.ShapeDtypeStruct((n_rows * PACK_SLABS, LANES), U32),
        compiler_params=pltpu.CompilerParams(
            dimension_semantics=("arbitrary",), vmem_limit_bytes=VMEM_LIMIT_BYTES),
        name="dispatch",
    )(dest, fill_lo, fill_hi, used_tiles, h2p)


def _experts_kernel(item_e_ref, item_row0_ref, item_nt_ref, item_nz_ref,
                    xs_hbm, wg_hbm, wu_hbm, wd_hbm, bg_ref, bu_ref, bd_ref, out_hbm,
                    stage, x_b, acc, wf_g0, wf_u0, wf_d0, wf_g1, wf_u1, wf_d1,
                    wb_g0, wb_u0, wb_d0, wb_g1, wb_u1, wb_d1, x_sems, o_sems, w_sems):
    w = pl.program_id(0)
    nt = item_nt_ref[w]
    nz = item_nz_ref[w]
    e = item_e_ref[w]
    row0 = item_row0_ref[w]
    w_prev = jnp.maximum(w - 1, 0)
    row0_prev, nt_prev = item_row0_ref[w_prev], item_nt_ref[w_prev]
    e_next, row0_next, nt_next = item_e_ref[w + 1], item_row0_ref[w + 1], item_nt_ref[w + 1]
    has_next = nt_next > 0
    slot = w & 1
    wf = ((wf_g0, wf_u0, wf_d0), (wf_g1, wf_u1, wf_d1))
    wb = ((wb_g0, wb_u0, wb_d0), (wb_g1, wb_u1, wb_d1))

    def w_copies(ee, j, ws):
        col = pl.ds(pl.multiple_of(j * MOE_TF, MOE_TF), MOE_TF)
        f_g, f_u, f_d = wf[ws]
        return (pltpu.make_async_copy(wg_hbm.at[ee, :, col], f_g, w_sems.at[ws, 0]),
                pltpu.make_async_copy(wu_hbm.at[ee, :, col], f_u, w_sems.at[ws, 1]),
                pltpu.make_async_copy(wd_hbm.at[ee, col, :], f_d, w_sems.at[ws, 2]))

    def w_start(ee, j, ws):
        for cp in w_copies(ee, j, ws):
            cp.start()

    def w_wait(ee, j, ws):
        for cp in w_copies(ee, j, ws):
            cp.wait()

    def w_cast(ws):
        for f_ref, b_ref in zip(wf[ws], wb[ws]):
            b_ref[...] = f_ref[...].astype(BF16)

    def x_copy(r0, t, ss):
        return pltpu.make_async_copy(
            xs_hbm.at[_tile_rows(r0 + t * MOE_TM, MOE_TM)], stage.at[ss, t], x_sems.at[ss, t])

    def out_copy(r0, t, ss):
        return pltpu.make_async_copy(
            stage.at[ss, t], out_hbm.at[_tile_rows(r0 + t * MOE_TM, MOE_TM)], o_sems.at[ss, t])

    def for_tiles(n, fn):
        for t in range(MOE_R):
            @pl.when(t < n)
            def _():
                fn(t)

    def gate_up(t, j, ws):
        b_g, b_u, _ = wb[ws]
        x = x_b[t]
        g = jnp.dot(x, b_g[...], preferred_element_type=F32) + bg_ref[j]
        u = jnp.dot(x, b_u[...], preferred_element_type=F32) + bu_ref[j]
        g = jnp.minimum(g, SWIGLU_LIMIT)
        u = jnp.clip(u, -SWIGLU_LIMIT, SWIGLU_LIMIT)
        return ((u + 1.0) * (g * jax.nn.sigmoid(SWIGLU_ALPHA * g))).astype(BF16)

    def down(t, hidden, ws):
        acc[t] += jnp.dot(hidden, wb[ws][2][...], preferred_element_type=F32)

    def unpack_tile(t):
        for a in range(PACK_SLABS):
            hi, lo = _unpack_slab(stage.at[slot, t], a, MOE_TM)
            x_b[t, :, a * LANES:(a + 1) * LANES] = hi.astype(BF16)
            x_b[t, :, PACK_HALF + a * LANES:PACK_HALF + (a + 1) * LANES] = lo.astype(BF16)

    def store_tile(t, rows):
        _pack_rows(rows, stage.at[slot, t])
        out_copy(row0, t, slot).start()

    def ff_step(j, ws, first=False, last=False):
        @pl.when(j + 2 < MOE_N_FF)
        def _():
            w_start(e, j + 2, ws)

        @pl.when(jnp.logical_and(j + 2 >= MOE_N_FF, has_next))
        def _():
            w_start(e_next, j + 2 - MOE_N_FF, ws)

        @pl.when(j + 1 < MOE_N_FF)
        def _():
            w_wait(e, j + 1, 1 - ws)

        @pl.when(jnp.logical_and(j + 1 >= MOE_N_FF, has_next))
        def _():
            w_wait(e_next, 0, 1 - ws)

        w_cast(1 - ws)

        done = 0
        for size in MOE_GROUPS:
            def group(i, c, size=size, done=done):
                ts = [done + size * i + k for k in range(size)]
                hs = []
                for t in ts:
                    if first:
                        unpack_tile(t)
                    hs.append(gate_up(t, j, ws))
                for t, h in zip(ts, hs):
                    if first:
                        acc[t] = bd_ref[...] + jnp.dot(h, wb[ws][2][...], preferred_element_type=F32)
                    elif last:
                        store_tile(t, acc[t] + jnp.dot(h, wb[ws][2][...], preferred_element_type=F32))
                    else:
                        down(t, h, ws)
                return c

            n_group = (nt - done) // size
            lax.fori_loop(0, n_group, group, 0)
            done = done + n_group * size

    @pl.when(nt > 0)
    def _active():
        @pl.when(w == 0)
        def _prologue():
            w_start(e, 0, 0)
            w_start(e, 1, 1)
            for_tiles(nt, lambda t: x_copy(row0, t, 0).start())
            w_wait(e, 0, 0)
            w_cast(0)

        for_tiles(nt, lambda t: x_copy(row0, t, slot).wait())
        n_pair = MOE_N_FF // 2

        def ff_pair(jj, c):
            pl.when(jj == 0)(lambda: ff_step(2 * jj, 0, first=True))
            pl.when(jj > 0)(lambda: ff_step(2 * jj, 0))

            @pl.when(jj == 0)
            def _():
                @pl.when(w > 0)
                def _():
                    for_tiles(nt_prev, lambda t: out_copy(row0_prev, t, 1 - slot).wait())

                @pl.when(has_next)
                def _():
                    for_tiles(nt_next, lambda t: x_copy(row0_next, t, 1 - slot).start())

            pl.when(jj < n_pair - 1)(lambda: ff_step(2 * jj + 1, 1))
            pl.when(jj == n_pair - 1)(lambda: ff_step(2 * jj + 1, 1, last=True))
            return c

        lax.fori_loop(0, n_pair, ff_pair, 0)

        @pl.when(jnp.logical_not(has_next))
        def _():
            for_tiles(nt, lambda t: out_copy(row0, t, slot).wait())

    @pl.when(nz > 0)
    def _zero_tail():
        def zero_tile(t):
            stage[0, t] = jnp.zeros(stage.shape[2:], U32)
            out_copy(row0, t, 0).start()

        for_tiles(nz, zero_tile)
        for_tiles(nz, lambda t: out_copy(row0, t, 0).wait())


def _experts(item_e, item_row0, item_nt, item_nz, xs, w_gate, b_gate, w_up, b_up, w_down, b_down, n_items):
    d = D_MODEL
    w_shapes = [(d, MOE_TF), (d, MOE_TF), (MOE_TF, d)]
    by_expert = lambda w, ie, r0, nt, nz: (ie[w], 0, 0, 0)
    return pl.pallas_call(
        _experts_kernel,
        grid_spec=pltpu.PrefetchScalarGridSpec(
            num_scalar_prefetch=4,
            grid=(n_items,),
            in_specs=[
                pl.BlockSpec(memory_space=pl.ANY),
                pl.BlockSpec(memory_space=pl.ANY),
                pl.BlockSpec(memory_space=pl.ANY),
                pl.BlockSpec(memory_space=pl.ANY),
                pl.BlockSpec((None, MOE_N_FF, 1, MOE_TF), by_expert),
                pl.BlockSpec((None, MOE_N_FF, 1, MOE_TF), by_expert),
                pl.BlockSpec((None, 1, d), lambda w, ie, r0, nt, nz: (ie[w], 0, 0)),
            ],
            out_specs=pl.BlockSpec(memory_space=pl.ANY),
            scratch_shapes=[
                pltpu.VMEM((2, MOE_R, MOE_TM * PACK_SLABS, LANES), U32),
                pltpu.VMEM((MOE_R, MOE_TM, d), BF16),
                pltpu.VMEM((MOE_R, MOE_TM, d), F32),
                *[pltpu.VMEM(shape, F32) for shape in w_shapes * 2],
                *[pltpu.VMEM(shape, BF16) for shape in w_shapes * 2],
                pltpu.SemaphoreType.DMA((2, MOE_R)),
                pltpu.SemaphoreType.DMA((2, MOE_R)),
                pltpu.SemaphoreType.DMA((2, 3)),
            ],
        ),
        out_shape=jax.ShapeDtypeStruct(xs.shape, U32),
        compiler_params=pltpu.CompilerParams(
            dimension_semantics=("arbitrary",), vmem_limit_bytes=VMEM_LIMIT_BYTES),
        name="experts",
    )(item_e, item_row0, item_nt, item_nz, xs, w_gate, w_up, w_down,
      b_gate.reshape(N_EXPERTS, MOE_N_FF, 1, MOE_TF), b_up.reshape(N_EXPERTS, MOE_N_FF, 1, MOE_TF),
      b_down.reshape(N_EXPERTS, 1, d))


def _combine_kernel(dest_ref, rows_hbm, gates_ref, x1_ref, gpost_ref, gate_ref, o_ref, buf, sems):
    tt = COMBINE_TT
    step = pl.program_id(0) * pl.num_programs(1) + pl.program_id(1)
    n_steps = pl.num_programs(0) * pl.num_programs(1)
    slot = step & 1

    def row_copy(s, ss, i, kk):
        row = dest_ref[(s * tt + i) * TOP_K + kk]
        return pltpu.make_async_copy(rows_hbm.at[_tile_rows(row)], buf.at[ss, kk, _tile_rows(i)], sems.at[ss])

    def start_step(s, ss):
        def start(i, c):
            for kk in range(TOP_K):
                row_copy(s, ss, i, kk).start(priority=kk % 2)
            return c

        lax.fori_loop(0, tt, start, 0, unroll=4)

    @pl.when(step == 0)
    def _():
        start_step(step, slot)

    @pl.when(step + 1 < n_steps)
    def _():
        start_step(step + 1, 1 - slot)

    def wait(i, c):
        for kk in range(TOP_K):
            row_copy(step, slot, i, kk).wait()
        return c

    lax.fori_loop(0, tt, wait, 0, unroll=4)
    gates = gates_ref[...]
    y_hi, y_lo = [], []
    for a in range(PACK_SLABS):
        acc_hi = acc_lo = None
        for kk in range(TOP_K):
            hi, lo = _unpack_slab(buf.at[slot, kk], a, tt)
            gk = gates[:, kk:kk + 1]
            acc_hi = gk * hi if acc_hi is None else acc_hi + gk * hi
            acc_lo = gk * lo if acc_lo is None else acc_lo + gk * lo
        y_hi.append(acc_hi)
        y_lo.append(acc_lo)
    y = jnp.concatenate(y_hi + y_lo, axis=1)
    o_ref[...] = x1_ref[...] + gate_ref[...] * _rms(y, gpost_ref[...])


def _combine(dest, rows, gates, x1, gpost, gate):
    b, s, d = x1.shape
    tt = COMBINE_TT
    row = lambda bi, i, dst: (bi, i, 0)
    return pl.pallas_call(
        _combine_kernel,
        grid_spec=pltpu.PrefetchScalarGridSpec(
            num_scalar_prefetch=1,
            grid=(b, s // tt),
            in_specs=[
                pl.BlockSpec(memory_space=pl.ANY),
                pl.BlockSpec((None, tt, TOP_K), row),
                pl.BlockSpec((None, tt, d), row),
                pl.BlockSpec((1, d), lambda bi, i, dst: (0, 0)),
                pl.BlockSpec((None, 1, d), lambda bi, i, dst: (bi, 0, 0)),
            ],
            out_specs=pl.BlockSpec((None, tt, d), row),
            scratch_shapes=[pltpu.VMEM((2, TOP_K, tt * PACK_SLABS, LANES), U32),
                            pltpu.SemaphoreType.DMA((2,))],
        ),
        out_shape=jax.ShapeDtypeStruct((b, s, d), F32),
        compiler_params=pltpu.CompilerParams(
            dimension_semantics=("arbitrary", "arbitrary"), vmem_limit_bytes=VMEM_LIMIT_BYTES),
        name="combine",
    )(dest, rows, gates, x1, gpost, gate)


def _rope_tables(seq_len):
    rows = seq_len // GRID_W
    row_idx = jnp.repeat(jnp.arange(rows, dtype=F32), GRID_W)
    col_idx = jnp.tile(jnp.arange(GRID_W, dtype=F32), rows)
    half = HEAD_DIM // 2
    inv_freq = 1.0 / (ROPE_THETA ** (jnp.arange(0, half, 2, dtype=F32) / half))
    ang = jnp.concatenate([row_idx[:, None] * inv_freq, col_idx[:, None] * inv_freq], axis=-1)
    cos = jnp.repeat(jnp.cos(ang), 2, axis=-1)
    sin = jnp.sin(ang)
    sin_signed = jnp.stack([-sin, sin], axis=-1).reshape(seq_len, HEAD_DIM)
    return cos, sin_signed


def _dft_tables(n):
    idx = np.arange(n, dtype=np.int64)
    ang = 2.0 * np.pi * ((idx[:, None] * idx[None, :]) % n).astype(np.float64) / n
    scale = 1.0 / np.sqrt(n)
    return np.cos(ang) * scale, np.sin(ang) * scale


def _topk_kernel(logit_ref, tri_ref, gate_ref, idx_ref, rank_ref, count_ref, base):
    step = pl.program_id(0)

    @pl.when(step == 0)
    def _():
        base[...] = jnp.zeros(base.shape, F32)

    lg = logit_ref[...]
    lane = lax.broadcasted_iota(jnp.int32, lg.shape, 1)
    v = jnp.where(lane < N_EXPERTS, lg, -jnp.inf)
    vals, hits = [], []
    for _ in range(TOP_K):
        m = jnp.max(v, axis=-1, keepdims=True)
        first = jnp.min(jnp.where(v == m, lane, ROUTER_PAD), axis=-1, keepdims=True)
        hit = lane == first
        vals.append(m)
        hits.append(hit)
        v = jnp.where(hit, -jnp.inf, v)
    ex = [jnp.exp(val - vals[0]) for val in vals]
    denom = ex[0]
    for e_k in ex[1:]:
        denom = denom + e_k
    chosen = hits[0]
    for hit in hits[1:]:
        chosen = jnp.logical_or(chosen, hit)
    onehot = chosen.astype(F32)
    before = jnp.dot(tri_ref[...], onehot.astype(BF16), preferred_element_type=F32) + base[...]
    lane_f = lane.astype(F32)
    gate_out = jnp.zeros(lg.shape, F32)
    idx_out = jnp.zeros(lg.shape, F32)
    rank_out = jnp.zeros(lg.shape, F32)
    for kk in range(TOP_K):
        sel = lane == kk
        gate_out = jnp.where(sel, ex[kk] / denom, gate_out)
        idx_out = jnp.where(sel, jnp.sum(jnp.where(hits[kk], lane_f, 0.0), axis=-1, keepdims=True), idx_out)
        rank_out = jnp.where(sel, jnp.sum(jnp.where(hits[kk], before, 0.0), axis=-1, keepdims=True), rank_out)
    gate_ref[...] = gate_out
    idx_ref[...] = idx_out.astype(jnp.int32)
    rank_ref[...] = rank_out.astype(jnp.int32)
    base[...] = base[...] + jnp.sum(onehot, axis=0, keepdims=True)
    count_ref[...] = base[...].astype(jnp.int32)


def _topk(logits):
    n_tok = logits.shape[0]
    tb = ROUTE_TB
    tri = jnp.asarray(np.tril(np.ones((tb, tb), np.float32), -1), dtype=BF16)
    blk = pl.BlockSpec((tb, ROUTER_PAD), lambda i: (i, 0))
    return pl.pallas_call(
        _topk_kernel,
        grid=(n_tok // tb,),
        in_specs=[blk, pl.BlockSpec((tb, tb), lambda i: (0, 0))],
        out_specs=[blk, blk, blk, pl.BlockSpec((1, ROUTER_PAD), lambda i: (0, 0))],
        out_shape=[
            jax.ShapeDtypeStruct((n_tok, ROUTER_PAD), F32),
            jax.ShapeDtypeStruct((n_tok, ROUTER_PAD), jnp.int32),
            jax.ShapeDtypeStruct((n_tok, ROUTER_PAD), jnp.int32),
            jax.ShapeDtypeStruct((1, ROUTER_PAD), jnp.int32),
        ],
        scratch_shapes=[pltpu.VMEM((1, ROUTER_PAD), F32)],
        compiler_params=pltpu.CompilerParams(
            dimension_semantics=("arbitrary",), vmem_limit_bytes=VMEM_LIMIT_BYTES),
        name="topk",
    )(logits, tri)


def _plan_kernel(count_ref, e_ref, rank_ref, dest_ref, fill_lo_ref, fill_hi_ref, used_ref,
                 item_e_ref, item_row0_ref, item_nt_ref, item_nz_ref, pad_start, *, n_tiles, n_items):
    def per_expert(e, carry):
        start, w = carry
        count = count_ref[e]
        tiles = (count + (MOE_TM - 1)) // MOE_TM
        pad_start[e] = start
        fill_lo_ref[e] = start + count
        fill_hi_ref[e] = start + tiles * MOE_TM

        def per_item(k, w):
            item_e_ref[w] = e
            item_row0_ref[w] = start + k * MOE_RS
            item_nt_ref[w] = jnp.minimum(MOE_R, tiles - k * MOE_R)
            item_nz_ref[w] = 0
            return w + 1

        w = lax.fori_loop(0, (tiles + (MOE_R - 1)) // MOE_R, per_item, w)
        return start + tiles * MOE_TM, w

    used_rows, total = lax.fori_loop(0, N_EXPERTS, per_expert, (jnp.int32(0), jnp.int32(0)))
    used_tiles = used_rows // MOE_TM
    used_ref[0] = used_tiles
    last_e = item_e_ref[jnp.maximum(total - 1, 0)]

    def per_filler(w, c):
        tile0 = used_tiles + (w - total) * MOE_R
        item_e_ref[w] = last_e
        item_row0_ref[w] = jnp.minimum(tile0, n_tiles - 1) * MOE_TM
        item_nt_ref[w] = 0
        item_nz_ref[w] = jnp.clip(n_tiles - tile0, 0, MOE_R)
        return c

    lax.fori_loop(total, n_items + 1, per_filler, 0)

    e = e_ref[...]
    dest = rank_ref[...]
    for ee in range(N_EXPERTS):
        dest = dest + jnp.where(e == ee, pad_start[ee], 0)
    dest_ref[...] = dest


def _route(logits, n_tok):
    gates, top_idx, rank, counts = _topk(logits)
    n_tiles = pl.cdiv(n_tok * TOP_K + N_EXPERTS * (MOE_TM - 1), MOE_TM)
    n_rows = n_tiles * MOE_TM
    n_items = N_EXPERTS + n_tiles // MOE_R
    flat = (n_tok * TOP_K // LANES, LANES)
    smem = pl.BlockSpec(memory_space=pltpu.SMEM)
    vmem = pl.BlockSpec(flat, lambda: (0, 0))
    i32 = lambda n: jax.ShapeDtypeStruct((n,), jnp.int32)
    dest, fill_lo, fill_hi, used_tiles, item_e, item_row0, item_nt, item_nz = pl.pallas_call(
        functools.partial(_plan_kernel, n_tiles=n_tiles, n_items=n_items),
        in_specs=[smem, vmem, vmem],
        out_specs=[vmem] + [smem] * 7,
        out_shape=[jax.ShapeDtypeStruct(flat, jnp.int32), i32(N_EXPERTS), i32(N_EXPERTS), i32(1)]
        + [i32(n_items + 1)] * 4,
        scratch_shapes=[pltpu.SMEM((N_EXPERTS,), jnp.int32)],
        name="plan",
    )(counts.reshape(ROUTER_PAD), top_idx[:, :TOP_K].reshape(flat), rank[:, :TOP_K].reshape(flat))
    fill = (fill_lo, fill_hi, used_tiles)
    items = (item_e, item_row0, item_nt, item_nz)
    return gates[:, :TOP_K], dest.reshape(-1), fill, items, n_rows, n_items


def kernel(x, c, w_ada, b_ada, g_pre_mix, w_in, w_fourier, q_norm_g, k_norm_g, g_fourier_out, g_attn_out,
           w_out, g_post_mix, g_pre_ffn, w_router, b_router, w_gate, b_gate, w_up, b_up, w_down, b_down,
           g_post_ffn):
    b, s, d = x.shape
    n_tok = b * s
    depth = w_ada.shape[0]
    cos, sin_signed = _rope_tables(s)
    cs_np, ss_np = _dft_tables(s)
    cs = jnp.asarray(cs_np, dtype=BF16)
    ss = jnp.asarray(ss_np, dtype=BF16)
    cc_np, sc_np = _dft_tables(FOURIER_GROUP_DIM)
    cc = jnp.asarray(np.concatenate([cc_np, -sc_np], axis=0), dtype=BF16)
    c_pad = jnp.pad(c, ((0, 8 - b), (0, 0)))
    row2 = lambda a: a.reshape(1, -1)

    for l in range(depth):
        mod = _ada(c_pad, w_ada[l], row2(b_ada[l]))[:b].reshape(b, N_MOD, 1, d)
        shift_m, scale_m, gate_m = mod[:, 0], mod[:, 1], mod[:, 2]
        shift_f, scale_f, gate_f = mod[:, 3], mod[:, 4], mod[:, 5]

        f, q, k, v = _inproj(
            x, row2(g_pre_mix[l]), shift_m, scale_m, w_in[l].astype(BF16),
            row2(q_norm_g[l] * (HEAD_DIM ** -0.5 * LOG2_E)), row2(k_norm_g[l]), cos, sin_signed)
        fo = _fourier(f, cs, ss, cc, w_fourier[l], row2(g_fourier_out[l]))
        ao = _attn(q, k, v, row2(g_attn_out[l]))

        wr = jnp.pad(w_router[l], ((0, 0), (0, ROUTER_PAD - N_EXPERTS)))
        wr_hi = wr.astype(BF16)
        wr_lo = (wr - wr_hi.astype(F32)).astype(BF16)
        br = jnp.pad(b_router[l], (0, ROUTER_PAD - N_EXPERTS)).reshape(1, ROUTER_PAD)
        x1, h2p, logits = _outproj(
            fo, ao, w_out[l].astype(BF16), x, row2(g_post_mix[l]), gate_m, row2(g_pre_ffn[l]),
            shift_f, scale_f, wr_hi, wr_lo, br)

        gates, dest, fill, items, n_rows, n_items = _route(logits.reshape(n_tok, ROUTER_PAD), n_tok)
        xs = _dispatch(dest, *fill, h2p, n_rows)
        rows = _experts(
            *items, xs,
            w_gate[l], b_gate[l], w_up[l], b_up[l], w_down[l], b_down[l], n_items)
        x = _combine(dest, rows, gates.reshape(b, s, TOP_K), x1, row2(g_post_ffn[l]), gate_f)
    return x
```

```python
import functools

import numpy as np
import jax
import jax.numpy as jnp
from jax import lax
from jax.experimental import pallas as pl
from jax.experimental.pallas import tpu as pltpu

F32 = jnp.float32
BF16 = jnp.bfloat16
U32 = jnp.uint32

D_MODEL = 2048
GRID_W = 64
FOURIER_WIDTH = 512
FOURIER_GROUP_DIM = 128
FOURIER_GROUPS = 4
ATTN_WIDTH = 1536
HEAD_DIM = 128
N_Q_HEADS = 12
GQA_GROUP = 3
N_KV_HEADS = 4
KV_WIDTH = 512
IN_WIDTH = 3072
ROPE_THETA = 10000.0
N_EXPERTS = 32
TOP_K = 4
D_FF = 2048
SWIGLU_LIMIT = 7.0
SWIGLU_ALPHA = 1.702
N_MOD = 6
EPS = 1e-6
LOG2_E = 1.4426950408889634

VMEM_LIMIT_BYTES = 56 * 1024 * 1024
LANES = 128

ADA_TN = 1024
INPROJ_TM = 256
FOURIER_TR = 256
ATTN_TQ = 128
ATTN_LOOKAHEAD = 1
OUTPROJ_CHAIN_ROWS = 256
OUTPROJ_CHAINS = 2
OUTPROJ_TM = OUTPROJ_CHAIN_ROWS * OUTPROJ_CHAINS
ROUTER_PAD = LANES
ROUTE_TB = 512
MOE_TM = 272
PAD_CHUNKS = tuple(1 << k for k in reversed(range((MOE_TM - 1).bit_length())))
MOE_R = 5
MOE_RS = MOE_TM * MOE_R
MOE_TF = 256
MOE_N_FF = D_FF // MOE_TF
MOE_GROUPS = (4, 1)
DISPATCH_TT = 512
COMBINE_TT = 256
PACK_HALF = D_MODEL // 2
PACK_SLABS = PACK_HALF // LANES


def _rms(x, g):
    return x * lax.rsqrt(jnp.mean(x * x, axis=-1, keepdims=True) + EPS) * g


def _ada_kernel(c_ref, w_ref, b_ref, o_ref):
    c = c_ref[...]
    s = (c * jax.nn.sigmoid(c)).astype(BF16)
    o_ref[...] = jnp.dot(s, w_ref[...].astype(BF16), preferred_element_type=F32) + b_ref[...]


def _ada(c_pad, w_ada, b_ada):
    m, d = c_pad.shape
    n = w_ada.shape[1]
    return pl.pallas_call(
        _ada_kernel,
        grid=(n // ADA_TN,),
        in_specs=[
            pl.BlockSpec((m, d), lambda j: (0, 0)),
            pl.BlockSpec((d, ADA_TN), lambda j: (0, j)),
            pl.BlockSpec((1, ADA_TN), lambda j: (0, j)),
        ],
        out_specs=pl.BlockSpec((m, ADA_TN), lambda j: (0, j)),
        out_shape=jax.ShapeDtypeStruct((m, n), F32),
        compiler_params=pltpu.CompilerParams(
            dimension_semantics=("arbitrary",), vmem_limit_bytes=VMEM_LIMIT_BYTES),
        name="ada",
    )(c_pad, w_ada, b_ada)


def _rope(p, cos, sin_signed, even_lane):
    partner = jnp.where(even_lane, pltpu.roll(p, LANES - 1, 1), pltpu.roll(p, 1, 1))
    return p * cos + partner * sin_signed


def _inproj_kernel(x_ref, g_ref, shift_ref, scale_ref, w_ref, qg_ref, kg_ref, cos_ref, sin_ref,
                   f_ref, q_ref, k_ref, v_ref):
    x = x_ref[...]
    h = _rms(x, g_ref[...]) * (1.0 + scale_ref[...]) + shift_ref[...]
    hb = h.astype(BF16)
    cos = cos_ref[...]
    sin = sin_ref[...]
    even_lane = (lax.broadcasted_iota(jnp.int32, cos.shape, 1) % 2) == 0
    chunk = 512
    heads_per_chunk = chunk // HEAD_DIM
    for ci in range(IN_WIDTH // chunk):
        p = jnp.dot(hb, w_ref[:, ci * chunk:(ci + 1) * chunk], preferred_element_type=F32)
        if ci == 0:
            f_ref[...] = p.astype(BF16)
        elif ci == 5:
            v_ref[...] = p.astype(BF16)
        else:
            gain = kg_ref[...] if ci == 4 else qg_ref[...]
            for hh in range(heads_per_chunk):
                ph = p[:, hh * HEAD_DIM:(hh + 1) * HEAD_DIM]
                ph = _rope(_rms(ph, gain), cos, sin, even_lane).astype(BF16)
                if ci == 4:
                    k_ref[:, hh * HEAD_DIM:(hh + 1) * HEAD_DIM] = ph
                else:
                    c0 = (ci - 1) * chunk + hh * HEAD_DIM
                    q_ref[:, c0:c0 + HEAD_DIM] = ph


def _inproj(x, g, shift, scale, w_in_b, qg, kg, cos, sin):
    b, s, d = x.shape
    tm = INPROJ_TM
    row = lambda bi, i: (bi, i, 0)
    per_batch = lambda bi, i: (bi, 0, 0)
    const2 = lambda bi, i: (0, 0)
    return pl.pallas_call(
        _inproj_kernel,
        grid=(b, s // tm),
        in_specs=[
            pl.BlockSpec((None, tm, d), row),
            pl.BlockSpec((1, d), const2),
            pl.BlockSpec((None, 1, d), per_batch),
            pl.BlockSpec((None, 1, d), per_batch),
            pl.BlockSpec((d, IN_WIDTH), const2),
            pl.BlockSpec((1, HEAD_DIM), const2),
            pl.BlockSpec((1, HEAD_DIM), const2),
            pl.BlockSpec((tm, HEAD_DIM), lambda bi, i: (i, 0)),
            pl.BlockSpec((tm, HEAD_DIM), lambda bi, i: (i, 0)),
        ],
        out_specs=[
            pl.BlockSpec((None, tm, FOURIER_WIDTH), row),
            pl.BlockSpec((None, tm, ATTN_WIDTH), row),
            pl.BlockSpec((None, tm, KV_WIDTH), row),
            pl.BlockSpec((None, tm, KV_WIDTH), row),
        ],
        out_shape=[
            jax.ShapeDtypeStruct((b, s, FOURIER_WIDTH), BF16),
            jax.ShapeDtypeStruct((b, s, ATTN_WIDTH), BF16),
            jax.ShapeDtypeStruct((b, s, KV_WIDTH), BF16),
            jax.ShapeDtypeStruct((b, s, KV_WIDTH), BF16),
        ],
        compiler_params=pltpu.CompilerParams(
            dimension_semantics=("arbitrary", "arbitrary"), vmem_limit_bytes=VMEM_LIMIT_BYTES),
        name="inproj",
    )(x, g, shift, scale, w_in_b, qg, kg, cos, sin)


def _fourier_kernel(cs_ref, ss_ref, f_ref, cc_ref, w_ref, g_ref, o_ref):
    f = f_ref[...]
    a = jnp.dot(cs_ref[...], f, preferred_element_type=F32)
    bm = jnp.dot(ss_ref[...], f, preferred_element_type=F32)
    cc = cc_ref[...]
    outs = []
    for gi in range(FOURIER_GROUPS):
        sl = slice(gi * FOURIER_GROUP_DIM, (gi + 1) * FOURIER_GROUP_DIM)
        ab = jnp.concatenate([a[:, sl], bm[:, sl]], axis=1).astype(BF16)
        fm = jnp.dot(ab, cc, preferred_element_type=F32)
        outs.append(jnp.dot(fm.astype(BF16), w_ref[gi].astype(BF16), preferred_element_type=F32))
    fo = jnp.concatenate(outs, axis=1)
    o_ref[...] = _rms(fo, g_ref[...]).astype(BF16)


def _fourier(f, cs, ss, cc, w_fourier, g):
    b, s, fw = f.shape
    tr = FOURIER_TR
    return pl.pallas_call(
        _fourier_kernel,
        grid=(b, s // tr),
        in_specs=[
            pl.BlockSpec((tr, s), lambda bi, i: (i, 0)),
            pl.BlockSpec((tr, s), lambda bi, i: (i, 0)),
            pl.BlockSpec((None, s, fw), lambda bi, i: (bi, 0, 0)),
            pl.BlockSpec((2 * FOURIER_GROUP_DIM, FOURIER_GROUP_DIM), lambda bi, i: (0, 0)),
            pl.BlockSpec((FOURIER_GROUPS, FOURIER_GROUP_DIM, FOURIER_GROUP_DIM), lambda bi, i: (0, 0, 0)),
            pl.BlockSpec((1, fw), lambda bi, i: (0, 0)),
        ],
        out_specs=pl.BlockSpec((None, tr, fw), lambda bi, i: (bi, i, 0)),
        out_shape=jax.ShapeDtypeStruct((b, s, fw), BF16),
        compiler_params=pltpu.CompilerParams(
            dimension_semantics=("arbitrary", "arbitrary"), vmem_limit_bytes=VMEM_LIMIT_BYTES),
        name="fourier",
    )(cs, ss, f, cc, w_fourier, g)


def _attn_kernel(q_ref, k_ref, v_ref, g_ref, o_ref, ao_ref):
    tq = q_ref.shape[0]

    def scores(kv):
        kh = k_ref[:, kv * HEAD_DIM:(kv + 1) * HEAD_DIM]
        c0 = kv * GQA_GROUP * HEAD_DIM
        qh = jnp.concatenate(
            [q_ref[:, c0 + gi * HEAD_DIM:c0 + (gi + 1) * HEAD_DIM] for gi in range(GQA_GROUP)], axis=0)
        return lax.dot_general(qh, kh, (((1,), (1,)), ((), ())), preferred_element_type=F32)

    def attend(kv, sc):
        vh = v_ref[:, kv * HEAD_DIM:(kv + 1) * HEAD_DIM]
        c0 = kv * GQA_GROUP * HEAD_DIM
        m = jnp.max(sc, axis=-1, keepdims=True)
        p = jnp.exp2(sc - m)
        l = jnp.sum(p, axis=-1, keepdims=True)
        o = jnp.dot(p.astype(BF16), vh, preferred_element_type=F32) / l
        for gi in range(GQA_GROUP):
            ao_ref[:, c0 + gi * HEAD_DIM:c0 + (gi + 1) * HEAD_DIM] = o[gi * tq:(gi + 1) * tq]

    pending = [scores(kv) for kv in range(ATTN_LOOKAHEAD)]
    for kv in range(N_KV_HEADS):
        if kv + ATTN_LOOKAHEAD < N_KV_HEADS:
            pending.append(scores(kv + ATTN_LOOKAHEAD))
        attend(kv, pending.pop(0))
    o_ref[...] = _rms(ao_ref[...], g_ref[...]).astype(BF16)


def _attn(q, k, v, g):
    b, s, _ = q.shape
    tq = ATTN_TQ
    return pl.pallas_call(
        _attn_kernel,
        grid=(b, s // tq),
        in_specs=[
            pl.BlockSpec((None, tq, ATTN_WIDTH), lambda bi, i: (bi, i, 0)),
            pl.BlockSpec((None, s, KV_WIDTH), lambda bi, i: (bi, 0, 0)),
            pl.BlockSpec((None, s, KV_WIDTH), lambda bi, i: (bi, 0, 0)),
            pl.BlockSpec((1, ATTN_WIDTH), lambda bi, i: (0, 0)),
        ],
        out_specs=pl.BlockSpec((None, tq, ATTN_WIDTH), lambda bi, i: (bi, i, 0)),
        out_shape=jax.ShapeDtypeStruct((b, s, ATTN_WIDTH), BF16),
        scratch_shapes=[pltpu.VMEM((tq, ATTN_WIDTH), F32)],
        compiler_params=pltpu.CompilerParams(
            dimension_semantics=("arbitrary", "arbitrary"), vmem_limit_bytes=VMEM_LIMIT_BYTES),
        name="attn",
    )(q, k, v, g)


def _split_bf16(a):
    hi = a.astype(BF16)
    lo = (a - hi.astype(F32)).astype(BF16)
    return hi, lo


def _pack_rows(val, dst_ref):
    n = val.shape[0]
    bits = lax.bitcast_convert_type(val.astype(BF16).astype(F32), U32)
    packed = bits[:, :PACK_HALF] | (bits[:, PACK_HALF:] >> 16)
    for a in range(PACK_SLABS):
        dst_ref[pl.ds(a, n, stride=PACK_SLABS), :] = packed[:, a * LANES:(a + 1) * LANES]


def _unpack_slab(src_ref, a, n):
    word = src_ref[pl.ds(a, n, stride=PACK_SLABS), :]
    hi = lax.bitcast_convert_type(word & jnp.uint32(0xFFFF0000), F32)
    lo = lax.bitcast_convert_type(word << 16, F32)
    return hi, lo


def _outproj_kernel(fo_ref, ao_ref, w_ref, x_ref, gpost_ref, gate_ref, gpre_ref, shift_ref, scale_ref,
                    wr_hi_ref, wr_lo_ref, br_ref, x1_ref, h2p_ref, logit_ref):
    chain_rows = [pl.ds(c * OUTPROJ_CHAIN_ROWS, OUTPROJ_CHAIN_ROWS) for c in range(OUTPROJ_CHAINS)]
    mixes = []
    for rows in chain_rows:
        mix = jnp.dot(fo_ref[rows, :], w_ref[:FOURIER_WIDTH, :], preferred_element_type=F32)
        mixes.append(mix + jnp.dot(ao_ref[rows, :], w_ref[FOURIER_WIDTH:, :], preferred_element_type=F32))
    for c, (rows, mix) in enumerate(zip(chain_rows, mixes)):
        x1 = x_ref[rows, :] + gate_ref[...] * _rms(mix, gpost_ref[...])
        x1_ref[rows, :] = x1
        h2 = _rms(x1, gpre_ref[...]) * (1.0 + scale_ref[...]) + shift_ref[...]
        _pack_rows(h2, h2p_ref.at[pl.ds(c * OUTPROJ_CHAIN_ROWS * PACK_SLABS, OUTPROJ_CHAIN_ROWS * PACK_SLABS)])
        h_hi, h_lo = _split_bf16(h2)
        lg = jnp.dot(h_hi, wr_hi_ref[...], preferred_element_type=F32)
        lg = lg + jnp.dot(h_hi, wr_lo_ref[...], preferred_element_type=F32)
        lg = lg + jnp.dot(h_lo, wr_hi_ref[...], preferred_element_type=F32)
        logit_ref[rows, :] = lg + br_ref[...]


def _outproj(fo, ao, w_out_b, x, gpost, gate, gpre, shift, scale, wr_hi, wr_lo, br):
    b, s, d = x.shape
    tm = OUTPROJ_TM
    row = lambda bi, i: (bi, i, 0)
    per_batch = lambda bi, i: (bi, 0, 0)
    const2 = lambda bi, i: (0, 0)
    return pl.pallas_call(
        _outproj_kernel,
        grid=(b, s // tm),
        in_specs=[
            pl.BlockSpec((None, tm, FOURIER_WIDTH), row),
            pl.BlockSpec((None, tm, ATTN_WIDTH), row),
            pl.BlockSpec((d, d), const2),
            pl.BlockSpec((None, tm, d), row),
            pl.BlockSpec((1, d), const2),
            pl.BlockSpec((None, 1, d), per_batch),
            pl.BlockSpec((1, d), const2),
            pl.BlockSpec((None, 1, d), per_batch),
            pl.BlockSpec((None, 1, d), per_batch),
            pl.BlockSpec((d, ROUTER_PAD), const2),
            pl.BlockSpec((d, ROUTER_PAD), const2),
            pl.BlockSpec((1, ROUTER_PAD), const2),
        ],
        out_specs=[
            pl.BlockSpec((None, tm, d), row),
            pl.BlockSpec((tm * PACK_SLABS, LANES), lambda bi, i: (bi * (s // tm) + i, 0)),
            pl.BlockSpec((None, tm, ROUTER_PAD), row),
        ],
        out_shape=[
            jax.ShapeDtypeStruct((b, s, d), F32),
            jax.ShapeDtypeStruct((b * s * PACK_SLABS, LANES), U32),
            jax.ShapeDtypeStruct((b, s, ROUTER_PAD), F32),
        ],
        compiler_params=pltpu.CompilerParams(
            dimension_semantics=("arbitrary", "arbitrary"), vmem_limit_bytes=VMEM_LIMIT_BYTES),
        name="outproj",
    )(fo, ao, w_out_b, x, gpost, gate, gpre, shift, scale, wr_hi, wr_lo, br)


def _tile_rows(row, n_rows=1):
    return pl.ds(pl.multiple_of(row * PACK_SLABS, PACK_SLABS), n_rows * PACK_SLABS)


def _dispatch_kernel(dest_ref, fill_lo_ref, fill_hi_ref, used_tiles_ref, h_ref, xs_hbm, zeros, sem, fill_sem):
    step = pl.program_id(0)
    base = step * (DISPATCH_TT * TOP_K)

    def row_copy(i, kk):
        return pltpu.make_async_copy(
            h_ref.at[_tile_rows(i)], xs_hbm.at[_tile_rows(dest_ref[base + i * TOP_K + kk])], sem)

    def start(i, c):
        for kk in range(TOP_K):
            row_copy(i, kk).start(priority=kk % 2)
        return c

    def wait(i, c):
        for kk in range(TOP_K):
            row_copy(i, kk).wait()
        return c

    lax.fori_loop(0, DISPATCH_TT, start, 0, unroll=4)

    @pl.when(step == 0)
    def _fill():
        zeros[...] = jnp.zeros(zeros.shape, U32)
        n_tiles = xs_hbm.shape[0] // (MOE_TM * PACK_SLABS)

        def pad_chunks(e, fn):
            lo = fill_lo_ref[e]
            length = fill_hi_ref[e] - lo
            for bit in PAD_CHUNKS:
                @pl.when((length & bit) != 0)
                def _():
                    row = lo + (length & ~(2 * bit - 1))
                    fn(pltpu.make_async_copy(
                        zeros.at[pl.ds(0, bit * PACK_SLABS)], xs_hbm.at[_tile_rows(row, bit)], fill_sem))

        def tail_copy(t):
            return pltpu.make_async_copy(zeros, xs_hbm.at[_tile_rows(t * MOE_TM, MOE_TM)], fill_sem)

        def start_e(e, c):
            pad_chunks(e, lambda cp: cp.start())
            return c

        def wait_e(e, c):
            pad_chunks(e, lambda cp: cp.wait())
            return c

        def start_t(t, c):
            tail_copy(t).start()
            return c

        def wait_t(t, c):
            tail_copy(t).wait()
            return c

        lax.fori_loop(0, N_EXPERTS, start_e, 0)
        lax.fori_loop(used_tiles_ref[0], n_tiles, start_t, 0)
        lax.fori_loop(0, N_EXPERTS, wait_e, 0)
        lax.fori_loop(used_tiles_ref[0], n_tiles, wait_t, 0)

    lax.fori_loop(0, DISPATCH_TT, wait, 0, unroll=4)


def _dispatch(dest, fill_lo, fill_hi, used_tiles, h2p, n_rows):
    n_tok = h2p.shape[0] // PACK_SLABS
    return pl.pallas_call(
        _dispatch_kernel,
        grid_spec=pltpu.PrefetchScalarGridSpec(
            num_scalar_prefetch=4,
            grid=(n_tok // DISPATCH_TT,),
            in_specs=[pl.BlockSpec((DISPATCH_TT * PACK_SLABS, LANES), lambda i, *_: (i, 0))],
            out_specs=pl.BlockSpec(memory_space=pl.ANY),
            scratch_shapes=[
                pltpu.VMEM((MOE_TM * PACK_SLABS, LANES), U32),
                pltpu.SemaphoreType.DMA,
                pltpu.SemaphoreType.DMA,
            ],
        ),
        out_shape=jax.ShapeDtypeStruct((n_rows * PACK_SLABS, LANES), U32),
        compiler_params=pltpu.CompilerParams(
            dimension_semantics=("arbitrary",), vmem_limit_bytes=VMEM_LIMIT_BYTES),
        name="dispatch",
    )(dest, fill_lo, fill_hi, used_tiles, h2p)


def _experts_kernel(item_e_ref, item_row0_ref, item_nt_ref, item_nz_ref,
                    xs_hbm, wg_hbm, wu_hbm, wd_hbm, bg_ref, bu_ref, bd_ref, out_hbm,
                    stage, x_b, acc, wf_g0, wf_u0, wf_d0, wf_g1, wf_u1, wf_d1,
                    wb_g0, wb_u0, wb_d0, wb_g1, wb_u1, wb_d1, x_sems, o_sems, w_sems):
    w = pl.program_id(0)
    nt = item_nt_ref[w]
    nz = item_nz_ref[w]
    e = item_e_ref[w]
    row0 = item_row0_ref[w]
    w_prev = jnp.maximum(w - 1, 0)
    row0_prev, nt_prev = item_row0_ref[w_prev], item_nt_ref[w_prev]
    e_next, row0_next, nt_next = item_e_ref[w + 1], item_row0_ref[w + 1], item_nt_ref[w + 1]
    has_next = nt_next > 0
    slot = w & 1
    wf = ((wf_g0, wf_u0, wf_d0), (wf_g1, wf_u1, wf_d1))
    wb = ((wb_g0, wb_u0, wb_d0), (wb_g1, wb_u1, wb_d1))

    def w_copies(ee, j, ws):
        col = pl.ds(pl.multiple_of(j * MOE_TF, MOE_TF), MOE_TF)
        f_g, f_u, f_d = wf[ws]
        return (pltpu.make_async_copy(wg_hbm.at[ee, :, col], f_g, w_sems.at[ws, 0]),
                pltpu.make_async_copy(wu_hbm.at[ee, :, col], f_u, w_sems.at[ws, 1]),
                pltpu.make_async_copy(wd_hbm.at[ee, col, :], f_d, w_sems.at[ws, 2]))

    def w_start(ee, j, ws):
        for cp in w_copies(ee, j, ws):
            cp.start()

    def w_wait(ee, j, ws):
        for cp in w_copies(ee, j, ws):
            cp.wait()

    def w_cast(ws):
        for f_ref, b_ref in zip(wf[ws], wb[ws]):
            b_ref[...] = f_ref[...].astype(BF16)

    def x_copy(r0, t, ss):
        return pltpu.make_async_copy(
            xs_hbm.at[_tile_rows(r0 + t * MOE_TM, MOE_TM)], stage.at[ss, t], x_sems.at[ss, t])

    def out_copy(r0, t, ss):
        return pltpu.make_async_copy(
            stage.at[ss, t], out_hbm.at[_tile_rows(r0 + t * MOE_TM, MOE_TM)], o_sems.at[ss, t])

    def for_tiles(n, fn):
        for t in range(MOE_R):
            @pl.when(t < n)
            def _():
                fn(t)

    def gate_up(t, j, ws):
        b_g, b_u, _ = wb[ws]
        x = x_b[t]
        g = jnp.dot(x, b_g[...], preferred_element_type=F32) + bg_ref[j]
        u = jnp.dot(x, b_u[...], preferred_element_type=F32) + bu_ref[j]
        g = jnp.minimum(g, SWIGLU_LIMIT)
        u = jnp.clip(u, -SWIGLU_LIMIT, SWIGLU_LIMIT)
        return ((u + 1.0) * (g * jax.nn.sigmoid(SWIGLU_ALPHA * g))).astype(BF16)

    def down(t, hidden, ws):
        acc[t] += jnp.dot(hidden, wb[ws][2][...], preferred_element_type=F32)

    def unpack_tile(t):
        for a in range(PACK_SLABS):
            hi, lo = _unpack_slab(stage.at[slot, t], a, MOE_TM)
            x_b[t, :, a * LANES:(a + 1) * LANES] = hi.astype(BF16)
            x_b[t, :, PACK_HALF + a * LANES:PACK_HALF + (a + 1) * LANES] = lo.astype(BF16)

    def store_tile(t, rows):
        _pack_rows(rows, stage.at[slot, t])
        out_copy(row0, t, slot).start()

    def ff_step(j, ws, first=False, last=False):
        @pl.when(j + 2 < MOE_N_FF)
        def _():
            w_start(e, j + 2, ws)

        @pl.when(jnp.logical_and(j + 2 >= MOE_N_FF, has_next))
        def _():
            w_start(e_next, j + 2 - MOE_N_FF, ws)

        @pl.when(j + 1 < MOE_N_FF)
        def _():
            w_wait(e, j + 1, 1 - ws)

        @pl.when(jnp.logical_and(j + 1 >= MOE_N_FF, has_next))
        def _():
            w_wait(e_next, 0, 1 - ws)

        w_cast(1 - ws)

        done = 0
        for size in MOE_GROUPS:
            def group(i, c, size=size, done=done):
                ts = [done + size * i + k for k in range(size)]
                hs = []
                for t in ts:
                    if first:
                        unpack_tile(t)
                    hs.append(gate_up(t, j, ws))
                for t, h in zip(ts, hs):
                    if first:
                        acc[t] = bd_ref[...] + jnp.dot(h, wb[ws][2][...], preferred_element_type=F32)
                    elif last:
                        store_tile(t, acc[t] + jnp.dot(h, wb[ws][2][...], preferred_element_type=F32))
                    else:
                        down(t, h, ws)
                return c

            n_group = (nt - done) // size
            lax.fori_loop(0, n_group, group, 0)
            done = done + n_group * size

    @pl.when(nt > 0)
    def _active():
        @pl.when(w == 0)
        def _prologue():
            w_start(e, 0, 0)
            w_start(e, 1, 1)
            for_tiles(nt, lambda t: x_copy(row0, t, 0).start())
            w_wait(e, 0, 0)
            w_cast(0)

        for_tiles(nt, lambda t: x_copy(row0, t, slot).wait())
        n_pair = MOE_N_FF // 2

        def ff_pair(jj, c):
            pl.when(jj == 0)(lambda: ff_step(2 * jj, 0, first=True))
            pl.when(jj > 0)(lambda: ff_step(2 * jj, 0))

            @pl.when(jj == 0)
            def _():
                @pl.when(w > 0)
                def _():
                    for_tiles(nt_prev, lambda t: out_copy(row0_prev, t, 1 - slot).wait())

                @pl.when(has_next)
                def _():
                    for_tiles(nt_next, lambda t: x_copy(row0_next, t, 1 - slot).start())

            pl.when(jj < n_pair - 1)(lambda: ff_step(2 * jj + 1, 1))
            pl.when(jj == n_pair - 1)(lambda: ff_step(2 * jj + 1, 1, last=True))
            return c

        lax.fori_loop(0, n_pair, ff_pair, 0)

        @pl.when(jnp.logical_not(has_next))
        def _():
            for_tiles(nt, lambda t: out_copy(row0, t, slot).wait())

    @pl.when(nz > 0)
    def _zero_tail():
        def zero_tile(t):
            stage[0, t] = jnp.zeros(stage.shape[2:], U32)
            out_copy(row0, t, 0).start()

        for_tiles(nz, zero_tile)
        for_tiles(nz, lambda t: out_copy(row0, t, 0).wait())


def _experts(item_e, item_row0, item_nt, item_nz, xs, w_gate, b_gate, w_up, b_up, w_down, b_down, n_items):
    d = D_MODEL
    w_shapes = [(d, MOE_TF), (d, MOE_TF), (MOE_TF, d)]
    by_expert = lambda w, ie, r0, nt, nz: (ie[w], 0, 0, 0)
    return pl.pallas_call(
        _experts_kernel,
        grid_spec=pltpu.PrefetchScalarGridSpec(
            num_scalar_prefetch=4,
            grid=(n_items,),
            in_specs=[
                pl.BlockSpec(memory_space=pl.ANY),
                pl.BlockSpec(memory_space=pl.ANY),
                pl.BlockSpec(memory_space=pl.ANY),
                pl.BlockSpec(memory_space=pl.ANY),
                pl.BlockSpec((None, MOE_N_FF, 1, MOE_TF), by_expert),
                pl.BlockSpec((None, MOE_N_FF, 1, MOE_TF), by_expert),
                pl.BlockSpec((None, 1, d), lambda w, ie, r0, nt, nz: (ie[w], 0, 0)),
            ],
            out_specs=pl.BlockSpec(memory_space=pl.ANY),
            scratch_shapes=[
                pltpu.VMEM((2, MOE_R, MOE_TM * PACK_SLABS, LANES), U32),
                pltpu.VMEM((MOE_R, MOE_TM, d), BF16),
                pltpu.VMEM((MOE_R, MOE_TM, d), F32),
                *[pltpu.VMEM(shape, F32) for shape in w_shapes * 2],
                *[pltpu.VMEM(shape, BF16) for shape in w_shapes * 2],
                pltpu.SemaphoreType.DMA((2, MOE_R)),
                pltpu.SemaphoreType.DMA((2, MOE_R)),
                pltpu.SemaphoreType.DMA((2, 3)),
            ],
        ),
        out_shape=jax.ShapeDtypeStruct(xs.shape, U32),
        compiler_params=pltpu.CompilerParams(
            dimension_semantics=("arbitrary",), vmem_limit_bytes=VMEM_LIMIT_BYTES),
        name="experts",
    )(item_e, item_row0, item_nt, item_nz, xs, w_gate, w_up, w_down,
      b_gate.reshape(N_EXPERTS, MOE_N_FF, 1, MOE_TF), b_up.reshape(N_EXPERTS, MOE_N_FF, 1, MOE_TF),
      b_down.reshape(N_EXPERTS, 1, d))


def _combine_kernel(dest_ref, rows_hbm, gates_ref, x1_ref, gpost_ref, gate_ref, o_ref, buf, sems):
    tt = COMBINE_TT
    step = pl.program_id(0) * pl.num_programs(1) + pl.program_id(1)
    n_steps = pl.num_programs(0) * pl.num_programs(1)
    slot = step & 1

    def row_copy(s, ss, i, kk):
        row = dest_ref[(s * tt + i) * TOP_K + kk]
        return pltpu.make_async_copy(rows_hbm.at[_tile_rows(row)], buf.at[ss, kk, _tile_rows(i)], sems.at[ss])

    def start_step(s, ss):
        def start(i, c):
            for kk in range(TOP_K):
                row_copy(s, ss, i, kk).start(priority=kk % 2)
            return c

        lax.fori_loop(0, tt, start, 0, unroll=4)

    @pl.when(step == 0)
    def _():
        start_step(step, slot)

    @pl.when(step + 1 < n_steps)
    def _():
        start_step(step + 1, 1 - slot)

    def wait(i, c):
        for kk in range(TOP_K):
            row_copy(step, slot, i, kk).wait()
        return c

    lax.fori_loop(0, tt, wait, 0, unroll=4)
    gates = gates_ref[...]
    y_hi, y_lo = [], []
    for a in range(PACK_SLABS):
        acc_hi = acc_lo = None
        for kk in range(TOP_K):
            hi, lo = _unpack_slab(buf.at[slot, kk], a, tt)
            gk = gates[:, kk:kk + 1]
            acc_hi = gk * hi if acc_hi is None else acc_hi + gk * hi
            acc_lo = gk * lo if acc_lo is None else acc_lo + gk * lo
        y_hi.append(acc_hi)
        y_lo.append(acc_lo)
    y = jnp.concatenate(y_hi + y_lo, axis=1)
    o_ref[...] = x1_ref[...] + gate_ref[...] * _rms(y, gpost_ref[...])


def _combine(dest, rows, gates, x1, gpost, gate):
    b, s, d = x1.shape
    tt = COMBINE_TT
    row = lambda bi, i, dst: (bi, i, 0)
    return pl.pallas_call(
        _combine_kernel,
        grid_spec=pltpu.PrefetchScalarGridSpec(
            num_scalar_prefetch=1,
            grid=(b, s // tt),
            in_specs=[
                pl.BlockSpec(memory_space=pl.ANY),
                pl.BlockSpec((None, tt, TOP_K), row),
                pl.BlockSpec((None, tt, d), row),
                pl.BlockSpec((1, d), lambda bi, i, dst: (0, 0)),
                pl.BlockSpec((None, 1, d), lambda bi, i, dst: (bi, 0, 0)),
            ],
            out_specs=pl.BlockSpec((None, tt, d), row),
            scratch_shapes=[pltpu.VMEM((2, TOP_K, tt * PACK_SLABS, LANES), U32),
                            pltpu.SemaphoreType.DMA((2,))],
        ),
        out_shape=jax.ShapeDtypeStruct((b, s, d), F32),
        compiler_params=pltpu.CompilerParams(
            dimension_semantics=("arbitrary", "arbitrary"), vmem_limit_bytes=VMEM_LIMIT_BYTES),
        name="combine",
    )(dest, rows, gates, x1, gpost, gate)


def _rope_tables(seq_len):
    rows = seq_len // GRID_W
    row_idx = jnp.repeat(jnp.arange(rows, dtype=F32), GRID_W)
    col_idx = jnp.tile(jnp.arange(GRID_W, dtype=F32), rows)
    half = HEAD_DIM // 2
    inv_freq = 1.0 / (ROPE_THETA ** (jnp.arange(0, half, 2, dtype=F32) / half))
    ang = jnp.concatenate([row_idx[:, None] * inv_freq, col_idx[:, None] * inv_freq], axis=-1)
    cos = jnp.repeat(jnp.cos(ang), 2, axis=-1)
    sin = jnp.sin(ang)
    sin_signed = jnp.stack([-sin, sin], axis=-1).reshape(seq_len, HEAD_DIM)
    return cos, sin_signed


def _dft_tables(n):
    idx = np.arange(n, dtype=np.int64)
    ang = 2.0 * np.pi * ((idx[:, None] * idx[None, :]) % n).astype(np.float64) / n
    scale = 1.0 / np.sqrt(n)
    return np.cos(ang) * scale, np.sin(ang) * scale


def _topk_kernel(logit_ref, tri_ref, gate_ref, idx_ref, rank_ref, count_ref, base):
    step = pl.program_id(0)

    @pl.when(step == 0)
    def _():
        base[...] = jnp.zeros(base.shape, F32)

    lg = logit_ref[...]
    lane = lax.broadcasted_iota(jnp.int32, lg.shape, 1)
    v = jnp.where(lane < N_EXPERTS, lg, -jnp.inf)
    vals, firsts, hits = [], [], []
    for _ in range(TOP_K):
        m = jnp.max(v, axis=-1, keepdims=True)
        first = jnp.min(jnp.where(v == m, lane, ROUTER_PAD), axis=-1, keepdims=True)
        hit = lane == first
        vals.append(m)
        firsts.append(first)
        hits.append(hit)
        v = jnp.where(hit, -jnp.inf, v)
    ex = [jnp.exp(val - vals[0]) for val in vals]
    denom = ex[0]
    for e_k in ex[1:]:
        denom = denom + e_k
    chosen = hits[0]
    for hit in hits[1:]:
        chosen = jnp.logical_or(chosen, hit)
    onehot = chosen.astype(F32)
    before = jnp.dot(tri_ref[...], onehot.astype(BF16), preferred_element_type=F32) + base[...]
    gate_out = jnp.zeros(lg.shape, F32)
    idx_out = jnp.zeros(lg.shape, jnp.int32)
    rank_out = jnp.zeros(lg.shape, F32)
    for kk in range(TOP_K):
        sel = lane == kk
        gate_out = jnp.where(sel, ex[kk] / denom, gate_out)
        idx_out = jnp.where(sel, firsts[kk], idx_out)
        rank_out = jnp.where(sel, jnp.sum(jnp.where(hits[kk], before, 0.0), axis=-1, keepdims=True), rank_out)
    gate_ref[...] = gate_out
    idx_ref[...] = idx_out
    rank_ref[...] = rank_out.astype(jnp.int32)
    base[...] = base[...] + jnp.sum(onehot, axis=0, keepdims=True)
    count_ref[...] = base[...].astype(jnp.int32)


def _topk(logits):
    n_tok = logits.shape[0]
    tb = ROUTE_TB
    tri = jnp.asarray(np.tril(np.ones((tb, tb), np.float32), -1), dtype=BF16)
    blk = pl.BlockSpec((tb, ROUTER_PAD), lambda i: (i, 0))
    return pl.pallas_call(
        _topk_kernel,
        grid=(n_tok // tb,),
        in_specs=[blk, pl.BlockSpec((tb, tb), lambda i: (0, 0))],
        out_specs=[blk, blk, blk, pl.BlockSpec((1, ROUTER_PAD), lambda i: (0, 0))],
        out_shape=[
            jax.ShapeDtypeStruct((n_tok, ROUTER_PAD), F32),
            jax.ShapeDtypeStruct((n_tok, ROUTER_PAD), jnp.int32),
            jax.ShapeDtypeStruct((n_tok, ROUTER_PAD), jnp.int32),
            jax.ShapeDtypeStruct((1, ROUTER_PAD), jnp.int32),
        ],
        scratch_shapes=[pltpu.VMEM((1, ROUTER_PAD), F32)],
        compiler_params=pltpu.CompilerParams(
            dimension_semantics=("arbitrary",), vmem_limit_bytes=VMEM_LIMIT_BYTES),
        name="topk",
    )(logits, tri)


def _plan_kernel(count_ref, e_ref, rank_ref, dest_ref, fill_lo_ref, fill_hi_ref, used_ref,
                 item_e_ref, item_row0_ref, item_nt_ref, item_nz_ref, pad_start, *, n_tiles, n_items):
    def per_expert(e, carry):
        start, w = carry
        count = count_ref[e]
        tiles = (count + (MOE_TM - 1)) // MOE_TM
        pad_start[e] = start
        fill_lo_ref[e] = start + count
        fill_hi_ref[e] = start + tiles * MOE_TM

        def per_item(k, w):
            item_e_ref[w] = e
            item_row0_ref[w] = start + k * MOE_RS
            item_nt_ref[w] = jnp.minimum(MOE_R, tiles - k * MOE_R)
            item_nz_ref[w] = 0
            return w + 1

        w = lax.fori_loop(0, (tiles + (MOE_R - 1)) // MOE_R, per_item, w)
        return start + tiles * MOE_TM, w

    used_rows, total = lax.fori_loop(0, N_EXPERTS, per_expert, (jnp.int32(0), jnp.int32(0)))
    used_tiles = used_rows // MOE_TM
    used_ref[0] = used_tiles
    last_e = item_e_ref[jnp.maximum(total - 1, 0)]

    def per_filler(w, c):
        tile0 = used_tiles + (w - total) * MOE_R
        item_e_ref[w] = last_e
        item_row0_ref[w] = jnp.minimum(tile0, n_tiles - 1) * MOE_TM
        item_nt_ref[w] = 0
        item_nz_ref[w] = jnp.clip(n_tiles - tile0, 0, MOE_R)
        return c

    lax.fori_loop(total, n_items + 1, per_filler, 0)

    e = e_ref[...]
    dest = rank_ref[...]
    for ee in range(N_EXPERTS):
        dest = dest + jnp.where(e == ee, pad_start[ee], 0)
    dest_ref[...] = dest


def _route(logits, n_tok):
    gates, top_idx, rank, counts = _topk(logits)
    n_tiles = pl.cdiv(n_tok * TOP_K + N_EXPERTS * (MOE_TM - 1), MOE_TM)
    n_rows = n_tiles * MOE_TM
    n_items = N_EXPERTS + n_tiles // MOE_R
    flat = (n_tok * TOP_K // LANES, LANES)
    smem = pl.BlockSpec(memory_space=pltpu.SMEM)
    vmem = pl.BlockSpec(flat, lambda: (0, 0))
    i32 = lambda n: jax.ShapeDtypeStruct((n,), jnp.int32)
    dest, fill_lo, fill_hi, used_tiles, item_e, item_row0, item_nt, item_nz = pl.pallas_call(
        functools.partial(_plan_kernel, n_tiles=n_tiles, n_items=n_items),
        in_specs=[smem, vmem, vmem],
        out_specs=[vmem] + [smem] * 7,
        out_shape=[jax.ShapeDtypeStruct(flat, jnp.int32), i32(N_EXPERTS), i32(N_EXPERTS), i32(1)]
        + [i32(n_items + 1)] * 4,
        scratch_shapes=[pltpu.SMEM((N_EXPERTS,), jnp.int32)],
        name="plan",
    )(counts.reshape(ROUTER_PAD), top_idx[:, :TOP_K].reshape(flat), rank[:, :TOP_K].reshape(flat))
    fill = (fill_lo, fill_hi, used_tiles)
    items = (item_e, item_row0, item_nt, item_nz)
    return gates[:, :TOP_K], dest.reshape(-1), fill, items, n_rows, n_items


def kernel(x, c, w_ada, b_ada, g_pre_mix, w_in, w_fourier, q_norm_g, k_norm_g, g_fourier_out, g_attn_out,
           w_out, g_post_mix, g_pre_ffn, w_router, b_router, w_gate, b_gate, w_up, b_up, w_down, b_down,
           g_post_ffn):
    b, s, d = x.shape
    n_tok = b * s
    depth = w_ada.shape[0]
    cos, sin_signed = _rope_tables(s)
    cs_np, ss_np = _dft_tables(s)
    cs = jnp.asarray(cs_np, dtype=BF16)
    ss = jnp.asarray(ss_np, dtype=BF16)
    cc_np, sc_np = _dft_tables(FOURIER_GROUP_DIM)
    cc = jnp.asarray(np.concatenate([cc_np, -sc_np], axis=0), dtype=BF16)
    c_pad = jnp.pad(c, ((0, 8 - b), (0, 0)))
    row2 = lambda a: a.reshape(1, -1)

    for l in range(depth):
        mod = _ada(c_pad, w_ada[l], row2(b_ada[l]))[:b].reshape(b, N_MOD, 1, d)
        shift_m, scale_m, gate_m = mod[:, 0], mod[:, 1], mod[:, 2]
        shift_f, scale_f, gate_f = mod[:, 3], mod[:, 4], mod[:, 5]

        f, q, k, v = _inproj(
            x, row2(g_pre_mix[l]), shift_m, scale_m, w_in[l].astype(BF16),
            row2(q_norm_g[l] * (HEAD_DIM ** -0.5 * LOG2_E)), row2(k_norm_g[l]), cos, sin_signed)
        fo = _fourier(f, cs, ss, cc, w_fourier[l], row2(g_fourier_out[l]))
        ao = _attn(q, k, v, row2(g_attn_out[l]))

        wr = jnp.pad(w_router[l], ((0, 0), (0, ROUTER_PAD - N_EXPERTS)))
        wr_hi = wr.astype(BF16)
        wr_lo = (wr - wr_hi.astype(F32)).astype(BF16)
        br = jnp.pad(b_router[l], (0, ROUTER_PAD - N_EXPERTS)).reshape(1, ROUTER_PAD)
        x1, h2p, logits = _outproj(
            fo, ao, w_out[l].astype(BF16), x, row2(g_post_mix[l]), gate_m, row2(g_pre_ffn[l]),
            shift_f, scale_f, wr_hi, wr_lo, br)

        gates, dest, fill, items, n_rows, n_items = _route(logits.reshape(n_tok, ROUTER_PAD), n_tok)
        xs = _dispatch(dest, *fill, h2p, n_rows)
        rows = _experts(
            *items, xs,
            w_gate[l], b_gate[l], w_up[l], b_up[l], w_down[l], b_down[l], n_items)
        x = _combine(dest, rows, gates.reshape(b, s, TOP_K), x1, row2(g_post_ffn[l]), gate_f)
    return x
```
